```python
import math
import jax, jax.numpy as jnp
from jax import lax
import numpy as np

D_MODEL = 1024
BATCH = 4
SEQ = 8192
DEPTH = 1
DEC_BATCH = 128
DEC_SEQ = 8
PAST_LEN = 16384
PAGE_SIZE = 128

GLA_HEADS = 4
GLA_DK = 128
GLA_DV = 256
GLA_RANK = 16
GLA_TAU = 16.0
GLA_CHUNK = 64
SWA_HEADS = 16
SWA_KV_HEADS = 2
SWA_GROUP = SWA_HEADS // SWA_KV_HEADS
SWA_HEAD_DIM = 64
WINDOW = 128
SWA_BLOCK = WINDOW
PEER_HEADS = 8
PEER_NKEYS = 128
PEER_N_EXPERTS = PEER_NKEYS * PEER_NKEYS
PEER_QDIM = 256
PEER_HALF = PEER_QDIM // 2
PEER_TOPK = 16
PEER_BLOCK = 256
EPS = 1e-6

GLA_QK = GLA_HEADS * GLA_DK
GLA_V = GLA_HEADS * GLA_DV
SWA_Q = SWA_HEADS * SWA_HEAD_DIM
SWA_KV = SWA_KV_HEADS * SWA_HEAD_DIM
IN_SIZES = (GLA_QK, GLA_QK, GLA_V, GLA_V, GLA_RANK, SWA_Q, SWA_KV, SWA_KV, D_MODEL, D_MODEL)
IN_WIDTH = sum(IN_SIZES)

kernel_name = "gla_swa_sink_peer_hybrid_step"


def rmsnorm(x, g):
    xf = x.astype(jnp.float32)
    y = xf * lax.rsqrt(jnp.mean(xf * xf, axis=-1, keepdims=True) + EPS) * g.astype(jnp.float32)
    return y.astype(x.dtype)


def gla_chunked(q, k, v, log_a, s0):
    B, T, H, _ = q.shape
    dv = v.shape[-1]
    c = math.gcd(T, GLA_CHUNK)
    n = T // c

    def to_chunks(t):
        return t.astype(jnp.float32).reshape(B, n, c, H, t.shape[-1]).transpose(1, 0, 3, 2, 4)

    qc, kc, vc, ac = to_chunks(q), to_chunks(k), to_chunks(v), to_chunks(log_a)
    causal = jnp.tril(jnp.ones((c, c), dtype=bool))[:, :, None]

    def step(S, inp):
        qi, ki, vi, ai = inp
        b = jnp.cumsum(ai, axis=-2)
        o_inter = jnp.einsum('bhid,bhde->bhie', qi * jnp.exp(b), S)
        rel = jnp.where(causal, b[:, :, :, None, :] - b[:, :, None, :, :], -jnp.inf)
        scores = jnp.einsum('bhid,bhjd,bhijd->bhij', qi, ki, jnp.exp(rel))
        o = o_inter + jnp.einsum('bhij,bhje->bhie', scores, vi)
        b_last = b[:, :, -1:, :]
        S_new = jnp.exp(b_last[:, :, 0, :])[..., None] * S + jnp.einsum(
            'bhjd,bhje->bhde', ki * jnp.exp(b_last - b), vi)
        return S_new, o

    S, o = lax.scan(step, s0.astype(jnp.float32), (qc, kc, vc, ac))
    o = o.transpose(1, 0, 3, 2, 4).reshape(B, T, H, dv)
    return o, S


def banded_sink_attention(q, k, v, q_pos, k_pos, sinks):
    s = jnp.einsum('bnqhgd,bnkhd->bnhgqk', q, k).astype(jnp.float32) * (SWA_HEAD_DIM ** -0.5)
    diff = q_pos[:, :, None] - k_pos[:, None, :]
    mask = (diff >= 0) & (diff < WINDOW) & (k_pos[:, None, :] >= 0)
    s = jnp.where(mask[None, :, None, None], s, -jnp.inf)
    sink = sinks.astype(jnp.float32)[:, :, None, None]
    m = jnp.maximum(jnp.max(s, axis=-1, keepdims=True), sink)
    e = jnp.exp(s - m)
    p = e / (jnp.sum(e, axis=-1, keepdims=True) + jnp.exp(sink - m))
    return jnp.einsum('bnhgqk,bnkhd->bnqhgd', p.astype(v.dtype), v)


def swa_prompt(q, k, v, sinks):
    B, T = q.shape[:2]
    nb = T // SWA_BLOCK
    qb = q.reshape(B, nb, SWA_BLOCK, SWA_KV_HEADS, SWA_GROUP, SWA_HEAD_DIM)
    kb = k.reshape(B, nb, SWA_BLOCK, SWA_KV_HEADS, SWA_HEAD_DIM)
    vb = v.reshape(B, nb, SWA_BLOCK, SWA_KV_HEADS, SWA_HEAD_DIM)
    prev = lambda t: jnp.concatenate([jnp.zeros_like(t[:, :1]), t[:, :-1]], axis=1)
    kk = jnp.concatenate([prev(kb), kb], axis=2)
    vv = jnp.concatenate([prev(vb), vb], axis=2)
    pos = jnp.arange(T, dtype=jnp.int32).reshape(nb, SWA_BLOCK)
    k_pos = jnp.concatenate([pos - SWA_BLOCK, pos], axis=1)
    o = banded_sink_attention(qb, kk, vv, pos, k_pos, sinks).reshape(B, T, SWA_Q)
    L = min(WINDOW, T)
    return o, k[:, T - L:], v[:, T - L:]


def swa_sample(q, k, v, win_k, win_v, sinks):
    B, T = q.shape[:2]
    L = win_k.shape[1]
    kk = jnp.concatenate([win_k.astype(k.dtype), k], axis=1)
    vv = jnp.concatenate([win_v.astype(v.dtype), v], axis=1)
    q_pos = (PAST_LEN + jnp.arange(T, dtype=jnp.int32))[None]
    k_pos = (PAST_LEN - L + jnp.arange(L + T, dtype=jnp.int32))[None]
    o = banded_sink_attention(q[:, None], kk[:, None], vv[:, None], q_pos, k_pos, sinks)
    return o.reshape(B, T, SWA_Q), kk[:, -L:], vv[:, -L:]


def peer_ffn(xn, w_q, keys1, keys2, u, v):
    shp = xn.shape
    xt = xn.reshape(-1, D_MODEL)
    n = xt.shape[0]
    q = (xt @ w_q).astype(jnp.float32).reshape(n, PEER_HEADS, 2, PEER_HALF)
    s1 = jnp.einsum('nhd,hkd->nhk', q[:, :, 0], keys1.astype(jnp.float32))
    s2 = jnp.einsum('nhd,hkd->nhk', q[:, :, 1], keys2.astype(jnp.float32))
    v1, i1 = lax.top_k(s1, PEER_TOPK)
    v2, i2 = lax.top_k(s2, PEER_TOPK)
    cand = (v1[..., :, None] + v2[..., None, :]).reshape(n, PEER_HEADS, PEER_TOPK * PEER_TOPK)
    cand_idx = (i1[..., :, None] * PEER_NKEYS + i2[..., None, :]).reshape(n, PEER_HEADS, PEER_TOPK * PEER_TOPK)
    top_s, sel = lax.top_k(cand, PEER_TOPK)
    idx = jnp.take_along_axis(cand_idx, sel, axis=-1)
    g = jax.nn.softmax(top_s, axis=-1)
    hk = PEER_HEADS * PEER_TOPK
    pad = (-n) % PEER_BLOCK
    nb = (n + pad) // PEER_BLOCK
    xp = jnp.pad(xt, ((0, pad), (0, 0))).reshape(nb, PEER_BLOCK, D_MODEL)
    ip = jnp.pad(idx.reshape(n, hk), ((0, pad), (0, 0))).reshape(nb, PEER_BLOCK, hk)
    gp = jnp.pad(g.reshape(n, hk), ((0, pad), (0, 0))).astype(xt.dtype).reshape(nb, PEER_BLOCK, hk)

    def block(args):
        xb, ib, gb = args
        ub = jnp.take(u, ib, axis=0)
        act = jax.nn.gelu(jnp.einsum('pd,ped->pe', xb, ub), approximate=False)
        vb = jnp.take(v, ib, axis=0)
        return jnp.einsum('pe,ped->pd', gb * act, vb)

    out = lax.map(block, (xp, ip, gp))
    return out.reshape(-1, D_MODEL)[:n].reshape(shp)


def hybrid_layer(x, gla_s0, win_k, win_v, is_prompt, norm_mix, w_in, gla_w_decay, gla_b_decay,
                 gla_norm, swa_sinks, w_branch_a, w_branch_b, w_out, norm_ffn,
                 peer_w_q, peer_keys1, peer_keys2, peer_u, peer_v):
    B, T, _ = x.shape
    xn = rmsnorm(x, norm_mix)
    z = xn @ w_in
    splits = [int(s) for s in np.cumsum(IN_SIZES)[:-1]]
    gq, gk, gv, gg, glr, sq, sk, sv, gate_a, gate_b = jnp.split(z, splits, axis=-1)
    log_a = jax.nn.log_sigmoid((glr @ gla_w_decay + gla_b_decay).astype(jnp.float32)) / GLA_TAU
    o_a, s_new = gla_chunked(
        (gq * (GLA_DK ** -0.5)).reshape(B, T, GLA_HEADS, GLA_DK),
        gk.reshape(B, T, GLA_HEADS, GLA_DK),
        gv.reshape(B, T, GLA_HEADS, GLA_DV),
        log_a.reshape(B, T, GLA_HEADS, GLA_DK),
        gla_s0)
    o_a = rmsnorm(o_a, gla_norm).astype(x.dtype) * jax.nn.silu(gg.reshape(B, T, GLA_HEADS, GLA_DV))
    o_a = o_a.reshape(B, T, GLA_V)
    q = sq.reshape(B, T, SWA_KV_HEADS, SWA_GROUP, SWA_HEAD_DIM)
    k = sk.reshape(B, T, SWA_KV_HEADS, SWA_HEAD_DIM)
    v = sv.reshape(B, T, SWA_KV_HEADS, SWA_HEAD_DIM)
    sinks = swa_sinks.reshape(SWA_KV_HEADS, SWA_GROUP)
    if is_prompt:
        o_b, new_k, new_v = swa_prompt(q, k, v, sinks)
    else:
        o_b, new_k, new_v = swa_sample(q, k, v, win_k, win_v, sinks)
    merged = jax.nn.sigmoid(gate_a) * (o_a @ w_branch_a) + jax.nn.sigmoid(gate_b) * (o_b @ w_branch_b)
    h = x + merged @ w_out
    h = h + peer_ffn(rmsnorm(h, norm_ffn), peer_w_q, peer_keys1, peer_keys2, peer_u, peer_v)
    return h, new_k, new_v, s_new.astype(x.dtype)


def setup_inputs(seed: int = 0) -> dict:
    key = jax.random.key(seed)
    ks = jax.random.split(key, 24)
    f32 = jnp.float32
    nrm = lambda k, shp, s: jax.random.normal(k, shp, f32) * s
    win_buf = min(WINDOW, PAST_LEN)
    return {
        "x_prompt": nrm(ks[0], (BATCH, SEQ, D_MODEL), 1.0),
        "x_sample": nrm(ks[1], (DEC_BATCH, DEC_SEQ, D_MODEL), 1.0),
        "cache_win_k": nrm(ks[2], (DEPTH, DEC_BATCH, win_buf, SWA_KV_HEADS, SWA_HEAD_DIM), 1.0),
        "cache_win_v": nrm(ks[3], (DEPTH, DEC_BATCH, win_buf, SWA_KV_HEADS, SWA_HEAD_DIM), 1.0),
        "state_gla": nrm(ks[4], (DEPTH, DEC_BATCH, GLA_HEADS, GLA_DK, GLA_DV), 1.0),
        "norm_mix": 1.0 + nrm(ks[5], (DEPTH, D_MODEL), 0.02),
        "w_in": nrm(ks[6], (DEPTH, D_MODEL, IN_WIDTH), D_MODEL ** -0.5),
        "gla_w_decay": nrm(ks[7], (DEPTH, GLA_RANK, GLA_QK), GLA_RANK ** -0.5),
        "gla_b_decay": nrm(ks[8], (DEPTH, GLA_QK), 0.1),
        "gla_norm": 1.0 + nrm(ks[9], (DEPTH, GLA_DV), 0.02),
        "swa_sinks": nrm(ks[10], (DEPTH, SWA_HEADS), 0.5),
        "w_branch_a": nrm(ks[11], (DEPTH, GLA_V, D_MODEL), GLA_V ** -0.5),
        "w_branch_b": nrm(ks[12], (DEPTH, SWA_Q, D_MODEL), SWA_Q ** -0.5),
        "w_out": nrm(ks[13], (DEPTH, D_MODEL, D_MODEL), D_MODEL ** -0.5),
        "norm_ffn": 1.0 + nrm(ks[14], (DEPTH, D_MODEL), 0.02),
        "peer_w_q": nrm(ks[15], (DEPTH, D_MODEL, PEER_HEADS * PEER_QDIM), D_MODEL ** -0.5),
        "peer_keys1": nrm(ks[16], (DEPTH, PEER_HEADS, PEER_NKEYS, PEER_HALF), PEER_HALF ** -0.5),
        "peer_keys2": nrm(ks[17], (DEPTH, PEER_HEADS, PEER_NKEYS, PEER_HALF), PEER_HALF ** -0.5),
        "peer_u": nrm(ks[18], (DEPTH, PEER_N_EXPERTS, D_MODEL), D_MODEL ** -0.5),
        "peer_v": nrm(ks[19], (DEPTH, PEER_N_EXPERTS, D_MODEL), PEER_HEADS ** -0.5),
        "norm_final": 1.0 + nrm(ks[20], (D_MODEL,), 0.02),
    }


def reference(x_prompt, x_sample, cache_win_k, cache_win_v, state_gla, norm_mix, w_in, gla_w_decay,
              gla_b_decay, gla_norm, swa_sinks, w_branch_a, w_branch_b, w_out, norm_ffn,
              peer_w_q, peer_keys1, peer_keys2, peer_u, peer_v, norm_final):
    hp, hs = x_prompt, x_sample
    pk, pv, ps, sk, sv, ss = [], [], [], [], [], []
    for l in range(DEPTH):
        params = (norm_mix[l], w_in[l], gla_w_decay[l], gla_b_decay[l], gla_norm[l], swa_sinks[l],
                  w_branch_a[l], w_branch_b[l], w_out[l], norm_ffn[l],
                  peer_w_q[l], peer_keys1[l], peer_keys2[l], peer_u[l], peer_v[l])
        s0 = jnp.zeros((hp.shape[0], GLA_HEADS, GLA_DK, GLA_DV), jnp.float32)
        hp, k1, v1, s1 = hybrid_layer(hp, s0, None, None, True, *params)
        hs, k2, v2, s2 = hybrid_layer(hs, state_gla[l], cache_win_k[l], cache_win_v[l], False, *params)
        pk.append(k1); pv.append(v1); ps.append(s1)
        sk.append(k2); sv.append(v2); ss.append(s2)
    y_prompt = rmsnorm(hp, norm_final)
    y_sample = rmsnorm(hs, norm_final)
    return (y_prompt, y_sample, jnp.stack(pk), jnp.stack(pv), jnp.stack(ps),
            jnp.stack(sk), jnp.stack(sv), jnp.stack(ss))
```

```python
import functools

import jax
import jax.numpy as jnp
import numpy as np
from jax import lax
from jax.experimental import pallas as pl
from jax.experimental.pallas import tpu as pltpu

F32 = jnp.float32
BF16 = jnp.bfloat16

D_MODEL = 1024
GLA_HEADS = 4
GLA_DK = 128
GLA_DV = 256
GLA_RANK = 16
GLA_TAU = 16.0
SWA_HEADS = 16
SWA_KV_HEADS = 2
SWA_GROUP = SWA_HEADS // SWA_KV_HEADS
SWA_HEAD_DIM = 64
WINDOW = 128
PEER_HEADS = 8
PEER_NKEYS = 128
PEER_N_EXPERTS = PEER_NKEYS * PEER_NKEYS
PEER_QDIM = 256
PEER_HALF = PEER_QDIM // 2
PEER_TOPK = 16
EPS = 1e-6

GLA_QK = GLA_HEADS * GLA_DK
GLA_V = GLA_HEADS * GLA_DV
SWA_Q = SWA_HEADS * SWA_HEAD_DIM
SWA_KV = SWA_KV_HEADS * SWA_HEAD_DIM

LANES = 128
SUBLANES = 8
GLA_CHUNK = 128
VMEM_LIMIT = 56 * 1024 * 1024
NEG_BIG = -1e30

_NT = (((1,), (1,)), ((), ()))


def _rms(x, g):
    return x * lax.rsqrt(jnp.mean(x * x, axis=-1, keepdims=True) + EPS) * g


def _dot(a, b):
    return jnp.dot(a, b, preferred_element_type=F32)


def _dot_nt(a, b):
    return lax.dot_general(a, b, _NT, preferred_element_type=F32)


def _sigmoid(x):
    return 1.0 / (1.0 + jnp.exp(-x))


_P_GQ, _P_GK, _P_GV, _P_GG, _P_SQ, _P_SK, _P_SV, _P_GA, _P_GB, _P_LR, _P_END = (
    0, 512, 1024, 2048, 3072, 4096, 4224, 4352, 5376, 6400, 6528)


def _inproj_kernel(x_ref, nm_ref, w_ref, wdec_ref, bdec_ref,
                   gq_ref, gk_ref, gv_ref, gg_ref, la_ref, sq_ref, sk_ref, sv_ref, ga_ref, gb_ref):
    xb = _rms(x_ref[...], nm_ref[...]).astype(BF16)

    def proj(lo, hi):
        return _dot(xb, w_ref[:, lo:hi])

    gq_ref[...] = proj(_P_GQ, _P_GK)
    gk_ref[...] = proj(_P_GK, _P_GV)
    gv_ref[...] = proj(_P_GV, _P_GG)
    gg_ref[...] = proj(_P_GG, _P_SQ)
    sq_ref[...] = proj(_P_SQ, _P_SK)
    sk_ref[...] = proj(_P_SK, _P_SV)
    sv_ref[...] = proj(_P_SV, _P_GA)
    ga_ref[...] = proj(_P_GA, _P_GB)
    gb_ref[...] = proj(_P_GB, _P_LR)
    glr = proj(_P_LR, _P_END)
    z = _dot(glr.astype(BF16), wdec_ref[...]) + bdec_ref[...]
    la_ref[...] = (jnp.minimum(z, 0.0) - jnp.log(1.0 + jnp.exp(-jnp.abs(z)))) * (1.0 / GLA_TAU)


def _inproj(x2d, nm, w_all, wdec, bdec, tn):
    n = x2d.shape[0]
    widths = (GLA_QK, GLA_QK, GLA_V, GLA_V, GLA_QK, SWA_Q, SWA_KV, SWA_KV, D_MODEL, D_MODEL)
    const = lambda i: (0, 0)
    return pl.pallas_call(
        _inproj_kernel,
        grid=(n // tn,),
        in_specs=[
            pl.BlockSpec((tn, D_MODEL), lambda i: (i, 0)),
            pl.BlockSpec((1, D_MODEL), const),
            pl.BlockSpec((D_MODEL, _P_END), const, pipeline_mode=pl.Buffered(1)),
            pl.BlockSpec((LANES, GLA_QK), const),
            pl.BlockSpec((1, GLA_QK), const),
        ],
        out_specs=[pl.BlockSpec((tn, w), lambda i: (i, 0)) for w in widths],
        out_shape=[jax.ShapeDtypeStruct((n, w), F32) for w in widths],
        compiler_params=pltpu.CompilerParams(
            dimension_semantics=("parallel",), vmem_limit_bytes=VMEM_LIMIT),
        name="in_proj",
    )(x2d, nm, w_all, wdec, bdec)


def _gla_kernel(*refs, n_chunks, t_valid, has_s0):
    if has_s0:
        q_ref, k_ref, v_ref, la_ref, s0_ref, o_ref, so_ref, s_ref = refs
    else:
        q_ref, k_ref, v_ref, la_ref, o_ref, so_ref, s_ref = refs
    j = pl.program_id(1)
    c = GLA_CHUNK

    @pl.when(j == 0)
    def _():
        if has_s0:
            s_ref[...] = s0_ref[0]
        else:
            s_ref[...] = jnp.zeros_like(s_ref)

    row = lax.broadcasted_iota(jnp.int32, (c, c), 0)
    col = lax.broadcasted_iota(jnp.int32, (c, c), 1)
    causal = row >= col
    tril = causal.astype(BF16)
    scale = GLA_DK ** -0.5

    def load(ref, r0, lo, w):
        if t_valid >= c:
            return ref[0, r0:r0 + c, lo:lo + w]
        t = ref[0, :, lo:lo + w]
        return jnp.concatenate([t, jnp.zeros((c - t_valid, w), F32)], axis=0)

    for ci in range(n_chunks):
        r0 = ci * c
        for h in range(GLA_HEADS):
            q = load(q_ref, r0, h * GLA_DK, GLA_DK)
            k = load(k_ref, r0, h * GLA_DK, GLA_DK)
            v = load(v_ref, r0, h * GLA_DV, GLA_DV).astype(BF16)
            la = load(la_ref, r0, h * GLA_DK, GLA_DK)
            la_hi = la.astype(BF16)
            la_lo = (la - la_hi.astype(F32)).astype(BF16)
            b = _dot(tril, la_hi) + _dot(tril, la_lo)
            qd = ((q * scale) * jnp.exp(b)).astype(BF16)
            kn = (k * jnp.exp(-b)).astype(BF16)
            a = jnp.where(causal, _dot_nt(qd, kn), 0.0).astype(BF16)
            s_old = s_ref[h]
            o = _dot(qd, s_old.astype(BF16)) + _dot(a, v)
            if t_valid >= c:
                o_ref[0, r0:r0 + c, h * GLA_DV:(h + 1) * GLA_DV] = o
            else:
                o_ref[0, :, h * GLA_DV:(h + 1) * GLA_DV] = o[:t_valid]
            bl = b[c - 1:c, :]
            kd_t = (k * jnp.exp(bl - b)).T.astype(BF16)
            dec_t = jnp.broadcast_to(jnp.exp(bl), (c, GLA_DK)).T
            dec = jnp.concatenate([dec_t] * (GLA_DV // c), axis=1)
            s_ref[h] = dec * s_old + _dot(kd_t, v)

    @pl.when(j == pl.num_programs(1) - 1)
    def _():
        so_ref[0] = s_ref[...]


def _gla(gq, gk, gv, la, s0, tb):
    bsz, t, _ = gq.shape
    t_valid = min(tb, GLA_CHUNK)
    n_chunks = max(tb // GLA_CHUNK, 1)
    has_s0 = s0 is not None
    tok = lambda w: pl.BlockSpec((1, tb, w), lambda b, j: (b, j, 0))
    st = pl.BlockSpec((1, GLA_HEADS, GLA_DK, GLA_DV), lambda b, j: (b, 0, 0, 0))
    in_specs = [tok(GLA_QK), tok(GLA_QK), tok(GLA_V), tok(GLA_QK)] + ([st] if has_s0 else [])
    args = (gq, gk, gv, la) + ((s0,) if has_s0 else ())
    return pl.pallas_call(
        functools.partial(_gla_kernel, n_chunks=n_chunks, t_valid=t_valid, has_s0=has_s0),
        grid=(bsz, t // tb),
        in_specs=in_specs,
        out_specs=[tok(GLA_V), st],
        out_shape=[jax.ShapeDtypeStruct((bsz, t, GLA_V), F32),
                   jax.ShapeDtypeStruct((bsz, GLA_HEADS, GLA_DK, GLA_DV), F32)],
        scratch_shapes=[pltpu.VMEM((GLA_HEADS, GLA_DK, GLA_DV), F32)],
        compiler_params=pltpu.CompilerParams(
            dimension_semantics=("parallel", "arbitrary"), vmem_limit_bytes=VMEM_LIMIT),
        name="gla",
    )(*args)


def _swa_kernel(sink_ref, q_ref, kc_ref, vc_ref, kp_ref, vp_ref, o_ref, *, tq, prev_is_block):
    i = pl.program_id(1)
    w = WINDOW
    nk = 2 * w
    pairs = SWA_GROUP // 2
    t_idx = lax.broadcasted_iota(jnp.int32, (tq, nk), 0)
    j_idx = lax.broadcasted_iota(jnp.int32, (tq, nk), 1)
    d = j_idx - t_idx
    vis = (d >= 1) & (d <= w)
    if prev_is_block:
        vis = vis & ((j_idx >= w) | (i > 0))
    vis = jnp.concatenate([vis] * pairs, axis=0)
    lane = lax.broadcasted_iota(jnp.int32, (nk, LANES), 1)
    left = lane < SWA_HEAD_DIM

    def keys(prev_ref, cur_ref):
        cur = cur_ref[0]
        if tq < w:
            cur = jnp.concatenate([cur, jnp.zeros((w - tq, SWA_KV), F32)], axis=0)
        full = jnp.concatenate([prev_ref[0], cur], axis=0)
        return full, pltpu.roll(full, SWA_HEAD_DIM, 1)

    k_all, k_rot = keys(kp_ref, kc_ref)
    v_all, v_rot = keys(vp_ref, vc_ref)

    def block_diag(full, rot, kvh):
        a, b = (full, rot) if kvh == 0 else (rot, full)
        return jnp.concatenate([jnp.where(left, a, 0.0), jnp.where(left, 0.0, b)], axis=0).astype(BF16)

    for kvh in range(SWA_KV_HEADS):
        kbd = block_diag(k_all, k_rot, kvh)
        vbd = block_diag(v_all, v_rot, kvh)
        base = kvh * SWA_GROUP * SWA_HEAD_DIM
        qs = jnp.concatenate(
            [q_ref[0, :, base + p * LANES: base + (p + 1) * LANES] for p in range(pairs)], axis=0)
        s = _dot_nt(qs.astype(BF16), kbd) * (SWA_HEAD_DIM ** -0.5)
        probs = []
        for e in range(2):
            se = jnp.where(vis, s[:, e * nk:(e + 1) * nk], NEG_BIG)
            sink = jnp.concatenate(
                [jnp.full((tq, 1), sink_ref[kvh * SWA_GROUP + 2 * p + e], F32) for p in range(pairs)],
                axis=0)
            m = jnp.maximum(jnp.max(se, axis=-1, keepdims=True), sink)
            ex = jnp.exp(se - m)
            den = jnp.sum(ex, axis=-1, keepdims=True) + jnp.exp(sink - m)
            probs.append(ex / den)
        o = _dot(jnp.concatenate(probs, axis=1).astype(BF16), vbd)
        for p in range(pairs):
            o_ref[0, :, base + p * LANES: base + (p + 1) * LANES] = o[p * tq:(p + 1) * tq]


def _swa(sq, sk, sv, sinks, prev_k=None, prev_v=None):
    bsz, t, _ = sq.shape
    prev_is_block = prev_k is None
    if prev_is_block:
        tq = WINDOW
        prev_k, prev_v = sk, sv
        prev_spec = pl.BlockSpec((1, WINDOW, SWA_KV), lambda b, i: (b, jnp.maximum(i - 1, 0), 0))
    else:
        tq = t
        prev_spec = pl.BlockSpec((1, WINDOW, SWA_KV), lambda b, i: (b, 0, 0))
    cur_spec = pl.BlockSpec((1, tq, SWA_KV), lambda b, i: (b, i, 0))
    q_spec = pl.BlockSpec((1, tq, SWA_Q), lambda b, i: (b, i, 0))
    return pl.pallas_call(
        functools.partial(_swa_kernel, tq=tq, prev_is_block=prev_is_block),
        grid=(bsz, t // tq),
        in_specs=[pl.BlockSpec(memory_space=pltpu.SMEM), q_spec, cur_spec, cur_spec, prev_spec, prev_spec],
        out_specs=q_spec,
        out_shape=jax.ShapeDtypeStruct((bsz, t, SWA_Q), F32),
        compiler_params=pltpu.CompilerParams(
            dimension_semantics=("parallel", "arbitrary"), vmem_limit_bytes=VMEM_LIMIT),
        name="swa",
    )(sinks, sq, sk, sv, prev_k, prev_v)


def _merge_kernel(x_ref, oa_ref, gg_ref, ob_ref, ga_ref, gb_ref, gn_ref, wa_ref, wb_ref, wo_ref, h_ref):
    parts = []
    for hd in range(GLA_HEADS):
        sl = slice(hd * GLA_DV, (hd + 1) * GLA_DV)
        g = gg_ref[:, sl]
        parts.append(_rms(oa_ref[:, sl], gn_ref[...]) * (g * _sigmoid(g)))
    oa = jnp.concatenate(parts, axis=1).astype(BF16)
    br_a = _dot(oa, wa_ref[...])
    br_b = _dot(ob_ref[...].astype(BF16), wb_ref[...])
    merged = _sigmoid(ga_ref[...]) * br_a + _sigmoid(gb_ref[...]) * br_b
    h_ref[...] = x_ref[...] + _dot(merged.astype(BF16), wo_ref[...])


def _merge(x2d, oa, gg, ob, ga, gb, gn, wa, wb, wo, tn):
    n = x2d.shape[0]
    tok = pl.BlockSpec((tn, D_MODEL), lambda i: (i, 0))
    const = lambda i: (0, 0)
    wspec = pl.BlockSpec((D_MODEL, D_MODEL), const)
    return pl.pallas_call(
        _merge_kernel,
        grid=(n // tn,),
        in_specs=[tok] * 6 + [pl.BlockSpec((1, GLA_DV), const), wspec, wspec, wspec],
        out_specs=tok,
        out_shape=jax.ShapeDtypeStruct((n, D_MODEL), F32),
        compiler_params=pltpu.CompilerParams(
            dimension_semantics=("parallel",), vmem_limit_bytes=VMEM_LIMIT),
        name="merge",
    )(x2d, oa, gg, ob, ga, gb, gn, wa, wb, wo)


def _sort16_pairs():
    pairs = []
    n = 16
    p = 1
    while p < n:
        k = p
        while k >= 1:
            for j in range(k % p, n - k, 2 * k):
                for i in range(min(k, n - j - k)):
                    if (i + j) // (2 * p) == (i + j + k) // (2 * p):
                        pairs.append((i + j, i + j + k))
            k //= 2
        p *= 2
    return pairs


_SORT16 = _sort16_pairs()


def _bitonic_top16(xa, xb):
    z = [jnp.maximum(xa[i], xb[15 - i]) for i in range(16)]
    dist = 8
    while dist >= 1:
        for i in range(16):
            if i & dist == 0:
                hi = jnp.maximum(z[i], z[i + dist])
                lo = jnp.minimum(z[i], z[i + dist])
                z[i], z[i + dist] = hi, lo
        dist //= 2
    return z


def _top16_rows(s):
    x = [s[SUBLANES * v:SUBLANES * (v + 1), :] for v in range(16)]
    for i, j in _SORT16:
        hi = jnp.maximum(x[i], x[j])
        lo = jnp.minimum(x[i], x[j])
        x[i], x[j] = hi, lo
    shift = 1
    while shift < SUBLANES:
        x = _bitonic_top16(x, [pltpu.roll(t, shift, 0) for t in x])
        shift *= 2
    return [t[SUBLANES - 1:SUBLANES, :] for t in x]


_CAND_Q = [PEER_TOPK // (p + 1) for p in range(PEER_TOPK)]


def _peer_kernel(h_ref, nf_ref, nfin_ref, wq_ref, k1_ref, k2_ref, u_ref, vt_ref, y_ref,
                 hn_ref, s1_ref, s2_ref, e1_ref, e2_ref, v1_ref, v2_ref, tau_ref,
                 ht_ref, g_ref, acc_ref, *, tn, ec):
    e = pl.program_id(1)
    na = ec // PEER_NKEYS
    nt = tn // LANES

    @pl.when(e == 0)
    def _routing():
        hn_ref[...] = _rms(h_ref[...], nf_ref[...]).astype(BF16)
        acc_ref[...] = jnp.zeros_like(acc_ref)

        for hd in range(PEER_HEADS):
            wq = wq_ref[hd * PEER_QDIM:(hd + 1) * PEER_QDIM, :]
            qt = _dot_nt(wq, hn_ref[...])
            s1 = _dot(k1_ref[hd], qt[:PEER_HALF].astype(BF16))
            s2 = _dot(k2_ref[hd], qt[PEER_HALF:].astype(BF16))
            s1_ref[hd] = s1
            s2_ref[hd] = s2
            for p, r in enumerate(_top16_rows(s1)):
                v1_ref[p, hd:hd + 1, :] = r
            for p, r in enumerate(_top16_rows(s2)):
                v2_ref[p, hd:hd + 1, :] = r

        v1 = [v1_ref[p] for p in range(PEER_TOPK)]
        v2 = [v2_ref[q] for q in range(PEER_TOPK)]
        ninf = jnp.full((PEER_HEADS, tn), -jnp.inf, F32)
        best = [v1[0] + v2[q] for q in range(PEER_TOPK)]
        for p in range(1, PEER_TOPK):
            row = [v1[p] + v2[q] if q < _CAND_Q[p] else ninf for q in range(PEER_TOPK)]
            best = _bitonic_top16(best, row)
        tau = best[PEER_TOPK - 1]
        top = best[0]
        zsum = jnp.zeros((PEER_HEADS, tn), F32)
        for p in range(PEER_TOPK):
            for q in range(_CAND_Q[p]):
                cnd = v1[p] + v2[q]
                zsum = zsum + jnp.where(cnd >= tau, jnp.exp(cnd - top), 0.0)
        tau_ref[0] = tau
        tau_ref[1] = 1.0 / zsum

        for hd in range(PEER_HEADS):
            e1_ref[hd] = jnp.exp(s1_ref[hd] - v1_ref[0, hd:hd + 1, :])
            e2_ref[hd] = jnp.exp(s2_ref[hd] - v2_ref[0, hd:hd + 1, :]) * tau_ref[1, hd:hd + 1, :]

    ht_ref[...] = _dot_nt(u_ref[...], hn_ref[...])
    a0 = e * na

    def lane_tile(ti, carry):
        cols = pl.ds(pl.multiple_of(ti * LANES, LANES), LANES)
        a_rows = pl.ds(pl.multiple_of(a0, na), na)
        s1g = [s1_ref[hd, a_rows, cols] for hd in range(PEER_HEADS)]
        e1g = [e1_ref[hd, a_rows, cols] for hd in range(PEER_HEADS)]
        for ai in range(na):
            w = jnp.zeros((PEER_NKEYS, LANES), F32)
            for hd in range(PEER_HEADS):
                pair = s2_ref[hd, :, cols] + s1g[hd][ai:ai + 1]
                sel = jnp.where(pair >= tau_ref[0, hd:hd + 1, cols], e2_ref[hd, :, cols], 0.0)
                w = w + sel * e1g[hd][ai:ai + 1]
            rows = slice(ai * PEER_NKEYS, (ai + 1) * PEER_NKEYS)
            hx = ht_ref[rows, cols]
            act = 0.5 * hx * (1.0 + lax.erf(hx * (2.0 ** -0.5)))
            g_ref[rows, cols] = (w * act).astype(BF16)
        return carry

    lax.fori_loop(0, nt, lane_tile, 0)
    acc_ref[...] += _dot(vt_ref[...], g_ref[...])

    @pl.when(e == pl.num_programs(1) - 1)
    def _finish():
        y_ref[...] = _rms(h_ref[...] + acc_ref[...].T, nfin_ref[...])


def _peer(h2d, nf, nfin, wq_t, k1, k2, u, vt, tn, ec):
    n = h2d.shape[0]
    tok = pl.BlockSpec((tn, D_MODEL), lambda i, e: (i, 0))
    const2 = lambda i, e: (0, 0)
    const3 = lambda i, e: (0, 0, 0)
    vec = pl.BlockSpec((1, D_MODEL), const2)
    kspec = pl.BlockSpec((PEER_HEADS, PEER_NKEYS, PEER_HALF), const3)
    head_tok = pltpu.VMEM((PEER_HEADS, PEER_NKEYS, tn), F32)
    rank_tok = pltpu.VMEM((PEER_TOPK, PEER_HEADS, tn), F32)
    return pl.pallas_call(
        functools.partial(_peer_kernel, tn=tn, ec=ec),
        grid=(n // tn, PEER_N_EXPERTS // ec),
        in_specs=[tok, vec, vec,
                  pl.BlockSpec((PEER_HEADS * PEER_QDIM, D_MODEL), const2, pipeline_mode=pl.Buffered(1)),
                  kspec, kspec,
                  pl.BlockSpec((ec, D_MODEL), lambda i, e: (e, 0)),
                  pl.BlockSpec((D_MODEL, ec), lambda i, e: (0, e))],
        out_specs=tok,
        out_shape=jax.ShapeDtypeStruct((n, D_MODEL), F32),
        scratch_shapes=[pltpu.VMEM((tn, D_MODEL), BF16),
                        head_tok, head_tok, head_tok, head_tok, rank_tok, rank_tok,
                        pltpu.VMEM((2, PEER_HEADS, tn), F32),
                        pltpu.VMEM((ec, tn), F32), pltpu.VMEM((ec, tn), BF16),
                        pltpu.VMEM((D_MODEL, tn), F32)],
        compiler_params=pltpu.CompilerParams(
            dimension_semantics=("parallel", "arbitrary"), vmem_limit_bytes=VMEM_LIMIT),
        name="peer",
    )(h2d, nf, nfin, wq_t, k1, k2, u, vt)


def _pack_params(norm_mix, w_in, gla_w_decay, gla_b_decay, gla_norm, swa_sinks, w_branch_a, w_branch_b,
                 w_out, norm_ffn, peer_w_q, peer_keys1, peer_keys2, peer_u, peer_v):
    off = np.cumsum((0, GLA_QK, GLA_QK, GLA_V, GLA_V, GLA_RANK, SWA_Q, SWA_KV, SWA_KV, D_MODEL, D_MODEL))
    col = lambda i: w_in[:, off[i]:off[i + 1]]
    lr = jnp.pad(col(4), ((0, 0), (0, LANES - GLA_RANK)))
    w_all = jnp.concatenate([col(0), col(1), col(2), col(3), col(5), col(6), col(7), col(8), col(9), lr],
                            axis=1).astype(BF16)
    wdec = jnp.pad(gla_w_decay, ((0, LANES - GLA_RANK), (0, 0))).astype(BF16)
    return dict(
        nm=norm_mix[None], w_all=w_all, wdec=wdec, bdec=gla_b_decay[None], gn=gla_norm[None],
        sinks=swa_sinks, wa=w_branch_a.astype(BF16), wb=w_branch_b.astype(BF16), wo=w_out.astype(BF16),
        nf=norm_ffn[None], wq_t=peer_w_q.T.astype(BF16), k1=peer_keys1.astype(BF16),
        k2=peer_keys2.astype(BF16), u=peer_u.astype(BF16), vt=peer_v.T.astype(BF16))


def _layer(x, p, nfin, s0, win_k, win_v, *, tn, gla_tb, peer_tn, peer_ec):
    bsz, t, _ = x.shape
    x2d = x.reshape(bsz * t, D_MODEL)
    gq, gk, gv, gg, la, sq, sk, sv, ga, gb = _inproj(x2d, p["nm"], p["w_all"], p["wdec"], p["bdec"], tn)
    r3 = lambda a: a.reshape(bsz, t, a.shape[-1])
    o_a, s_new = _gla(r3(gq), r3(gk), r3(gv), r3(la), s0, gla_tb)
    if win_k is None:
        o_b = _swa(r3(sq), r3(sk), r3(sv), p["sinks"])
        ln = min(WINDOW, t)
        new_k, new_v = r3(sk)[:, t - ln:], r3(sv)[:, t - ln:]
    else:
        wk = win_k.reshape(bsz, WINDOW, SWA_KV)
        wv = win_v.reshape(bsz, WINDOW, SWA_KV)
        o_b = _swa(r3(sq), r3(sk), r3(sv), p["sinks"], wk, wv)
        new_k = jnp.concatenate([wk, r3(sk)], axis=1)[:, -WINDOW:]
        new_v = jnp.concatenate([wv, r3(sv)], axis=1)[:, -WINDOW:]
    h = _merge(x2d, o_a.reshape(bsz * t, GLA_V), gg, o_b.reshape(bsz * t, SWA_Q), ga, gb,
               p["gn"], p["wa"], p["wb"], p["wo"], tn)
    y = _peer(h, p["nf"], nfin, p["wq_t"], p["k1"], p["k2"], p["u"], p["vt"], peer_tn, peer_ec)
    kv_shape = (bsz, -1, SWA_KV_HEADS, SWA_HEAD_DIM)
    return y.reshape(bsz, t, D_MODEL), new_k.reshape(kv_shape), new_v.reshape(kv_shape), s_new


def kernel(x_prompt, x_sample, cache_win_k, cache_win_v, state_gla, norm_mix, w_in, gla_w_decay, gla_b_decay,
           gla_norm, swa_sinks, w_branch_a, w_branch_b, w_out, norm_ffn, peer_w_q, peer_keys1, peer_keys2,
           peer_u, peer_v, norm_final):
    depth = w_in.shape[0]
    assert depth == 1, "the final norm is fused into the last layer's channel mixer; one layer supported"
    p = _pack_params(norm_mix[0], w_in[0], gla_w_decay[0], gla_b_decay[0], gla_norm[0], swa_sinks[0],
                     w_branch_a[0], w_branch_b[0], w_out[0], norm_ffn[0], peer_w_q[0], peer_keys1[0],
                     peer_keys2[0], peer_u[0], peer_v[0])
    nfin = norm_final[None]
    yp, pk, pv, ps = _layer(x_prompt, p, nfin, None, None, None,
                            tn=512, gla_tb=512, peer_tn=512, peer_ec=1024)
    ys, sk, sv, ss = _layer(x_sample, p, nfin, state_gla[0], cache_win_k[0], cache_win_v[0],
                            tn=512, gla_tb=x_sample.shape[1], peer_tn=512, peer_ec=1024)
    return (yp, ys, pk[None], pv[None], ps[None], sk[None], sv[None], ss[None])
```

```python
import functools

import jax
import jax.numpy as jnp
import numpy as np
from jax import lax
from jax.experimental import pallas as pl
from jax.experimental.pallas import tpu as pltpu

F32 = jnp.float32
BF16 = jnp.bfloat16

D_MODEL = 1024
GLA_HEADS = 4
GLA_DK = 128
GLA_DV = 256
GLA_RANK = 16
GLA_TAU = 16.0
SWA_HEADS = 16
SWA_KV_HEADS = 2
SWA_GROUP = SWA_HEADS // SWA_KV_HEADS
SWA_HEAD_DIM = 64
WINDOW = 128
PEER_HEADS = 8
PEER_NKEYS = 128
PEER_N_EXPERTS = PEER_NKEYS * PEER_NKEYS
PEER_QDIM = 256
PEER_HALF = PEER_QDIM // 2
PEER_TOPK = 16
EPS = 1e-6

GLA_QK = GLA_HEADS * GLA_DK
GLA_V = GLA_HEADS * GLA_DV
SWA_Q = SWA_HEADS * SWA_HEAD_DIM
SWA_KV = SWA_KV_HEADS * SWA_HEAD_DIM

LANES = 128
SUBLANES = 8
GLA_CHUNK = 128
VMEM_LIMIT = 56 * 1024 * 1024
NEG_BIG = -1e30

_NT = (((1,), (1,)), ((), ()))


def _rms(x, g):
    return x * lax.rsqrt(jnp.mean(x * x, axis=-1, keepdims=True) + EPS) * g


def _dot(a, b):
    return jnp.dot(a, b, preferred_element_type=F32)


def _dot_nt(a, b):
    return lax.dot_general(a, b, _NT, preferred_element_type=F32)


def _sigmoid(x):
    return 1.0 / (1.0 + jnp.exp(-x))


_P_GQ, _P_GK, _P_GV, _P_GG, _P_SQ, _P_SK, _P_SV, _P_GA, _P_GB, _P_LR, _P_END = (
    0, 512, 1024, 2048, 3072, 4096, 4224, 4352, 5376, 6400, 6528)


def _inproj_kernel(x_ref, nm_ref, w_ref, wdec_ref, bdec_ref,
                   gq_ref, gk_ref, gv_ref, gg_ref, la_ref, sq_ref, sk_ref, sv_ref, ga_ref, gb_ref):
    xb = _rms(x_ref[...], nm_ref[...]).astype(BF16)

    def proj(lo, hi):
        return _dot(xb, w_ref[:, lo:hi])

    gq_ref[...] = proj(_P_GQ, _P_GK)
    gk_ref[...] = proj(_P_GK, _P_GV)
    gv_ref[...] = proj(_P_GV, _P_GG)
    gg_ref[...] = proj(_P_GG, _P_SQ)
    sq_ref[...] = proj(_P_SQ, _P_SK)
    sk_ref[...] = proj(_P_SK, _P_SV)
    sv_ref[...] = proj(_P_SV, _P_GA)
    ga_ref[...] = proj(_P_GA, _P_GB)
    gb_ref[...] = proj(_P_GB, _P_LR)
    glr = proj(_P_LR, _P_END)
    z = _dot(glr.astype(BF16), wdec_ref[...]) + bdec_ref[...]
    la_ref[...] = (jnp.minimum(z, 0.0) - jnp.log(1.0 + jnp.exp(-jnp.abs(z)))) * (1.0 / GLA_TAU)


def _inproj(x2d, nm, w_all, wdec, bdec, tn):
    n = x2d.shape[0]
    widths = (GLA_QK, GLA_QK, GLA_V, GLA_V, GLA_QK, SWA_Q, SWA_KV, SWA_KV, D_MODEL, D_MODEL)
    const = lambda i: (0, 0)
    return pl.pallas_call(
        _inproj_kernel,
        grid=(n // tn,),
        in_specs=[
            pl.BlockSpec((tn, D_MODEL), lambda i: (i, 0)),
            pl.BlockSpec((1, D_MODEL), const),
            pl.BlockSpec((D_MODEL, _P_END), const, pipeline_mode=pl.Buffered(1)),
            pl.BlockSpec((LANES, GLA_QK), const),
            pl.BlockSpec((1, GLA_QK), const),
        ],
        out_specs=[pl.BlockSpec((tn, w), lambda i: (i, 0)) for w in widths],
        out_shape=[jax.ShapeDtypeStruct((n, w), F32) for w in widths],
        compiler_params=pltpu.CompilerParams(
            dimension_semantics=("parallel",), vmem_limit_bytes=VMEM_LIMIT),
        name="in_proj",
    )(x2d, nm, w_all, wdec, bdec)


def _gla_kernel(*refs, n_chunks, t_valid, has_s0):
    if has_s0:
        q_ref, k_ref, v_ref, la_ref, s0_ref, o_ref, so_ref, s_ref = refs
    else:
        q_ref, k_ref, v_ref, la_ref, o_ref, so_ref, s_ref = refs
    j = pl.program_id(1)
    c = GLA_CHUNK

    @pl.when(j == 0)
    def _():
        if has_s0:
            s_ref[...] = s0_ref[0]
        else:
            s_ref[...] = jnp.zeros_like(s_ref)

    row = lax.broadcasted_iota(jnp.int32, (c, c), 0)
    col = lax.broadcasted_iota(jnp.int32, (c, c), 1)
    causal = row >= col
    tril = causal.astype(BF16)
    scale = GLA_DK ** -0.5

    def load(ref, r0, lo, w):
        if t_valid >= c:
            return ref[0, r0:r0 + c, lo:lo + w]
        t = ref[0, :, lo:lo + w]
        return jnp.concatenate([t, jnp.zeros((c - t_valid, w), F32)], axis=0)

    for ci in range(n_chunks):
        r0 = ci * c
        for h in range(GLA_HEADS):
            q = load(q_ref, r0, h * GLA_DK, GLA_DK)
            k = load(k_ref, r0, h * GLA_DK, GLA_DK)
            v = load(v_ref, r0, h * GLA_DV, GLA_DV).astype(BF16)
            la = load(la_ref, r0, h * GLA_DK, GLA_DK)
            la_hi = la.astype(BF16)
            la_lo = (la - la_hi.astype(F32)).astype(BF16)
            b = _dot(tril, la_hi) + _dot(tril, la_lo)
            qd = ((q * scale) * jnp.exp(b)).astype(BF16)
            kn = (k * jnp.exp(-b)).astype(BF16)
            a = jnp.where(causal, _dot_nt(qd, kn), 0.0).astype(BF16)
            s_old = s_ref[h]
            o = _dot(qd, s_old.astype(BF16)) + _dot(a, v)
            if t_valid >= c:
                o_ref[0, r0:r0 + c, h * GLA_DV:(h + 1) * GLA_DV] = o
            else:
                o_ref[0, :, h * GLA_DV:(h + 1) * GLA_DV] = o[:t_valid]
            bl = b[c - 1:c, :]
            kd_t = (k * jnp.exp(bl - b)).T.astype(BF16)
            dec_t = jnp.broadcast_to(jnp.exp(bl), (c, GLA_DK)).T
            dec = jnp.concatenate([dec_t] * (GLA_DV // c), axis=1)
            s_ref[h] = dec * s_old + _dot(kd_t, v)

    @pl.when(j == pl.num_programs(1) - 1)
    def _():
        so_ref[0] = s_ref[...]


def _gla(gq, gk, gv, la, s0, tb):
    bsz, t, _ = gq.shape
    t_valid = min(tb, GLA_CHUNK)
    n_chunks = max(tb // GLA_CHUNK, 1)
    has_s0 = s0 is not None
    tok = lambda w: pl.BlockSpec((1, tb, w), lambda b, j: (b, j, 0))
    st = pl.BlockSpec((1, GLA_HEADS, GLA_DK, GLA_DV), lambda b, j: (b, 0, 0, 0))
    in_specs = [tok(GLA_QK), tok(GLA_QK), tok(GLA_V), tok(GLA_QK)] + ([st] if has_s0 else [])
    args = (gq, gk, gv, la) + ((s0,) if has_s0 else ())
    return pl.pallas_call(
        functools.partial(_gla_kernel, n_chunks=n_chunks, t_valid=t_valid, has_s0=has_s0),
        grid=(bsz, t // tb),
        in_specs=in_specs,
        out_specs=[tok(GLA_V), st],
        out_shape=[jax.ShapeDtypeStruct((bsz, t, GLA_V), F32),
                   jax.ShapeDtypeStruct((bsz, GLA_HEADS, GLA_DK, GLA_DV), F32)],
        scratch_shapes=[pltpu.VMEM((GLA_HEADS, GLA_DK, GLA_DV), F32)],
        compiler_params=pltpu.CompilerParams(
            dimension_semantics=("parallel", "arbitrary"), vmem_limit_bytes=VMEM_LIMIT),
        name="gla",
    )(*args)


def _swa_kernel(sink_ref, q_ref, kc_ref, vc_ref, kp_ref, vp_ref, o_ref, *, tq, prev_is_block):
    i = pl.program_id(1)
    w = WINDOW
    nk = 2 * w
    pairs = SWA_GROUP // 2
    t_idx = lax.broadcasted_iota(jnp.int32, (tq, nk), 0)
    j_idx = lax.broadcasted_iota(jnp.int32, (tq, nk), 1)
    d = j_idx - t_idx
    vis = (d >= 1) & (d <= w)
    if prev_is_block:
        vis = vis & ((j_idx >= w) | (i > 0))
    vis = jnp.concatenate([vis] * pairs, axis=0)
    lane = lax.broadcasted_iota(jnp.int32, (nk, LANES), 1)
    left = lane < SWA_HEAD_DIM

    def keys(prev_ref, cur_ref):
        cur = cur_ref[0]
        if tq < w:
            cur = jnp.concatenate([cur, jnp.zeros((w - tq, SWA_KV), F32)], axis=0)
        full = jnp.concatenate([prev_ref[0], cur], axis=0)
        return full, pltpu.roll(full, SWA_HEAD_DIM, 1)

    k_all, k_rot = keys(kp_ref, kc_ref)
    v_all, v_rot = keys(vp_ref, vc_ref)

    def block_diag(full, rot, kvh):
        a, b = (full, rot) if kvh == 0 else (rot, full)
        return jnp.concatenate([jnp.where(left, a, 0.0), jnp.where(left, 0.0, b)], axis=0).astype(BF16)

    for kvh in range(SWA_KV_HEADS):
        kbd = block_diag(k_all, k_rot, kvh)
        vbd = block_diag(v_all, v_rot, kvh)
        base = kvh * SWA_GROUP * SWA_HEAD_DIM
        qs = jnp.concatenate(
            [q_ref[0, :, base + p * LANES: base + (p + 1) * LANES] for p in range(pairs)], axis=0)
        s = _dot_nt(qs.astype(BF16), kbd) * (SWA_HEAD_DIM ** -0.5)
        probs = []
        for e in range(2):
            se = jnp.where(vis, s[:, e * nk:(e + 1) * nk], NEG_BIG)
            sink = jnp.concatenate(
                [jnp.full((tq, 1), sink_ref[kvh * SWA_GROUP + 2 * p + e], F32) for p in range(pairs)],
                axis=0)
            m = jnp.maximum(jnp.max(se, axis=-1, keepdims=True), sink)
            ex = jnp.exp(se - m)
            den = jnp.sum(ex, axis=-1, keepdims=True) + jnp.exp(sink - m)
            probs.append(ex / den)
        o = _dot(jnp.concatenate(probs, axis=1).astype(BF16), vbd)
        for p in range(pairs):
            o_ref[0, :, base + p * LANES: base + (p + 1) * LANES] = o[p * tq:(p + 1) * tq]


def _swa(sq, sk, sv, sinks, prev_k=None, prev_v=None):
    bsz, t, _ = sq.shape
    prev_is_block = prev_k is None
    if prev_is_block:
        tq = WINDOW
        prev_k, prev_v = sk, sv
        prev_spec = pl.BlockSpec((1, WINDOW, SWA_KV), lambda b, i: (b, jnp.maximum(i - 1, 0), 0))
    else:
        tq = t
        prev_spec = pl.BlockSpec((1, WINDOW, SWA_KV), lambda b, i: (b, 0, 0))
    cur_spec = pl.BlockSpec((1, tq, SWA_KV), lambda b, i: (b, i, 0))
    q_spec = pl.BlockSpec((1, tq, SWA_Q), lambda b, i: (b, i, 0))
    return pl.pallas_call(
        functools.partial(_swa_kernel, tq=tq, prev_is_block=prev_is_block),
        grid=(bsz, t // tq),
        in_specs=[pl.BlockSpec(memory_space=pltpu.SMEM), q_spec, cur_spec, cur_spec, prev_spec, prev_spec],
        out_specs=q_spec,
        out_shape=jax.ShapeDtypeStruct((bsz, t, SWA_Q), F32),
        compiler_params=pltpu.CompilerParams(
            dimension_semantics=("parallel", "arbitrary"), vmem_limit_bytes=VMEM_LIMIT),
        name="swa",
    )(sinks, sq, sk, sv, prev_k, prev_v)


def _merge_kernel(x_ref, oa_ref, gg_ref, ob_ref, ga_ref, gb_ref, gn_ref, wa_ref, wb_ref, wo_ref, h_ref):
    parts = []
    for hd in range(GLA_HEADS):
        sl = slice(hd * GLA_DV, (hd + 1) * GLA_DV)
        g = gg_ref[:, sl]
        parts.append(_rms(oa_ref[:, sl], gn_ref[...]) * (g * _sigmoid(g)))
    oa = jnp.concatenate(parts, axis=1).astype(BF16)
    br_a = _dot(oa, wa_ref[...])
    br_b = _dot(ob_ref[...].astype(BF16), wb_ref[...])
    merged = _sigmoid(ga_ref[...]) * br_a + _sigmoid(gb_ref[...]) * br_b
    h_ref[...] = x_ref[...] + _dot(merged.astype(BF16), wo_ref[...])


def _merge(x2d, oa, gg, ob, ga, gb, gn, wa, wb, wo, tn):
    n = x2d.shape[0]
    tok = pl.BlockSpec((tn, D_MODEL), lambda i: (i, 0))
    const = lambda i: (0, 0)
    wspec = pl.BlockSpec((D_MODEL, D_MODEL), const)
    return pl.pallas_call(
        _merge_kernel,
        grid=(n // tn,),
        in_specs=[tok] * 6 + [pl.BlockSpec((1, GLA_DV), const), wspec, wspec, wspec],
        out_specs=tok,
        out_shape=jax.ShapeDtypeStruct((n, D_MODEL), F32),
        compiler_params=pltpu.CompilerParams(
            dimension_semantics=("parallel",), vmem_limit_bytes=VMEM_LIMIT),
        name="merge",
    )(x2d, oa, gg, ob, ga, gb, gn, wa, wb, wo)


def _sort16_pairs():
    pairs = []
    n = 16
    p = 1
    while p < n:
        k = p
        while k >= 1:
            for j in range(k % p, n - k, 2 * k):
                for i in range(min(k, n - j - k)):
                    if (i + j) // (2 * p) == (i + j + k) // (2 * p):
                        pairs.append((i + j, i + j + k))
            k //= 2
        p *= 2
    return pairs


_SORT16 = _sort16_pairs()


def _bitonic_top16(xa, xb):
    z = [jnp.maximum(xa[i], xb[15 - i]) for i in range(16)]
    dist = 8
    while dist >= 1:
        for i in range(16):
            if i & dist == 0:
                hi = jnp.maximum(z[i], z[i + dist])
                lo = jnp.minimum(z[i], z[i + dist])
                z[i], z[i + dist] = hi, lo
        dist //= 2
    return z


def _top16_rows(s):
    x = [s[SUBLANES * v:SUBLANES * (v + 1), :] for v in range(16)]
    for i, j in _SORT16:
        hi = jnp.maximum(x[i], x[j])
        lo = jnp.minimum(x[i], x[j])
        x[i], x[j] = hi, lo
    shift = 1
    while shift < SUBLANES:
        x = _bitonic_top16(x, [pltpu.roll(t, shift, 0) for t in x])
        shift *= 2
    return [t[SUBLANES - 1:SUBLANES, :] for t in x]


_CAND_Q = [PEER_TOPK // (p + 1) for p in range(PEER_TOPK)]


PACK = 2 * SUBLANES
_HS_TAU, _HS_ZINV, _HS_TOP1, _HS_ROWS = PEER_TOPK, PEER_TOPK + 1, PEER_TOPK + 2, PEER_TOPK + SUBLANES


def _peer_kernel(h_ref, nf_ref, nfin_ref, wq_ref, k1_ref, k2_ref, u_ref, vt_ref, y_ref,
                 hn_ref, sa_ref, sb_ref, r2_ref, e2_ref, v1_ref, v2_ref, hs_ref,
                 ht_ref, g_ref, acc_ref, *, tn, ec):
    e = pl.program_id(1)
    na = ec // PEER_NKEYS
    nt = tn // LANES
    rb = 4 * SUBLANES

    @pl.when(e == 0)
    def _routing():
        hn_ref[...] = _rms(h_ref[...], nf_ref[...]).astype(BF16)
        acc_ref[...] = jnp.zeros_like(acc_ref)

        for hd in range(PEER_HEADS):
            wq = wq_ref[hd * PEER_QDIM:(hd + 1) * PEER_QDIM, :]
            qt = _dot_nt(wq, hn_ref[...])
            s1 = _dot(k1_ref[hd], qt[:PEER_HALF].astype(BF16))
            s2 = _dot(k2_ref[hd], qt[PEER_HALF:].astype(BF16))
            sa_ref[hd] = s1
            sb_ref[hd] = s2
            for p, r in enumerate(_top16_rows(s1)):
                v1_ref[p, hd:hd + 1, :] = r
            for p, r in enumerate(_top16_rows(s2)):
                v2_ref[p, hd:hd + 1, :] = r
                hs_ref[hd, p:p + 1, :] = r

        v1 = [v1_ref[p] for p in range(PEER_TOPK)]
        v2 = [v2_ref[q] for q in range(PEER_TOPK)]
        ninf = jnp.full((PEER_HEADS, tn), -jnp.inf, F32)
        best = [v1[0] + v2[q] for q in range(PEER_TOPK)]
        for p in range(1, PEER_TOPK):
            row = [v1[p] + v2[q] if q < _CAND_Q[p] else ninf for q in range(PEER_TOPK)]
            best = _bitonic_top16(best, row)
        tau = best[PEER_TOPK - 1]
        top = best[0]
        zsum = jnp.zeros((PEER_HEADS, tn), F32)
        for p in range(PEER_TOPK):
            for q in range(_CAND_Q[p]):
                cnd = v1[p] + v2[q]
                zsum = zsum + jnp.where(cnd >= tau, jnp.exp(cnd - top), 0.0)
        zinv = 1.0 / zsum
        for hd in range(PEER_HEADS):
            hs_ref[hd, _HS_TAU:_HS_TAU + 1, :] = tau[hd:hd + 1]
            hs_ref[hd, _HS_ZINV:_HS_ZINV + 1, :] = zinv[hd:hd + 1]
            hs_ref[hd, _HS_TOP1:_HS_TOP1 + 1, :] = v1[0][hd:hd + 1]

        def staircase(idx, carry):
            hd = idx // (PEER_NKEYS // rb)
            rows = pl.ds(pl.multiple_of((idx % (PEER_NKEYS // rb)) * rb, rb), rb)
            s1 = sa_ref[hd, rows, :]
            s2 = sb_ref[hd, rows, :]
            tau_h = hs_ref[hd, _HS_TAU:_HS_TAU + 1, :]
            rank = jnp.zeros((rb, tn), F32)
            cnt = jnp.zeros((rb, tn), F32)
            for q in range(PEER_TOPK):
                v2q = hs_ref[hd, q:q + 1, :]
                rank = rank + jnp.where(v2q > s2, 1.0, 0.0)
                cnt = cnt + jnp.where(s1 + v2q >= tau_h, 1.0, 0.0)
            r2_ref[hd, rows, :] = rank.astype(BF16)
            e2_ref[hd, rows, :] = (jnp.exp(s2 - hs_ref[hd, 0:1, :])
                                   * hs_ref[hd, _HS_ZINV:_HS_ZINV + 1, :]).astype(BF16)
            sa_ref[hd, rows, :] = cnt
            sb_ref[hd, rows, :] = jnp.exp(s1 - hs_ref[hd, _HS_TOP1:_HS_TOP1 + 1, :])
            return carry

        lax.fori_loop(0, PEER_HEADS * (PEER_NKEYS // rb), staircase, 0)

    ht_ref[...] = _dot_nt(u_ref[...], hn_ref[...])
    a0 = e * na
    n_grp = na // SUBLANES

    def gate(idx, carry):
        cols = pl.ds(pl.multiple_of((idx // n_grp) * LANES, LANES), LANES)
        ag = idx % n_grp
        grp = pl.ds(pl.multiple_of(a0 + ag * SUBLANES, SUBLANES), SUBLANES)
        cnt_g = [sa_ref[hd, grp, cols] for hd in range(PEER_HEADS)]
        e1_g = [sb_ref[hd, grp, cols] for hd in range(PEER_HEADS)]
        bc = lambda x, r: jnp.broadcast_to(x[r:r + 1], (PACK, LANES)).astype(BF16)
        for ap in range(0, SUBLANES, 2):
            cnt_b = [[bc(cnt_g[hd], ap + j) for hd in range(PEER_HEADS)] for j in range(2)]
            e1_b = [[bc(e1_g[hd], ap + j) for hd in range(PEER_HEADS)] for j in range(2)]
            for bt in range(PEER_NKEYS // PACK):
                brow = slice(bt * PACK, (bt + 1) * PACK)
                rk = [r2_ref[hd, brow, cols] for hd in range(PEER_HEADS)]
                e2 = [e2_ref[hd, brow, cols] for hd in range(PEER_HEADS)]
                for j in range(2):
                    w = jnp.zeros((PACK, LANES), BF16)
                    for hd in range(PEER_HEADS):
                        w = w + jnp.where(rk[hd] < cnt_b[j][hd], e2[hd], 0.0) * e1_b[j][hd]
                    rows = pl.ds(pl.multiple_of((ag * SUBLANES + ap + j) * PEER_NKEYS + bt * PACK, PACK), PACK)
                    hx = ht_ref[rows, cols]
                    act = 0.5 * hx * (1.0 + lax.erf(hx * (2.0 ** -0.5)))
                    g_ref[rows, cols] = (w.astype(F32) * act).astype(BF16)
        return carry

    lax.fori_loop(0, nt * n_grp, gate, 0)
    acc_ref[...] += _dot(vt_ref[...], g_ref[...])

    @pl.when(e == pl.num_programs(1) - 1)
    def _finish():
        y_ref[...] = _rms(h_ref[...] + acc_ref[...].T, nfin_ref[...])


def _peer(h2d, nf, nfin, wq_t, k1, k2, u, vt, tn, ec):
    n = h2d.shape[0]
    assert ec % (SUBLANES * PEER_NKEYS) == 0 and tn % LANES == 0
    tok = pl.BlockSpec((tn, D_MODEL), lambda i, e: (i, 0))
    const2 = lambda i, e: (0, 0)
    const3 = lambda i, e: (0, 0, 0)
    vec = pl.BlockSpec((1, D_MODEL), const2)
    kspec = pl.BlockSpec((PEER_HEADS, PEER_NKEYS, PEER_HALF), const3)
    head_f32 = pltpu.VMEM((PEER_HEADS, PEER_NKEYS, tn), F32)
    head_bf16 = pltpu.VMEM((PEER_HEADS, PEER_NKEYS, tn), BF16)
    rank_tok = pltpu.VMEM((PEER_TOPK, PEER_HEADS, tn), F32)
    return pl.pallas_call(
        functools.partial(_peer_kernel, tn=tn, ec=ec),
        grid=(n // tn, PEER_N_EXPERTS // ec),
        in_specs=[tok, vec, vec,
                  pl.BlockSpec((PEER_HEADS * PEER_QDIM, D_MODEL), const2, pipeline_mode=pl.Buffered(1)),
                  kspec, kspec,
                  pl.BlockSpec((ec, D_MODEL), lambda i, e: (e, 0)),
                  pl.BlockSpec((D_MODEL, ec), lambda i, e: (0, e))],
        out_specs=tok,
        out_shape=jax.ShapeDtypeStruct((n, D_MODEL), F32),
        scratch_shapes=[pltpu.VMEM((tn, D_MODEL), BF16),
                        head_f32, head_f32, head_bf16, head_bf16, rank_tok, rank_tok,
                        pltpu.VMEM((PEER_HEADS, _HS_ROWS, tn), F32),
                        pltpu.VMEM((ec, tn), F32), pltpu.VMEM((ec, tn), BF16),
                        pltpu.VMEM((D_MODEL, tn), F32)],
        compiler_params=pltpu.CompilerParams(
            dimension_semantics=("parallel", "arbitrary"), vmem_limit_bytes=VMEM_LIMIT),
        name="peer",
    )(h2d, nf, nfin, wq_t, k1, k2, u, vt)


def _pack_params(norm_mix, w_in, gla_w_decay, gla_b_decay, gla_norm, swa_sinks, w_branch_a, w_branch_b,
                 w_out, norm_ffn, peer_w_q, peer_keys1, peer_keys2, peer_u, peer_v):
    off = np.cumsum((0, GLA_QK, GLA_QK, GLA_V, GLA_V, GLA_RANK, SWA_Q, SWA_KV, SWA_KV, D_MODEL, D_MODEL))
    col = lambda i: w_in[:, off[i]:off[i + 1]]
    lr = jnp.pad(col(4), ((0, 0), (0, LANES - GLA_RANK)))
    w_all = jnp.concatenate([col(0), col(1), col(2), col(3), col(5), col(6), col(7), col(8), col(9), lr],
                            axis=1).astype(BF16)
    wdec = jnp.pad(gla_w_decay, ((0, LANES - GLA_RANK), (0, 0))).astype(BF16)
    return dict(
        nm=norm_mix[None], w_all=w_all, wdec=wdec, bdec=gla_b_decay[None], gn=gla_norm[None],
        sinks=swa_sinks, wa=w_branch_a.astype(BF16), wb=w_branch_b.astype(BF16), wo=w_out.astype(BF16),
        nf=norm_ffn[None], wq_t=peer_w_q.T.astype(BF16), k1=peer_keys1.astype(BF16),
        k2=peer_keys2.astype(BF16), u=peer_u.astype(BF16), vt=peer_v.T.astype(BF16))


def _layer(x, p, nfin, s0, win_k, win_v, *, tn, gla_tb, peer_tn, peer_ec):
    bsz, t, _ = x.shape
    x2d = x.reshape(bsz * t, D_MODEL)
    gq, gk, gv, gg, la, sq, sk, sv, ga, gb = _inproj(x2d, p["nm"], p["w_all"], p["wdec"], p["bdec"], tn)
    r3 = lambda a: a.reshape(bsz, t, a.shape[-1])
    o_a, s_new = _gla(r3(gq), r3(gk), r3(gv), r3(la), s0, gla_tb)
    if win_k is None:
        o_b = _swa(r3(sq), r3(sk), r3(sv), p["sinks"])
        ln = min(WINDOW, t)
        new_k, new_v = r3(sk)[:, t - ln:], r3(sv)[:, t - ln:]
    else:
        wk = win_k.reshape(bsz, WINDOW, SWA_KV)
        wv = win_v.reshape(bsz, WINDOW, SWA_KV)
        o_b = _swa(r3(sq), r3(sk), r3(sv), p["sinks"], wk, wv)
        new_k = jnp.concatenate([wk, r3(sk)], axis=1)[:, -WINDOW:]
        new_v = jnp.concatenate([wv, r3(sv)], axis=1)[:, -WINDOW:]
    h = _merge(x2d, o_a.reshape(bsz * t, GLA_V), gg, o_b.reshape(bsz * t, SWA_Q), ga, gb,
               p["gn"], p["wa"], p["wb"], p["wo"], tn)
    y = _peer(h, p["nf"], nfin, p["wq_t"], p["k1"], p["k2"], p["u"], p["vt"], peer_tn, peer_ec)
    kv_shape = (bsz, -1, SWA_KV_HEADS, SWA_HEAD_DIM)
    return y.reshape(bsz, t, D_MODEL), new_k.reshape(kv_shape), new_v.reshape(kv_shape), s_new


def kernel(x_prompt, x_sample, cache_win_k, cache_win_v, state_gla, norm_mix, w_in, gla_w_decay, gla_b_decay,
           gla_norm, swa_sinks, w_branch_a, w_branch_b, w_out, norm_ffn, peer_w_q, peer_keys1, peer_keys2,
           peer_u, peer_v, norm_final):
    depth = w_in.shape[0]
    assert depth == 1, "the final norm is fused into the last layer's channel mixer; one layer supported"
    p = _pack_params(norm_mix[0], w_in[0], gla_w_decay[0], gla_b_decay[0], gla_norm[0], swa_sinks[0],
                     w_branch_a[0], w_branch_b[0], w_out[0], norm_ffn[0], peer_w_q[0], peer_keys1[0],
                     peer_keys2[0], peer_u[0], peer_v[0])
    nfin = norm_final[None]
    yp, pk, pv, ps = _layer(x_prompt, p, nfin, None, None, None,
                            tn=512, gla_tb=512, peer_tn=512, peer_ec=2048)
    ys, sk, sv, ss = _layer(x_sample, p, nfin, state_gla[0], cache_win_k[0], cache_win_v[0],
                            tn=512, gla_tb=x_sample.shape[1], peer_tn=512, peer_ec=2048)
    return (yp, ys, pk[None], pv[None], ps[None], sk[None], sv[None], ss[None])
```

```python
import functools

import jax
import jax.numpy as jnp
import numpy as np
from jax import lax
from jax.experimental import pallas as pl
from jax.experimental.pallas import tpu as pltpu

F32 = jnp.float32
BF16 = jnp.bfloat16

D_MODEL = 1024
GLA_HEADS = 4
GLA_DK = 128
GLA_DV = 256
GLA_RANK = 16
GLA_TAU = 16.0
SWA_HEADS = 16
SWA_KV_HEADS = 2
SWA_GROUP = SWA_HEADS // SWA_KV_HEADS
SWA_HEAD_DIM = 64
WINDOW = 128
PEER_HEADS = 8
PEER_NKEYS = 128
PEER_N_EXPERTS = PEER_NKEYS * PEER_NKEYS
PEER_QDIM = 256
PEER_HALF = PEER_QDIM // 2
PEER_TOPK = 16
EPS = 1e-6

GLA_QK = GLA_HEADS * GLA_DK
GLA_V = GLA_HEADS * GLA_DV
SWA_Q = SWA_HEADS * SWA_HEAD_DIM
SWA_KV = SWA_KV_HEADS * SWA_HEAD_DIM

LANES = 128
SUBLANES = 8
GLA_CHUNK = 128
GLA_SAFE_DECAY = 60.0
VMEM_LIMIT = 56 * 1024 * 1024
NEG_BIG = -1e30

_NT = (((1,), (1,)), ((), ()))


def _rms(x, g):
    return x * lax.rsqrt(jnp.mean(x * x, axis=-1, keepdims=True) + EPS) * g


def _dot(a, b):
    return jnp.dot(a, b, preferred_element_type=F32)


def _dot_nt(a, b):
    return lax.dot_general(a, b, _NT, preferred_element_type=F32)


def _sigmoid(x):
    return 1.0 / (1.0 + jnp.exp(-x))


_P_GQ, _P_GK, _P_GV, _P_GG, _P_SQ, _P_SK, _P_SV, _P_GA, _P_GB, _P_LR, _P_END = (
    0, 512, 1024, 2048, 3072, 4096, 4224, 4352, 5376, 6400, 6528)


def _inproj_kernel(x_ref, nm_ref, w_ref, wdec_ref, bdec_ref,
                   gq_ref, gk_ref, gv_ref, gg_ref, la_ref, sq_ref, sk_ref, sv_ref, ga_ref, gb_ref):
    xb = _rms(x_ref[...], nm_ref[...]).astype(BF16)

    def proj(lo, hi):
        return _dot(xb, w_ref[:, lo:hi])

    gq_ref[...] = proj(_P_GQ, _P_GK)
    gk_ref[...] = proj(_P_GK, _P_GV)
    gv_ref[...] = proj(_P_GV, _P_GG)
    gg_ref[...] = proj(_P_GG, _P_SQ)
    sq_ref[...] = proj(_P_SQ, _P_SK)
    sk_ref[...] = proj(_P_SK, _P_SV)
    sv_ref[...] = proj(_P_SV, _P_GA)
    ga_ref[...] = proj(_P_GA, _P_GB)
    gb_ref[...] = proj(_P_GB, _P_LR)
    glr = proj(_P_LR, _P_END)
    z = _dot(glr.astype(BF16), wdec_ref[...]) + bdec_ref[...]
    la_ref[...] = (jnp.minimum(z, 0.0) - jnp.log(1.0 + jnp.exp(-jnp.abs(z)))) * (1.0 / GLA_TAU)


def _inproj(x2d, nm, w_all, wdec, bdec, tn):
    n = x2d.shape[0]
    widths = (GLA_QK, GLA_QK, GLA_V, GLA_V, GLA_QK, SWA_Q, SWA_KV, SWA_KV, D_MODEL, D_MODEL)
    const = lambda i: (0, 0)
    return pl.pallas_call(
        _inproj_kernel,
        grid=(n // tn,),
        in_specs=[
            pl.BlockSpec((tn, D_MODEL), lambda i: (i, 0)),
            pl.BlockSpec((1, D_MODEL), const),
            pl.BlockSpec((D_MODEL, _P_END), const, pipeline_mode=pl.Buffered(1)),
            pl.BlockSpec((LANES, GLA_QK), const),
            pl.BlockSpec((1, GLA_QK), const),
        ],
        out_specs=[pl.BlockSpec((tn, w), lambda i: (i, 0)) for w in widths],
        out_shape=[jax.ShapeDtypeStruct((n, w), F32) for w in widths],
        compiler_params=pltpu.CompilerParams(
            dimension_semantics=("parallel",), vmem_limit_bytes=VMEM_LIMIT),
        name="in_proj",
    )(x2d, nm, w_all, wdec, bdec)


def _gla_kernel(*refs, n_chunks, t_valid, has_s0):
    if has_s0:
        q_ref, k_ref, v_ref, la_ref, s0_ref, o_ref, so_ref, s_ref = refs
    else:
        q_ref, k_ref, v_ref, la_ref, o_ref, so_ref, s_ref = refs
    j = pl.program_id(1)
    c = GLA_CHUNK

    @pl.when(j == 0)
    def _():
        if has_s0:
            s_ref[...] = s0_ref[0]
        else:
            s_ref[...] = jnp.zeros_like(s_ref)

    row = lax.broadcasted_iota(jnp.int32, (c, c), 0)
    col = lax.broadcasted_iota(jnp.int32, (c, c), 1)
    causal = row >= col
    tril = causal.astype(BF16)
    scale = GLA_DK ** -0.5

    def load(ref, r0, lo, w):
        if t_valid >= c:
            return ref[0, r0:r0 + c, lo:lo + w]
        t = ref[0, :, lo:lo + w]
        return jnp.concatenate([t, jnp.zeros((c - t_valid, w), F32)], axis=0)

    def pairwise_scores(qs, k, b):
        def score_row(i, a_t):
            pick = row == i
            q_i = jnp.sum(jnp.where(pick, qs, 0.0), axis=0, keepdims=True)
            b_i = jnp.sum(jnp.where(pick, b, 0.0), axis=0, keepdims=True)
            rel = jnp.where(row <= i, b_i - b, -jnp.inf)
            s_col = jnp.sum(k * jnp.exp(rel) * q_i, axis=1, keepdims=True)
            return jnp.where(col == i, s_col, a_t)
        return lax.fori_loop(0, c, score_row, jnp.zeros((c, c), F32)).T

    def run(factored):
        for ci in range(n_chunks):
            r0 = ci * c
            for h in range(GLA_HEADS):
                q = load(q_ref, r0, h * GLA_DK, GLA_DK)
                k = load(k_ref, r0, h * GLA_DK, GLA_DK)
                v = load(v_ref, r0, h * GLA_DV, GLA_DV).astype(BF16)
                la = load(la_ref, r0, h * GLA_DK, GLA_DK)
                la_hi = la.astype(BF16)
                la_lo = (la - la_hi.astype(F32)).astype(BF16)
                b = _dot(tril, la_hi) + _dot(tril, la_lo)
                qs = q * scale
                qd = (qs * jnp.exp(b)).astype(BF16)
                if factored:
                    a = jnp.where(causal, _dot_nt(qd, (k * jnp.exp(-b)).astype(BF16)), 0.0)
                else:
                    a = pairwise_scores(qs, k, b)
                s_old = s_ref[h]
                o = _dot(qd, s_old.astype(BF16)) + _dot(a.astype(BF16), v)
                if t_valid >= c:
                    o_ref[0, r0:r0 + c, h * GLA_DV:(h + 1) * GLA_DV] = o
                else:
                    o_ref[0, :, h * GLA_DV:(h + 1) * GLA_DV] = o[:t_valid]
                bl = b[c - 1:c, :]
                kd_t = (k * jnp.exp(bl - b)).T.astype(BF16)
                dec_t = jnp.broadcast_to(jnp.exp(bl), (c, GLA_DK)).T
                dec = jnp.concatenate([dec_t] * (GLA_DV // c), axis=1)
                s_ref[h] = dec * s_old + _dot(kd_t, v)

    worst = jnp.float32(0.0)
    for ci in range(n_chunks):
        rows = slice(ci * c, (ci + 1) * c) if t_valid >= c else slice(None)
        worst = jnp.maximum(worst, jnp.max(-jnp.sum(la_ref[0, rows, :], axis=0, keepdims=True)))
    safe = worst < GLA_SAFE_DECAY
    pl.when(safe)(lambda: run(True))
    pl.when(jnp.logical_not(safe))(lambda: run(False))

    @pl.when(j == pl.num_programs(1) - 1)
    def _():
        so_ref[0] = s_ref[...]


def _gla(gq, gk, gv, la, s0, tb):
    bsz, t, _ = gq.shape
    t_valid = min(tb, GLA_CHUNK)
    n_chunks = max(tb // GLA_CHUNK, 1)
    has_s0 = s0 is not None
    tok = lambda w: pl.BlockSpec((1, tb, w), lambda b, j: (b, j, 0))
    st = pl.BlockSpec((1, GLA_HEADS, GLA_DK, GLA_DV), lambda b, j: (b, 0, 0, 0))
    in_specs = [tok(GLA_QK), tok(GLA_QK), tok(GLA_V), tok(GLA_QK)] + ([st] if has_s0 else [])
    args = (gq, gk, gv, la) + ((s0,) if has_s0 else ())
    return pl.pallas_call(
        functools.partial(_gla_kernel, n_chunks=n_chunks, t_valid=t_valid, has_s0=has_s0),
        grid=(bsz, t // tb),
        in_specs=in_specs,
        out_specs=[tok(GLA_V), st],
        out_shape=[jax.ShapeDtypeStruct((bsz, t, GLA_V), F32),
                   jax.ShapeDtypeStruct((bsz, GLA_HEADS, GLA_DK, GLA_DV), F32)],
        scratch_shapes=[pltpu.VMEM((GLA_HEADS, GLA_DK, GLA_DV), F32)],
        compiler_params=pltpu.CompilerParams(
            dimension_semantics=("parallel", "arbitrary"), vmem_limit_bytes=VMEM_LIMIT),
        name="gla",
    )(*args)


def _swa_kernel(sink_ref, q_ref, kc_ref, vc_ref, kp_ref, vp_ref, o_ref, *, tq, nb, prev_is_block):
    i = pl.program_id(1)
    w = WINDOW
    nk = 2 * w
    pairs = SWA_GROUP // 2
    t_idx = lax.broadcasted_iota(jnp.int32, (tq, nk), 0)
    j_idx = lax.broadcasted_iota(jnp.int32, (tq, nk), 1)
    d = j_idx - t_idx
    band = (d >= 1) & (d <= w)
    lane = lax.broadcasted_iota(jnp.int32, (nk, LANES), 1)
    left = lane < SWA_HEAD_DIM

    def window(prev_ref, cur_ref, blk):
        cur = cur_ref[0, blk * tq:(blk + 1) * tq, :]
        if tq < w:
            cur = jnp.concatenate([cur, jnp.zeros((w - tq, SWA_KV), F32)], axis=0)
        prev = prev_ref[0] if blk == 0 else cur_ref[0, (blk - 1) * tq:blk * tq, :]
        full = jnp.concatenate([prev, cur], axis=0)
        return full, pltpu.roll(full, SWA_HEAD_DIM, 1)

    def block_diag(full, rot, kvh):
        a, b = (full, rot) if kvh == 0 else (rot, full)
        return jnp.concatenate([jnp.where(left, a, 0.0), jnp.where(left, 0.0, b)], axis=0).astype(BF16)

    for blk in range(nb):
        vis = band
        if prev_is_block and blk == 0:
            vis = vis & ((j_idx >= w) | (i > 0))
        vis = jnp.concatenate([vis] * pairs, axis=0)
        k_all, k_rot = window(kp_ref, kc_ref, blk)
        v_all, v_rot = window(vp_ref, vc_ref, blk)
        rows = slice(blk * tq, (blk + 1) * tq)
        for kvh in range(SWA_KV_HEADS):
            kbd = block_diag(k_all, k_rot, kvh)
            vbd = block_diag(v_all, v_rot, kvh)
            base = kvh * SWA_GROUP * SWA_HEAD_DIM
            qs = jnp.concatenate(
                [q_ref[0, rows, base + p * LANES: base + (p + 1) * LANES] for p in range(pairs)], axis=0)
            s = _dot_nt(qs.astype(BF16), kbd) * (SWA_HEAD_DIM ** -0.5)
            probs = []
            for e in range(2):
                se = jnp.where(vis, s[:, e * nk:(e + 1) * nk], NEG_BIG)
                sink = jnp.concatenate(
                    [jnp.full((tq, 1), sink_ref[kvh * SWA_GROUP + 2 * p + e], F32) for p in range(pairs)],
                    axis=0)
                m = jnp.maximum(jnp.max(se, axis=-1, keepdims=True), sink)
                ex = jnp.exp(se - m)
                den = jnp.sum(ex, axis=-1, keepdims=True) + jnp.exp(sink - m)
                probs.append(ex / den)
            o = _dot(jnp.concatenate(probs, axis=1).astype(BF16), vbd)
            for p in range(pairs):
                o_ref[0, rows, base + p * LANES: base + (p + 1) * LANES] = o[p * tq:(p + 1) * tq]


def _swa(sq, sk, sv, sinks, prev_k=None, prev_v=None, nb=1):
    bsz, t, _ = sq.shape
    prev_is_block = prev_k is None
    if prev_is_block:
        tq = WINDOW
        prev_k, prev_v = sk, sv
        prev_spec = pl.BlockSpec((1, WINDOW, SWA_KV), lambda b, i: (b, jnp.maximum(i * nb - 1, 0), 0))
    else:
        tq = t
        assert nb == 1
        prev_spec = pl.BlockSpec((1, WINDOW, SWA_KV), lambda b, i: (b, 0, 0))
    cur_spec = pl.BlockSpec((1, nb * tq, SWA_KV), lambda b, i: (b, i, 0))
    q_spec = pl.BlockSpec((1, nb * tq, SWA_Q), lambda b, i: (b, i, 0))
    return pl.pallas_call(
        functools.partial(_swa_kernel, tq=tq, nb=nb, prev_is_block=prev_is_block),
        grid=(bsz, t // (nb * tq)),
        in_specs=[pl.BlockSpec(memory_space=pltpu.SMEM), q_spec, cur_spec, cur_spec, prev_spec, prev_spec],
        out_specs=q_spec,
        out_shape=jax.ShapeDtypeStruct((bsz, t, SWA_Q), F32),
        compiler_params=pltpu.CompilerParams(
            dimension_semantics=("parallel", "arbitrary"), vmem_limit_bytes=VMEM_LIMIT),
        name="swa",
    )(sinks, sq, sk, sv, prev_k, prev_v)


def _merge_kernel(x_ref, oa_ref, gg_ref, ob_ref, ga_ref, gb_ref, gn_ref, wa_ref, wb_ref, wo_ref, h_ref):
    parts = []
    for hd in range(GLA_HEADS):
        sl = slice(hd * GLA_DV, (hd + 1) * GLA_DV)
        g = gg_ref[:, sl]
        parts.append(_rms(oa_ref[:, sl], gn_ref[...]) * (g * _sigmoid(g)))
    oa = jnp.concatenate(parts, axis=1).astype(BF16)
    br_a = _dot(oa, wa_ref[...])
    br_b = _dot(ob_ref[...].astype(BF16), wb_ref[...])
    merged = _sigmoid(ga_ref[...]) * br_a + _sigmoid(gb_ref[...]) * br_b
    h_ref[...] = x_ref[...] + _dot(merged.astype(BF16), wo_ref[...])


def _merge(x2d, oa, gg, ob, ga, gb, gn, wa, wb, wo, tn):
    n = x2d.shape[0]
    tok = pl.BlockSpec((tn, D_MODEL), lambda i: (i, 0))
    const = lambda i: (0, 0)
    wspec = pl.BlockSpec((D_MODEL, D_MODEL), const)
    return pl.pallas_call(
        _merge_kernel,
        grid=(n // tn,),
        in_specs=[tok] * 6 + [pl.BlockSpec((1, GLA_DV), const), wspec, wspec, wspec],
        out_specs=tok,
        out_shape=jax.ShapeDtypeStruct((n, D_MODEL), F32),
        compiler_params=pltpu.CompilerParams(
            dimension_semantics=("parallel",), vmem_limit_bytes=VMEM_LIMIT),
        name="merge",
    )(x2d, oa, gg, ob, ga, gb, gn, wa, wb, wo)


def _sort16_pairs():
    pairs = []
    n = 16
    p = 1
    while p < n:
        k = p
        while k >= 1:
            for j in range(k % p, n - k, 2 * k):
                for i in range(min(k, n - j - k)):
                    if (i + j) // (2 * p) == (i + j + k) // (2 * p):
                        pairs.append((i + j, i + j + k))
            k //= 2
        p *= 2
    return pairs


_SORT16 = _sort16_pairs()


def _bitonic_top16(xa, xb):
    z = [jnp.maximum(xa[i], xb[15 - i]) for i in range(16)]
    dist = 8
    while dist >= 1:
        for i in range(16):
            if i & dist == 0:
                hi = jnp.maximum(z[i], z[i + dist])
                lo = jnp.minimum(z[i], z[i + dist])
                z[i], z[i + dist] = hi, lo
        dist //= 2
    return z


def _top16_rows(s):
    x = [s[SUBLANES * v:SUBLANES * (v + 1), :] for v in range(16)]
    for i, j in _SORT16:
        hi = jnp.maximum(x[i], x[j])
        lo = jnp.minimum(x[i], x[j])
        x[i], x[j] = hi, lo
    shift = 1
    while shift < SUBLANES:
        x = _bitonic_top16(x, [pltpu.roll(t, shift, 0) for t in x])
        shift *= 2
    return [t[SUBLANES - 1:SUBLANES, :] for t in x]


_CAND_Q = [PEER_TOPK // (p + 1) for p in range(PEER_TOPK)]


PACK = 2 * SUBLANES
_HS_TAU, _HS_ZINV, _HS_TOP1, _HS_ROWS = PEER_TOPK, PEER_TOPK + 1, PEER_TOPK + 2, PEER_TOPK + SUBLANES


def _peer_kernel(h_ref, nf_ref, nfin_ref, wq_ref, k1_ref, k2_ref, u_ref, vt_ref, y_ref,
                 hn_ref, sa_ref, sb_ref, r2_ref, e2_ref, v1_ref, v2_ref, hs_ref,
                 ca_ref, ea_ref, ht_ref, g_ref, acc_ref, *, tn, ec):
    e = pl.program_id(1)
    na = ec // PEER_NKEYS
    nt = tn // LANES
    rb = 4 * SUBLANES

    @pl.when(e == 0)
    def _routing():
        hn_ref[...] = _rms(h_ref[...], nf_ref[...]).astype(BF16)
        acc_ref[...] = jnp.zeros_like(acc_ref)

        for hd in range(PEER_HEADS):
            wq = wq_ref[hd * PEER_QDIM:(hd + 1) * PEER_QDIM, :]
            qt = _dot_nt(wq, hn_ref[...])
            s1 = _dot(k1_ref[hd], qt[:PEER_HALF].astype(BF16))
            s2 = _dot(k2_ref[hd], qt[PEER_HALF:].astype(BF16))
            sa_ref[hd] = s1
            sb_ref[hd] = s2
            for p, r in enumerate(_top16_rows(s1)):
                v1_ref[p, hd:hd + 1, :] = r
            for p, r in enumerate(_top16_rows(s2)):
                v2_ref[p, hd:hd + 1, :] = r
                hs_ref[hd, p:p + 1, :] = r

        v1 = [v1_ref[p] for p in range(PEER_TOPK)]
        v2 = [v2_ref[q] for q in range(PEER_TOPK)]
        ninf = jnp.full((PEER_HEADS, tn), -jnp.inf, F32)
        best = [v1[0] + v2[q] for q in range(PEER_TOPK)]
        for p in range(1, PEER_TOPK):
            row = [v1[p] + v2[q] if q < _CAND_Q[p] else ninf for q in range(PEER_TOPK)]
            best = _bitonic_top16(best, row)
        tau = best[PEER_TOPK - 1]
        top = best[0]
        zsum = jnp.zeros((PEER_HEADS, tn), F32)
        for p in range(PEER_TOPK):
            for q in range(_CAND_Q[p]):
                cnd = v1[p] + v2[q]
                zsum = zsum + jnp.where(cnd >= tau, jnp.exp(cnd - top), 0.0)
        zinv = 1.0 / zsum
        for hd in range(PEER_HEADS):
            hs_ref[hd, _HS_TAU:_HS_TAU + 1, :] = tau[hd:hd + 1]
            hs_ref[hd, _HS_ZINV:_HS_ZINV + 1, :] = zinv[hd:hd + 1]
            hs_ref[hd, _HS_TOP1:_HS_TOP1 + 1, :] = v1[0][hd:hd + 1]

        def staircase(idx, carry):
            hd = idx // (PEER_NKEYS // rb)
            rows = pl.ds(pl.multiple_of((idx % (PEER_NKEYS // rb)) * rb, rb), rb)
            s1 = sa_ref[hd, rows, :]
            s2 = sb_ref[hd, rows, :]
            tau_h = hs_ref[hd, _HS_TAU:_HS_TAU + 1, :]
            rank = jnp.zeros((rb, tn), F32)
            cnt = jnp.zeros((rb, tn), F32)
            for q in range(PEER_TOPK):
                v2q = hs_ref[hd, q:q + 1, :]
                rank = rank + jnp.where(v2q > s2, 1.0, 0.0)
                cnt = cnt + jnp.where(s1 + v2q >= tau_h, 1.0, 0.0)
            r2_ref[hd, rows, :] = rank.astype(BF16)
            e2_ref[hd, rows, :] = (jnp.exp(s2 - hs_ref[hd, 0:1, :])
                                   * hs_ref[hd, _HS_ZINV:_HS_ZINV + 1, :]).astype(BF16)
            sa_ref[hd, rows, :] = cnt
            sb_ref[hd, rows, :] = jnp.exp(s1 - hs_ref[hd, _HS_TOP1:_HS_TOP1 + 1, :])
            return carry

        lax.fori_loop(0, PEER_HEADS * (PEER_NKEYS // rb), staircase, 0)

    ht_ref[...] = _dot_nt(u_ref[...], hn_ref[...])
    a_rows = pl.ds(pl.multiple_of(e * na, na), na)
    for hd in range(PEER_HEADS):
        ca_ref[hd] = sa_ref[hd, a_rows, :]
        ea_ref[hd] = sb_ref[hd, a_rows, :]

    ga = 4
    gb = 4
    n_bg = PEER_NKEYS // (gb * PACK)

    def gate(idx, carry):
        cols = pl.ds(pl.multiple_of((idx // n_bg) * LANES, LANES), LANES)
        b0 = (idx % n_bg) * (gb * PACK)
        bc = lambda ref, hd, r: jnp.broadcast_to(ref[hd, r:r + 1, cols], (PACK, LANES)).astype(BF16)
        for a_blk in range(0, na, ga):
            w = [[jnp.zeros((PACK, LANES), BF16) for _ in range(gb)] for _ in range(ga)]
            for hd in range(PEER_HEADS):
                rk = [r2_ref[hd, pl.ds(pl.multiple_of(b0 + t * PACK, PACK), PACK), cols] for t in range(gb)]
                e2 = [e2_ref[hd, pl.ds(pl.multiple_of(b0 + t * PACK, PACK), PACK), cols] for t in range(gb)]
                for i in range(ga):
                    cnt_b = bc(ca_ref, hd, a_blk + i)
                    e1_b = bc(ea_ref, hd, a_blk + i)
                    for t in range(gb):
                        w[i][t] = w[i][t] + jnp.where(rk[t] < cnt_b, e2[t], 0.0) * e1_b
            for i in range(ga):
                for t in range(gb):
                    rows = pl.ds(pl.multiple_of((a_blk + i) * PEER_NKEYS + b0 + t * PACK, PACK), PACK)
                    hx = ht_ref[rows, cols]
                    act = 0.5 * hx * (1.0 + lax.erf(hx * (2.0 ** -0.5)))
                    g_ref[rows, cols] = (w[i][t].astype(F32) * act).astype(BF16)
        return carry

    lax.fori_loop(0, nt * n_bg, gate, 0)
    acc_ref[...] += _dot(vt_ref[...], g_ref[...])

    @pl.when(e == pl.num_programs(1) - 1)
    def _finish():
        y_ref[...] = _rms(h_ref[...] + acc_ref[...].T, nfin_ref[...])


def _peer(h2d, nf, nfin, wq_t, k1, k2, u, vt, tn, ec):
    n = h2d.shape[0]
    assert ec % (SUBLANES * PEER_NKEYS) == 0 and tn % LANES == 0
    tok = pl.BlockSpec((tn, D_MODEL), lambda i, e: (i, 0))
    const2 = lambda i, e: (0, 0)
    const3 = lambda i, e: (0, 0, 0)
    vec = pl.BlockSpec((1, D_MODEL), const2)
    kspec = pl.BlockSpec((PEER_HEADS, PEER_NKEYS, PEER_HALF), const3)
    head_f32 = pltpu.VMEM((PEER_HEADS, PEER_NKEYS, tn), F32)
    head_bf16 = pltpu.VMEM((PEER_HEADS, PEER_NKEYS, tn), BF16)
    rank_tok = pltpu.VMEM((PEER_TOPK, PEER_HEADS, tn), F32)
    stage = pltpu.VMEM((PEER_HEADS, ec // PEER_NKEYS, tn), F32)
    return pl.pallas_call(
        functools.partial(_peer_kernel, tn=tn, ec=ec),
        grid=(n // tn, PEER_N_EXPERTS // ec),
        in_specs=[tok, vec, vec,
                  pl.BlockSpec((PEER_HEADS * PEER_QDIM, D_MODEL), const2, pipeline_mode=pl.Buffered(1)),
                  kspec, kspec,
                  pl.BlockSpec((ec, D_MODEL), lambda i, e: (e, 0)),
                  pl.BlockSpec((D_MODEL, ec), lambda i, e: (0, e))],
        out_specs=tok,
        out_shape=jax.ShapeDtypeStruct((n, D_MODEL), F32),
        scratch_shapes=[pltpu.VMEM((tn, D_MODEL), BF16),
                        head_f32, head_f32, head_bf16, head_bf16, rank_tok, rank_tok,
                        pltpu.VMEM((PEER_HEADS, _HS_ROWS, tn), F32),
                        stage, stage, pltpu.VMEM((ec, tn), F32), pltpu.VMEM((ec, tn), BF16),
                        pltpu.VMEM((D_MODEL, tn), F32)],
        compiler_params=pltpu.CompilerParams(
            dimension_semantics=("parallel", "arbitrary"), vmem_limit_bytes=VMEM_LIMIT),
        name="peer",
    )(h2d, nf, nfin, wq_t, k1, k2, u, vt)


def _pack_params(norm_mix, w_in, gla_w_decay, gla_b_decay, gla_norm, swa_sinks, w_branch_a, w_branch_b,
                 w_out, norm_ffn, peer_w_q, peer_keys1, peer_keys2, peer_u, peer_v):
    off = np.cumsum((0, GLA_QK, GLA_QK, GLA_V, GLA_V, GLA_RANK, SWA_Q, SWA_KV, SWA_KV, D_MODEL, D_MODEL))
    col = lambda i: w_in[:, off[i]:off[i + 1]]
    lr = jnp.pad(col(4), ((0, 0), (0, LANES - GLA_RANK)))
    w_all = jnp.concatenate([col(0), col(1), col(2), col(3), col(5), col(6), col(7), col(8), col(9), lr],
                            axis=1).astype(BF16)
    wdec = jnp.pad(gla_w_decay, ((0, LANES - GLA_RANK), (0, 0))).astype(BF16)
    return dict(
        nm=norm_mix[None], w_all=w_all, wdec=wdec, bdec=gla_b_decay[None], gn=gla_norm[None],
        sinks=swa_sinks, wa=w_branch_a.astype(BF16), wb=w_branch_b.astype(BF16), wo=w_out.astype(BF16),
        nf=norm_ffn[None], wq_t=peer_w_q.T.astype(BF16), k1=peer_keys1.astype(BF16),
        k2=peer_keys2.astype(BF16), u=peer_u.astype(BF16), vt=peer_v.T.astype(BF16))


def _layer(x, p, nfin, s0, win_k, win_v, *, tn, gla_tb, swa_nb, peer_tn, peer_ec):
    bsz, t, _ = x.shape
    x2d = x.reshape(bsz * t, D_MODEL)
    gq, gk, gv, gg, la, sq, sk, sv, ga, gb = _inproj(x2d, p["nm"], p["w_all"], p["wdec"], p["bdec"], tn)
    r3 = lambda a: a.reshape(bsz, t, a.shape[-1])
    o_a, s_new = _gla(r3(gq), r3(gk), r3(gv), r3(la), s0, gla_tb)
    if win_k is None:
        o_b = _swa(r3(sq), r3(sk), r3(sv), p["sinks"], nb=swa_nb)
        ln = min(WINDOW, t)
        new_k, new_v = r3(sk)[:, t - ln:], r3(sv)[:, t - ln:]
    else:
        wk = win_k.reshape(bsz, WINDOW, SWA_KV)
        wv = win_v.reshape(bsz, WINDOW, SWA_KV)
        o_b = _swa(r3(sq), r3(sk), r3(sv), p["sinks"], wk, wv)
        new_k = jnp.concatenate([wk, r3(sk)], axis=1)[:, -WINDOW:]
        new_v = jnp.concatenate([wv, r3(sv)], axis=1)[:, -WINDOW:]
    h = _merge(x2d, o_a.reshape(bsz * t, GLA_V), gg, o_b.reshape(bsz * t, SWA_Q), ga, gb,
               p["gn"], p["wa"], p["wb"], p["wo"], tn)
    y = _peer(h, p["nf"], nfin, p["wq_t"], p["k1"], p["k2"], p["u"], p["vt"], peer_tn, peer_ec)
    kv_shape = (bsz, -1, SWA_KV_HEADS, SWA_HEAD_DIM)
    return y.reshape(bsz, t, D_MODEL), new_k.reshape(kv_shape), new_v.reshape(kv_shape), s_new


def kernel(x_prompt, x_sample, cache_win_k, cache_win_v, state_gla, norm_mix, w_in, gla_w_decay, gla_b_decay,
           gla_norm, swa_sinks, w_branch_a, w_branch_b, w_out, norm_ffn, peer_w_q, peer_keys1, peer_keys2,
           peer_u, peer_v, norm_final):
    depth = w_in.shape[0]
    assert depth == 1, "the final norm is fused into the last layer's channel mixer; one layer supported"
    p = _pack_params(norm_mix[0], w_in[0], gla_w_decay[0], gla_b_decay[0], gla_norm[0], swa_sinks[0],
                     w_branch_a[0], w_branch_b[0], w_out[0], norm_ffn[0], peer_w_q[0], peer_keys1[0],
                     peer_keys2[0], peer_u[0], peer_v[0])
    nfin = norm_final[None]
    yp, pk, pv, ps = _layer(x_prompt, p, nfin, None, None, None,
                            tn=512, gla_tb=512, swa_nb=4, peer_tn=512, peer_ec=2048)
    ys, sk, sv, ss = _layer(x_sample, p, nfin, state_gla[0], cache_win_k[0], cache_win_v[0],
                            tn=512, gla_tb=x_sample.shape[1], swa_nb=1, peer_tn=512, peer_ec=2048)
    return (yp, ys, pk[None], pv[None], ps[None], sk[None], sv[None], ss[None])
```

```python
import functools

import jax
import jax.numpy as jnp
import numpy as np
from jax import lax
from jax.experimental import pallas as pl
from jax.experimental.pallas import tpu as pltpu

F32 = jnp.float32
BF16 = jnp.bfloat16

D_MODEL = 1024
GLA_HEADS = 4
GLA_DK = 128
GLA_DV = 256
GLA_RANK = 16
GLA_TAU = 16.0
SWA_HEADS = 16
SWA_KV_HEADS = 2
SWA_GROUP = SWA_HEADS // SWA_KV_HEADS
SWA_HEAD_DIM = 64
WINDOW = 128
PEER_HEADS = 8
PEER_NKEYS = 128
PEER_N_EXPERTS = PEER_NKEYS * PEER_NKEYS
PEER_QDIM = 256
PEER_HALF = PEER_QDIM // 2
PEER_TOPK = 16
EPS = 1e-6

GLA_QK = GLA_HEADS * GLA_DK
GLA_V = GLA_HEADS * GLA_DV
SWA_Q = SWA_HEADS * SWA_HEAD_DIM
SWA_KV = SWA_KV_HEADS * SWA_HEAD_DIM

LANES = 128
SUBLANES = 8
GLA_CHUNK = 128
GLA_HEAD_GROUP = 4
GLA_SAFE_DECAY = 60.0
VMEM_LIMIT = 56 * 1024 * 1024
NEG_BIG = -1e30

_NT = (((1,), (1,)), ((), ()))


def _rms(x, g):
    return x * lax.rsqrt(jnp.mean(x * x, axis=-1, keepdims=True) + EPS) * g


def _dot(a, b):
    return jnp.dot(a, b, preferred_element_type=F32)


def _dot_nt(a, b):
    return lax.dot_general(a, b, _NT, preferred_element_type=F32)


def _sigmoid(x):
    return 1.0 / (1.0 + jnp.exp(-x))


_P_GQ, _P_GK, _P_GV, _P_GG, _P_SQ, _P_SK, _P_SV, _P_GA, _P_GB, _P_LR, _P_END = (
    0, 512, 1024, 2048, 3072, 4096, 4224, 4352, 5376, 6400, 6528)


def _inproj_kernel(x_ref, nm_ref, w_ref, wdec_ref, bdec_ref,
                   gq_ref, gk_ref, gv_ref, gg_ref, la_ref, sq_ref, sk_ref, sv_ref, ga_ref, gb_ref):
    xb = _rms(x_ref[...], nm_ref[...]).astype(BF16)

    def proj(lo, hi):
        return _dot(xb, w_ref[:, lo:hi])

    gq_ref[...] = proj(_P_GQ, _P_GK)
    gk_ref[...] = proj(_P_GK, _P_GV)
    gv_ref[...] = proj(_P_GV, _P_GG)
    gg_ref[...] = proj(_P_GG, _P_SQ)
    sq_ref[...] = proj(_P_SQ, _P_SK)
    sk_ref[...] = proj(_P_SK, _P_SV)
    sv_ref[...] = proj(_P_SV, _P_GA)
    ga_ref[...] = proj(_P_GA, _P_GB)
    gb_ref[...] = proj(_P_GB, _P_LR)
    glr = proj(_P_LR, _P_END)
    z = _dot(glr.astype(BF16), wdec_ref[...]) + bdec_ref[...]
    la_ref[...] = (jnp.minimum(z, 0.0) - jnp.log(1.0 + jnp.exp(-jnp.abs(z)))) * (1.0 / GLA_TAU)


def _inproj(x2d, nm, w_all, wdec, bdec, tn):
    n = x2d.shape[0]
    widths = (GLA_QK, GLA_QK, GLA_V, GLA_V, GLA_QK, SWA_Q, SWA_KV, SWA_KV, D_MODEL, D_MODEL)
    const = lambda i: (0, 0)
    return pl.pallas_call(
        _inproj_kernel,
        grid=(n // tn,),
        in_specs=[
            pl.BlockSpec((tn, D_MODEL), lambda i: (i, 0)),
            pl.BlockSpec((1, D_MODEL), const),
            pl.BlockSpec((D_MODEL, _P_END), const, pipeline_mode=pl.Buffered(1)),
            pl.BlockSpec((LANES, GLA_QK), const),
            pl.BlockSpec((1, GLA_QK), const),
        ],
        out_specs=[pl.BlockSpec((tn, w), lambda i: (i, 0)) for w in widths],
        out_shape=[jax.ShapeDtypeStruct((n, w), F32) for w in widths],
        compiler_params=pltpu.CompilerParams(
            dimension_semantics=("parallel",), vmem_limit_bytes=VMEM_LIMIT),
        name="in_proj",
    )(x2d, nm, w_all, wdec, bdec)


def _gla_kernel(*refs, n_chunks, t_valid, has_s0):
    if has_s0:
        q_ref, k_ref, v_ref, la_ref, s0_ref, o_ref, so_ref, s_ref = refs
    else:
        q_ref, k_ref, v_ref, la_ref, o_ref, so_ref, s_ref = refs
    j = pl.program_id(1)
    c = GLA_CHUNK

    @pl.when(j == 0)
    def _():
        if has_s0:
            s_ref[...] = s0_ref[0]
        else:
            s_ref[...] = jnp.zeros_like(s_ref)

    row = lax.broadcasted_iota(jnp.int32, (c, c), 0)
    col = lax.broadcasted_iota(jnp.int32, (c, c), 1)
    causal = row >= col
    tril = causal.astype(BF16)
    scale = GLA_DK ** -0.5

    def load(ref, r0, lo, w):
        if t_valid >= c:
            return ref[0, r0:r0 + c, lo:lo + w]
        t = ref[0, :, lo:lo + w]
        return jnp.concatenate([t, jnp.zeros((c - t_valid, w), F32)], axis=0)

    def pairwise_scores(qs, k, b):
        def score_row(i, a_t):
            pick = row == i
            q_i = jnp.sum(jnp.where(pick, qs, 0.0), axis=0, keepdims=True)
            b_i = jnp.sum(jnp.where(pick, b, 0.0), axis=0, keepdims=True)
            rel = jnp.where(row <= i, b_i - b, -jnp.inf)
            s_col = jnp.sum(k * jnp.exp(rel) * q_i, axis=1, keepdims=True)
            return jnp.where(col == i, s_col, a_t)
        return lax.fori_loop(0, c, score_row, jnp.zeros((c, c), F32)).T

    def run(factored):
        groups = [range(h0, h0 + GLA_HEAD_GROUP) for h0 in range(0, GLA_HEADS, GLA_HEAD_GROUP)]
        for ci in range(n_chunks):
            r0 = ci * c
            for hs in groups:
                q = [load(q_ref, r0, h * GLA_DK, GLA_DK) for h in hs]
                k = [load(k_ref, r0, h * GLA_DK, GLA_DK) for h in hs]
                v = [load(v_ref, r0, h * GLA_DV, GLA_DV).astype(BF16) for h in hs]
                la = [load(la_ref, r0, h * GLA_DK, GLA_DK) for h in hs]
                la_hi = [x.astype(BF16) for x in la]
                la_lo = [(x - y.astype(F32)).astype(BF16) for x, y in zip(la, la_hi)]
                b = [_dot(tril, x) + _dot(tril, y) for x, y in zip(la_hi, la_lo)]
                qs = [x * scale for x in q]
                qd = [(x * jnp.exp(y)).astype(BF16) for x, y in zip(qs, b)]
                if factored:
                    kn = [(x * jnp.exp(-y)).astype(BF16) for x, y in zip(k, b)]
                    a = [jnp.where(causal, _dot_nt(x, y), 0.0) for x, y in zip(qd, kn)]
                else:
                    a = [pairwise_scores(x, y, z) for x, y, z in zip(qs, k, b)]
                s_old = [s_ref[h] for h in hs]
                o = [_dot(x, s.astype(BF16)) + _dot(y.astype(BF16), z) for x, s, y, z in zip(qd, s_old, a, v)]
                for h, x in zip(hs, o):
                    if t_valid >= c:
                        o_ref[0, r0:r0 + c, h * GLA_DV:(h + 1) * GLA_DV] = x
                    else:
                        o_ref[0, :, h * GLA_DV:(h + 1) * GLA_DV] = x[:t_valid]
                bl = [y[c - 1:c, :] for y in b]
                kd_t = [(x * jnp.exp(l - y)).T.astype(BF16) for x, l, y in zip(k, bl, b)]
                dec_t = [jnp.broadcast_to(jnp.exp(l), (c, GLA_DK)).T for l in bl]
                for h, d, s, x, z in zip(hs, dec_t, s_old, kd_t, v):
                    s_ref[h] = jnp.concatenate([d] * (GLA_DV // c), axis=1) * s + _dot(x, z)

    worst = jnp.float32(0.0)
    for ci in range(n_chunks):
        rows = slice(ci * c, (ci + 1) * c) if t_valid >= c else slice(None)
        worst = jnp.maximum(worst, jnp.max(-jnp.sum(la_ref[0, rows, :], axis=0, keepdims=True)))
    safe = worst < GLA_SAFE_DECAY
    pl.when(safe)(lambda: run(True))
    pl.when(jnp.logical_not(safe))(lambda: run(False))

    @pl.when(j == pl.num_programs(1) - 1)
    def _():
        so_ref[0] = s_ref[...]


def _gla(gq, gk, gv, la, s0, tb):
    bsz, t, _ = gq.shape
    t_valid = min(tb, GLA_CHUNK)
    n_chunks = max(tb // GLA_CHUNK, 1)
    has_s0 = s0 is not None
    tok = lambda w: pl.BlockSpec((1, tb, w), lambda b, j: (b, j, 0))
    st = pl.BlockSpec((1, GLA_HEADS, GLA_DK, GLA_DV), lambda b, j: (b, 0, 0, 0))
    in_specs = [tok(GLA_QK), tok(GLA_QK), tok(GLA_V), tok(GLA_QK)] + ([st] if has_s0 else [])
    args = (gq, gk, gv, la) + ((s0,) if has_s0 else ())
    return pl.pallas_call(
        functools.partial(_gla_kernel, n_chunks=n_chunks, t_valid=t_valid, has_s0=has_s0),
        grid=(bsz, t // tb),
        in_specs=in_specs,
        out_specs=[tok(GLA_V), st],
        out_shape=[jax.ShapeDtypeStruct((bsz, t, GLA_V), F32),
                   jax.ShapeDtypeStruct((bsz, GLA_HEADS, GLA_DK, GLA_DV), F32)],
        scratch_shapes=[pltpu.VMEM((GLA_HEADS, GLA_DK, GLA_DV), F32)],
        compiler_params=pltpu.CompilerParams(
            dimension_semantics=("parallel", "arbitrary"), vmem_limit_bytes=VMEM_LIMIT),
        name="gla",
    )(*args)


def _swa_kernel(sink_ref, q_ref, kc_ref, vc_ref, kp_ref, vp_ref, o_ref, *, tq, nb, prev_is_block):
    i = pl.program_id(1)
    w = WINDOW
    nk = 2 * w
    pairs = SWA_GROUP // 2
    t_idx = lax.broadcasted_iota(jnp.int32, (tq, nk), 0)
    j_idx = lax.broadcasted_iota(jnp.int32, (tq, nk), 1)
    d = j_idx - t_idx
    band = (d >= 1) & (d <= w)
    lane = lax.broadcasted_iota(jnp.int32, (nk, LANES), 1)
    left = lane < SWA_HEAD_DIM

    def window(prev_ref, cur_ref, blk):
        cur = cur_ref[0, blk * tq:(blk + 1) * tq, :]
        if tq < w:
            cur = jnp.concatenate([cur, jnp.zeros((w - tq, SWA_KV), F32)], axis=0)
        prev = prev_ref[0] if blk == 0 else cur_ref[0, (blk - 1) * tq:blk * tq, :]
        full = jnp.concatenate([prev, cur], axis=0)
        return full, pltpu.roll(full, SWA_HEAD_DIM, 1)

    def block_diag(full, rot, kvh):
        a, b = (full, rot) if kvh == 0 else (rot, full)
        return jnp.concatenate([jnp.where(left, a, 0.0), jnp.where(left, 0.0, b)], axis=0).astype(BF16)

    def mask(blk):
        vis = band
        if prev_is_block and blk == 0:
            vis = vis & ((j_idx >= w) | (i > 0))
        return jnp.concatenate([vis] * pairs, axis=0)

    units = [(blk, kvh) for blk in range(nb) for kvh in range(SWA_KV_HEADS)]
    ug = SWA_KV_HEADS if tq < w else 1
    win, vis = {}, {}
    for g0 in range(0, len(units), ug):
        grp = units[g0:g0 + ug]
        for blk, _ in grp:
            if blk not in win:
                win[blk] = (window(kp_ref, kc_ref, blk), window(vp_ref, vc_ref, blk))
                vis[blk] = mask(blk)
        kbd = [block_diag(*win[blk][0], kvh) for blk, kvh in grp]
        vbd = [block_diag(*win[blk][1], kvh) for blk, kvh in grp]
        base = [kvh * SWA_GROUP * SWA_HEAD_DIM for _, kvh in grp]
        rows = [slice(blk * tq, (blk + 1) * tq) for blk, _ in grp]
        qs = [jnp.concatenate([q_ref[0, r, b0 + p * LANES: b0 + (p + 1) * LANES] for p in range(pairs)], axis=0)
              for r, b0 in zip(rows, base)]
        s = [_dot_nt(x.astype(BF16), y) * (SWA_HEAD_DIM ** -0.5) for x, y in zip(qs, kbd)]
        probs = [[], []]
        for e in range(2):
            se = [jnp.where(vis[blk], x[:, e * nk:(e + 1) * nk], NEG_BIG) for (blk, _), x in zip(grp, s)]
            sink = [jnp.concatenate(
                [jnp.full((tq, 1), sink_ref[kvh * SWA_GROUP + 2 * p + e], F32) for p in range(pairs)], axis=0)
                for _, kvh in grp]
            m = [jnp.maximum(jnp.max(x, axis=-1, keepdims=True), y) for x, y in zip(se, sink)]
            ex = [jnp.exp(x - y) for x, y in zip(se, m)]
            den = [jnp.sum(x, axis=-1, keepdims=True) + jnp.exp(y - z) for x, y, z in zip(ex, sink, m)]
            probs[e] = [x / y for x, y in zip(ex, den)]
        o = [_dot(jnp.concatenate([p0, p1], axis=1).astype(BF16), y)
             for p0, p1, y in zip(probs[0], probs[1], vbd)]
        for r, b0, x in zip(rows, base, o):
            for p in range(pairs):
                o_ref[0, r, b0 + p * LANES: b0 + (p + 1) * LANES] = x[p * tq:(p + 1) * tq]


def _swa(sq, sk, sv, sinks, prev_k=None, prev_v=None, nb=1):
    bsz, t, _ = sq.shape
    prev_is_block = prev_k is None
    if prev_is_block:
        tq = WINDOW
        prev_k, prev_v = sk, sv
        prev_spec = pl.BlockSpec((1, WINDOW, SWA_KV), lambda b, i: (b, jnp.maximum(i * nb - 1, 0), 0))
    else:
        tq = t
        assert nb == 1
        prev_spec = pl.BlockSpec((1, WINDOW, SWA_KV), lambda b, i: (b, 0, 0))
    cur_spec = pl.BlockSpec((1, nb * tq, SWA_KV), lambda b, i: (b, i, 0))
    q_spec = pl.BlockSpec((1, nb * tq, SWA_Q), lambda b, i: (b, i, 0))
    return pl.pallas_call(
        functools.partial(_swa_kernel, tq=tq, nb=nb, prev_is_block=prev_is_block),
        grid=(bsz, t // (nb * tq)),
        in_specs=[pl.BlockSpec(memory_space=pltpu.SMEM), q_spec, cur_spec, cur_spec, prev_spec, prev_spec],
        out_specs=q_spec,
        out_shape=jax.ShapeDtypeStruct((bsz, t, SWA_Q), F32),
        compiler_params=pltpu.CompilerParams(
            dimension_semantics=("parallel", "arbitrary"), vmem_limit_bytes=VMEM_LIMIT),
        name="swa",
    )(sinks, sq, sk, sv, prev_k, prev_v)


def _merge_kernel(x_ref, oa_ref, gg_ref, ob_ref, ga_ref, gb_ref, gn_ref, wa_ref, wb_ref, wo_ref, h_ref):
    parts = []
    for hd in range(GLA_HEADS):
        sl = slice(hd * GLA_DV, (hd + 1) * GLA_DV)
        g = gg_ref[:, sl]
        parts.append(_rms(oa_ref[:, sl], gn_ref[...]) * (g * _sigmoid(g)))
    oa = jnp.concatenate(parts, axis=1).astype(BF16)
    br_a = _dot(oa, wa_ref[...])
    br_b = _dot(ob_ref[...].astype(BF16), wb_ref[...])
    merged = _sigmoid(ga_ref[...]) * br_a + _sigmoid(gb_ref[...]) * br_b
    h_ref[...] = x_ref[...] + _dot(merged.astype(BF16), wo_ref[...])


def _merge(x2d, oa, gg, ob, ga, gb, gn, wa, wb, wo, tn):
    n = x2d.shape[0]
    tok = pl.BlockSpec((tn, D_MODEL), lambda i: (i, 0))
    const = lambda i: (0, 0)
    wspec = pl.BlockSpec((D_MODEL, D_MODEL), const)
    return pl.pallas_call(
        _merge_kernel,
        grid=(n // tn,),
        in_specs=[tok] * 6 + [pl.BlockSpec((1, GLA_DV), const), wspec, wspec, wspec],
        out_specs=tok,
        out_shape=jax.ShapeDtypeStruct((n, D_MODEL), F32),
        compiler_params=pltpu.CompilerParams(
            dimension_semantics=("parallel",), vmem_limit_bytes=VMEM_LIMIT),
        name="merge",
    )(x2d, oa, gg, ob, ga, gb, gn, wa, wb, wo)


def _sort16_pairs():
    pairs = []
    n = 16
    p = 1
    while p < n:
        k = p
        while k >= 1:
            for j in range(k % p, n - k, 2 * k):
                for i in range(min(k, n - j - k)):
                    if (i + j) // (2 * p) == (i + j + k) // (2 * p):
                        pairs.append((i + j, i + j + k))
            k //= 2
        p *= 2
    return pairs


_SORT16 = _sort16_pairs()


def _bitonic_top16(xa, xb):
    z = [jnp.maximum(xa[i], xb[15 - i]) for i in range(16)]
    dist = 8
    while dist >= 1:
        for i in range(16):
            if i & dist == 0:
                hi = jnp.maximum(z[i], z[i + dist])
                lo = jnp.minimum(z[i], z[i + dist])
                z[i], z[i + dist] = hi, lo
        dist //= 2
    return z


def _top16_rows(s):
    x = [s[SUBLANES * v:SUBLANES * (v + 1), :] for v in range(16)]
    for i, j in _SORT16:
        hi = jnp.maximum(x[i], x[j])
        lo = jnp.minimum(x[i], x[j])
        x[i], x[j] = hi, lo
    shift = 1
    while shift < SUBLANES:
        x = _bitonic_top16(x, [pltpu.roll(t, shift, 0) for t in x])
        shift *= 2
    return [t[SUBLANES - 1:SUBLANES, :] for t in x]


_CAND_Q = [PEER_TOPK // (p + 1) for p in range(PEER_TOPK)]


PACK = 2 * SUBLANES
_HS_TAU, _HS_ZINV, _HS_TOP1, _HS_ROWS = PEER_TOPK, PEER_TOPK + 1, PEER_TOPK + 2, PEER_TOPK + SUBLANES


def _peer_kernel(h_ref, nf_ref, nfin_ref, wq_ref, k1_ref, k2_ref, u_ref, vt_ref, y_ref,
                 hn_ref, sa_ref, sb_ref, e1_ref, e2_ref, v1_ref, v2_ref, hs_ref,
                 ca_ref, ea_ref, ht_ref, g_ref, acc_ref, *, tn, ec):
    e = pl.program_id(1)
    na = ec // PEER_NKEYS
    nt = tn // LANES
    rb = 4 * SUBLANES

    @pl.when(e == 0)
    def _routing():
        hn_ref[...] = _rms(h_ref[...], nf_ref[...]).astype(BF16)
        acc_ref[...] = jnp.zeros_like(acc_ref)

        for hd in range(PEER_HEADS):
            wq = wq_ref[hd * PEER_QDIM:(hd + 1) * PEER_QDIM, :]
            qt = _dot_nt(wq, hn_ref[...])
            s1 = _dot(k1_ref[hd], qt[:PEER_HALF].astype(BF16))
            s2 = _dot(k2_ref[hd], qt[PEER_HALF:].astype(BF16))
            sa_ref[hd] = s1
            sb_ref[hd] = s2
            for p, r in enumerate(_top16_rows(s1)):
                v1_ref[p, hd:hd + 1, :] = r
            for p, r in enumerate(_top16_rows(s2)):
                v2_ref[p, hd:hd + 1, :] = r
                hs_ref[hd, p:p + 1, :] = r

        v1 = [v1_ref[p] for p in range(PEER_TOPK)]
        v2 = [v2_ref[q] for q in range(PEER_TOPK)]
        ninf = jnp.full((PEER_HEADS, tn), -jnp.inf, F32)
        best = [v1[0] + v2[q] for q in range(PEER_TOPK)]
        for p in range(1, PEER_TOPK):
            row = [v1[p] + v2[q] if q < _CAND_Q[p] else ninf for q in range(PEER_TOPK)]
            best = _bitonic_top16(best, row)
        tau = best[PEER_TOPK - 1]
        top = best[0]
        zsum = jnp.zeros((PEER_HEADS, tn), F32)
        for p in range(PEER_TOPK):
            for q in range(_CAND_Q[p]):
                cnd = v1[p] + v2[q]
                zsum = zsum + jnp.where(cnd >= tau, jnp.exp(cnd - top), 0.0)
        zinv = 1.0 / zsum
        for hd in range(PEER_HEADS):
            hs_ref[hd, _HS_TAU:_HS_TAU + 1, :] = tau[hd:hd + 1]
            hs_ref[hd, _HS_ZINV:_HS_ZINV + 1, :] = zinv[hd:hd + 1]
            hs_ref[hd, _HS_TOP1:_HS_TOP1 + 1, :] = v1[0][hd:hd + 1]

        def staircase(idx, carry):
            hd = idx // (PEER_NKEYS // rb)
            rows = pl.ds(pl.multiple_of((idx % (PEER_NKEYS // rb)) * rb, rb), rb)
            s1 = sa_ref[hd, rows, :]
            tau_h = hs_ref[hd, _HS_TAU:_HS_TAU + 1, :]
            theta = jnp.full((rb, tn), jnp.inf, F32)
            for q in range(PEER_TOPK):
                v2q = hs_ref[hd, q:q + 1, :]
                theta = jnp.where(s1 + v2q >= tau_h, v2q, theta)
            e2_ref[hd, rows, :] = (jnp.exp(sb_ref[hd, rows, :] - hs_ref[hd, 0:1, :])
                                   * hs_ref[hd, _HS_ZINV:_HS_ZINV + 1, :])
            e1_ref[hd, rows, :] = jnp.exp(s1 - hs_ref[hd, _HS_TOP1:_HS_TOP1 + 1, :])
            sa_ref[hd, rows, :] = theta
            return carry

        lax.fori_loop(0, PEER_HEADS * (PEER_NKEYS // rb), staircase, 0)

    ht_ref[...] = _dot_nt(u_ref[...], hn_ref[...])
    a_rows = pl.ds(pl.multiple_of(e * na, na), na)
    for hd in range(PEER_HEADS):
        ca_ref[hd] = sa_ref[hd, a_rows, :]
        ea_ref[hd] = e1_ref[hd, a_rows, :]

    ga = 4
    gb = 4
    n_bg = PEER_NKEYS // (gb * SUBLANES)

    def gate(idx, carry):
        cols = pl.ds(pl.multiple_of((idx // n_bg) * LANES, LANES), LANES)
        b0 = pl.multiple_of((idx % n_bg) * (gb * SUBLANES), gb * SUBLANES)
        bc = lambda ref, hd, r: jnp.broadcast_to(ref[hd, r:r + 1, cols], (SUBLANES, LANES))
        for a_blk in range(0, na, ga):
            w = [[jnp.zeros((SUBLANES, LANES), F32) for _ in range(gb)] for _ in range(ga)]
            for hd in range(PEER_HEADS):
                s2 = [sb_ref[hd, pl.ds(b0 + t * SUBLANES, SUBLANES), cols] for t in range(gb)]
                e2 = [e2_ref[hd, pl.ds(b0 + t * SUBLANES, SUBLANES), cols] for t in range(gb)]
                for i in range(ga):
                    theta_b = bc(ca_ref, hd, a_blk + i)
                    e1_b = bc(ea_ref, hd, a_blk + i)
                    for t in range(gb):
                        w[i][t] = w[i][t] + jnp.where(s2[t] >= theta_b, e2[t], 0.0) * e1_b
            for i in range(ga):
                for t in range(0, gb, 2):
                    rows = pl.ds(pl.multiple_of((a_blk + i) * PEER_NKEYS + b0 + t * SUBLANES, PACK), PACK)
                    hx = ht_ref[rows, cols]
                    act = 0.5 * hx * (1.0 + lax.erf(hx * (2.0 ** -0.5)))
                    g_ref[rows, cols] = (jnp.concatenate([w[i][t], w[i][t + 1]], axis=0) * act).astype(BF16)
        return carry

    lax.fori_loop(0, nt * n_bg, gate, 0)
    acc_ref[...] += _dot(vt_ref[...], g_ref[...])

    @pl.when(e == pl.num_programs(1) - 1)
    def _finish():
        y_ref[...] = _rms(h_ref[...] + acc_ref[...].T, nfin_ref[...])


def _peer(h2d, nf, nfin, wq_t, k1, k2, u, vt, tn, ec):
    n = h2d.shape[0]
    assert ec % (SUBLANES * PEER_NKEYS) == 0 and tn % LANES == 0
    tok = pl.BlockSpec((tn, D_MODEL), lambda i, e: (i, 0))
    const2 = lambda i, e: (0, 0)
    const3 = lambda i, e: (0, 0, 0)
    vec = pl.BlockSpec((1, D_MODEL), const2)
    kspec = pl.BlockSpec((PEER_HEADS, PEER_NKEYS, PEER_HALF), const3)
    head_f32 = pltpu.VMEM((PEER_HEADS, PEER_NKEYS, tn), F32)
    rank_tok = pltpu.VMEM((PEER_TOPK, PEER_HEADS, tn), F32)
    stage = pltpu.VMEM((PEER_HEADS, ec // PEER_NKEYS, tn), F32)
    return pl.pallas_call(
        functools.partial(_peer_kernel, tn=tn, ec=ec),
        grid=(n // tn, PEER_N_EXPERTS // ec),
        in_specs=[tok, vec, vec,
                  pl.BlockSpec((PEER_HEADS * PEER_QDIM, D_MODEL), const2, pipeline_mode=pl.Buffered(1)),
                  kspec, kspec,
                  pl.BlockSpec((ec, D_MODEL), lambda i, e: (e, 0)),
                  pl.BlockSpec((D_MODEL, ec), lambda i, e: (0, e))],
        out_specs=tok,
        out_shape=jax.ShapeDtypeStruct((n, D_MODEL), F32),
        scratch_shapes=[pltpu.VMEM((tn, D_MODEL), BF16),
                        head_f32, head_f32, head_f32, head_f32, rank_tok, rank_tok,
                        pltpu.VMEM((PEER_HEADS, _HS_ROWS, tn), F32),
                        stage, stage, pltpu.VMEM((ec, tn), F32), pltpu.VMEM((ec, tn), BF16),
                        pltpu.VMEM((D_MODEL, tn), F32)],
        compiler_params=pltpu.CompilerParams(
            dimension_semantics=("parallel", "arbitrary"), vmem_limit_bytes=VMEM_LIMIT),
        name="peer",
    )(h2d, nf, nfin, wq_t, k1, k2, u, vt)


def _pack_params(norm_mix, w_in, gla_w_decay, gla_b_decay, gla_norm, swa_sinks, w_branch_a, w_branch_b,
                 w_out, norm_ffn, peer_w_q, peer_keys1, peer_keys2, peer_u, peer_v):
    off = np.cumsum((0, GLA_QK, GLA_QK, GLA_V, GLA_V, GLA_RANK, SWA_Q, SWA_KV, SWA_KV, D_MODEL, D_MODEL))
    col = lambda i: w_in[:, off[i]:off[i + 1]]
    lr = jnp.pad(col(4), ((0, 0), (0, LANES - GLA_RANK)))
    w_all = jnp.concatenate([col(0), col(1), col(2), col(3), col(5), col(6), col(7), col(8), col(9), lr],
                            axis=1).astype(BF16)
    wdec = jnp.pad(gla_w_decay, ((0, LANES - GLA_RANK), (0, 0))).astype(BF16)
    return dict(
        nm=norm_mix[None], w_all=w_all, wdec=wdec, bdec=gla_b_decay[None], gn=gla_norm[None],
        sinks=swa_sinks, wa=w_branch_a.astype(BF16), wb=w_branch_b.astype(BF16), wo=w_out.astype(BF16),
        nf=norm_ffn[None], wq_t=peer_w_q.T.astype(BF16), k1=peer_keys1.astype(BF16),
        k2=peer_keys2.astype(BF16), u=peer_u.astype(BF16), vt=peer_v.T.astype(BF16))


def _layer(x, p, nfin, s0, win_k, win_v, *, tn, gla_tb, swa_nb, peer_tn, peer_ec):
    bsz, t, _ = x.shape
    x2d = x.reshape(bsz * t, D_MODEL)
    gq, gk, gv, gg, la, sq, sk, sv, ga, gb = _inproj(x2d, p["nm"], p["w_all"], p["wdec"], p["bdec"], tn)
    r3 = lambda a: a.reshape(bsz, t, a.shape[-1])
    o_a, s_new = _gla(r3(gq), r3(gk), r3(gv), r3(la), s0, gla_tb)
    if win_k is None:
        o_b = _swa(r3(sq), r3(sk), r3(sv), p["sinks"], nb=swa_nb)
        ln = min(WINDOW, t)
        new_k, new_v = r3(sk)[:, t - ln:], r3(sv)[:, t - ln:]
    else:
        wk = win_k.reshape(bsz, WINDOW, SWA_KV)
        wv = win_v.reshape(bsz, WINDOW, SWA_KV)
        o_b = _swa(r3(sq), r3(sk), r3(sv), p["sinks"], wk, wv)
        new_k = jnp.concatenate([wk, r3(sk)], axis=1)[:, -WINDOW:]
        new_v = jnp.concatenate([wv, r3(sv)], axis=1)[:, -WINDOW:]
    h = _merge(x2d, o_a.reshape(bsz * t, GLA_V), gg, o_b.reshape(bsz * t, SWA_Q), ga, gb,
               p["gn"], p["wa"], p["wb"], p["wo"], tn)
    y = _peer(h, p["nf"], nfin, p["wq_t"], p["k1"], p["k2"], p["u"], p["vt"], peer_tn, peer_ec)
    kv_shape = (bsz, -1, SWA_KV_HEADS, SWA_HEAD_DIM)
    return y.reshape(bsz, t, D_MODEL), new_k.reshape(kv_shape), new_v.reshape(kv_shape), s_new


def kernel(x_prompt, x_sample, cache_win_k, cache_win_v, state_gla, norm_mix, w_in, gla_w_decay, gla_b_decay,
           gla_norm, swa_sinks, w_branch_a, w_branch_b, w_out, norm_ffn, peer_w_q, peer_keys1, peer_keys2,
           peer_u, peer_v, norm_final):
    depth = w_in.shape[0]
    assert depth == 1, "the final norm is fused into the last layer's channel mixer; one layer supported"
    p = _pack_params(norm_mix[0], w_in[0], gla_w_decay[0], gla_b_decay[0], gla_norm[0], swa_sinks[0],
                     w_branch_a[0], w_branch_b[0], w_out[0], norm_ffn[0], peer_w_q[0], peer_keys1[0],
                     peer_keys2[0], peer_u[0], peer_v[0])
    nfin = norm_final[None]
    yp, pk, pv, ps = _layer(x_prompt, p, nfin, None, None, None,
                            tn=512, gla_tb=512, swa_nb=4, peer_tn=512, peer_ec=2048)
    ys, sk, sv, ss = _layer(x_sample, p, nfin, state_gla[0], cache_win_k[0], cache_win_v[0],
                            tn=512, gla_tb=x_sample.shape[1], swa_nb=1, peer_tn=512, peer_ec=2048)
    return (yp, ys, pk[None], pv[None], ps[None], sk[None], sv[None], ss[None])
```

```python
import functools

import jax
import jax.numpy as jnp
import numpy as np
from jax import lax
from jax.experimental import pallas as pl
from jax.experimental.pallas import tpu as pltpu

F32 = jnp.float32
BF16 = jnp.bfloat16

D_MODEL = 1024
GLA_HEADS = 4
GLA_DK = 128
GLA_DV = 256
GLA_RANK = 16
GLA_TAU = 16.0
SWA_HEADS = 16
SWA_KV_HEADS = 2
SWA_GROUP = SWA_HEADS // SWA_KV_HEADS
SWA_HEAD_DIM = 64
WINDOW = 128
PEER_HEADS = 8
PEER_NKEYS = 128
PEER_N_EXPERTS = PEER_NKEYS * PEER_NKEYS
PEER_QDIM = 256
PEER_HALF = PEER_QDIM // 2
PEER_TOPK = 16
EPS = 1e-6

GLA_QK = GLA_HEADS * GLA_DK
GLA_V = GLA_HEADS * GLA_DV
SWA_Q = SWA_HEADS * SWA_HEAD_DIM
SWA_KV = SWA_KV_HEADS * SWA_HEAD_DIM

LANES = 128
SUBLANES = 8
GLA_CHUNK = 128
GLA_HEAD_GROUP = 4
GLA_SAFE_DECAY = 60.0
VMEM_LIMIT = 56 * 1024 * 1024
NEG_BIG = -1e30

_NT = (((1,), (1,)), ((), ()))


def _rms(x, g):
    return x * lax.rsqrt(jnp.mean(x * x, axis=-1, keepdims=True) + EPS) * g


def _dot(a, b):
    return jnp.dot(a, b, preferred_element_type=F32)


def _dot_nt(a, b):
    return lax.dot_general(a, b, _NT, preferred_element_type=F32)


def _sigmoid(x):
    return 1.0 / (1.0 + jnp.exp(-x))


_P_GQ, _P_GK, _P_GV, _P_GG, _P_SQ, _P_SK, _P_SV, _P_GA, _P_GB, _P_LR, _P_END = (
    0, 512, 1024, 2048, 3072, 4096, 4224, 4352, 5376, 6400, 6528)


def _inproj_kernel(x_ref, nm_ref, w_ref, wdec_ref, bdec_ref,
                   gq_ref, gk_ref, gv_ref, gg_ref, la_ref, sq_ref, sk_ref, sv_ref, ga_ref, gb_ref):
    xb = _rms(x_ref[...], nm_ref[...]).astype(BF16)

    def proj(lo, hi):
        return _dot(xb, w_ref[:, lo:hi])

    gq_ref[...] = proj(_P_GQ, _P_GK)
    gk_ref[...] = proj(_P_GK, _P_GV)
    gv_ref[...] = proj(_P_GV, _P_GG)
    gg_ref[...] = proj(_P_GG, _P_SQ)
    sq_ref[...] = proj(_P_SQ, _P_SK)
    sk_ref[...] = proj(_P_SK, _P_SV)
    sv_ref[...] = proj(_P_SV, _P_GA)
    ga_ref[...] = proj(_P_GA, _P_GB)
    gb_ref[...] = proj(_P_GB, _P_LR)
    glr = proj(_P_LR, _P_END)
    z = _dot(glr.astype(BF16), wdec_ref[...]) + bdec_ref[...]
    la_ref[...] = (jnp.minimum(z, 0.0) - jnp.log(1.0 + jnp.exp(-jnp.abs(z)))) * (1.0 / GLA_TAU)


def _inproj(x2d, nm, w_all, wdec, bdec, tn):
    n = x2d.shape[0]
    widths = (GLA_QK, GLA_QK, GLA_V, GLA_V, GLA_QK, SWA_Q, SWA_KV, SWA_KV, D_MODEL, D_MODEL)
    const = lambda i: (0, 0)
    return pl.pallas_call(
        _inproj_kernel,
        grid=(n // tn,),
        in_specs=[
            pl.BlockSpec((tn, D_MODEL), lambda i: (i, 0)),
            pl.BlockSpec((1, D_MODEL), const),
            pl.BlockSpec((D_MODEL, _P_END), const, pipeline_mode=pl.Buffered(1)),
            pl.BlockSpec((LANES, GLA_QK), const),
            pl.BlockSpec((1, GLA_QK), const),
        ],
        out_specs=[pl.BlockSpec((tn, w), lambda i: (i, 0)) for w in widths],
        out_shape=[jax.ShapeDtypeStruct((n, w), F32) for w in widths],
        compiler_params=pltpu.CompilerParams(
            dimension_semantics=("parallel",), vmem_limit_bytes=VMEM_LIMIT),
        name="in_proj",
    )(x2d, nm, w_all, wdec, bdec)


def _gla_kernel(*refs, bb, n_chunks, t_valid, has_s0):
    if has_s0:
        q_ref, k_ref, v_ref, la_ref, s0_ref, o_ref, so_ref, s_ref = refs
    else:
        q_ref, k_ref, v_ref, la_ref, o_ref, so_ref, s_ref = refs
    j = pl.program_id(1)
    c = GLA_CHUNK

    @pl.when(j == 0)
    def _():
        if has_s0:
            s_ref[...] = s0_ref[...]
        else:
            s_ref[...] = jnp.zeros_like(s_ref)

    row = lax.broadcasted_iota(jnp.int32, (c, c), 0)
    col = lax.broadcasted_iota(jnp.int32, (c, c), 1)
    causal = row >= col
    tril = causal.astype(BF16)
    scale = GLA_DK ** -0.5

    def load(ref, bi, r0, lo, w):
        if t_valid >= c:
            return ref[bi, r0:r0 + c, lo:lo + w]
        t = ref[bi, :, lo:lo + w]
        return jnp.concatenate([t, jnp.zeros((c - t_valid, w), F32)], axis=0)

    def pairwise_scores(qs, k, b):
        def score_row(i, a_t):
            pick = row == i
            q_i = jnp.sum(jnp.where(pick, qs, 0.0), axis=0, keepdims=True)
            b_i = jnp.sum(jnp.where(pick, b, 0.0), axis=0, keepdims=True)
            rel = jnp.where(row <= i, b_i - b, -jnp.inf)
            s_col = jnp.sum(k * jnp.exp(rel) * q_i, axis=1, keepdims=True)
            return jnp.where(col == i, s_col, a_t)
        return lax.fori_loop(0, c, score_row, jnp.zeros((c, c), F32)).T

    def run(factored):
        pairs = [(bi, h) for bi in range(bb) for h in range(GLA_HEADS)]
        groups = [pairs[g0:g0 + GLA_HEAD_GROUP] for g0 in range(0, len(pairs), GLA_HEAD_GROUP)]
        for ci in range(n_chunks):
            r0 = ci * c
            for hs in groups:
                q = [load(q_ref, bi, r0, h * GLA_DK, GLA_DK) for bi, h in hs]
                k = [load(k_ref, bi, r0, h * GLA_DK, GLA_DK) for bi, h in hs]
                v = [load(v_ref, bi, r0, h * GLA_DV, GLA_DV).astype(BF16) for bi, h in hs]
                la = [load(la_ref, bi, r0, h * GLA_DK, GLA_DK) for bi, h in hs]
                la_hi = [x.astype(BF16) for x in la]
                la_lo = [(x - y.astype(F32)).astype(BF16) for x, y in zip(la, la_hi)]
                b = [_dot(tril, x) + _dot(tril, y) for x, y in zip(la_hi, la_lo)]
                qs = [x * scale for x in q]
                qd = [(x * jnp.exp(y)).astype(BF16) for x, y in zip(qs, b)]
                if factored:
                    kn = [(x * jnp.exp(-y)).astype(BF16) for x, y in zip(k, b)]
                    a = [jnp.where(causal, _dot_nt(x, y), 0.0) for x, y in zip(qd, kn)]
                else:
                    a = [pairwise_scores(x, y, z) for x, y, z in zip(qs, k, b)]
                s_old = [s_ref[bi, h] for bi, h in hs]
                o = [_dot(x, s.astype(BF16)) + _dot(y.astype(BF16), z) for x, s, y, z in zip(qd, s_old, a, v)]
                for (bi, h), x in zip(hs, o):
                    if t_valid >= c:
                        o_ref[bi, r0:r0 + c, h * GLA_DV:(h + 1) * GLA_DV] = x
                    else:
                        o_ref[bi, :, h * GLA_DV:(h + 1) * GLA_DV] = x[:t_valid]
                bl = [y[c - 1:c, :] for y in b]
                kd_t = [(x * jnp.exp(l - y)).T.astype(BF16) for x, l, y in zip(k, bl, b)]
                dec_t = [jnp.broadcast_to(jnp.exp(l), (c, GLA_DK)).T for l in bl]
                for (bi, h), d, s, x, z in zip(hs, dec_t, s_old, kd_t, v):
                    s_ref[bi, h] = jnp.concatenate([d] * (GLA_DV // c), axis=1) * s + _dot(x, z)

    worst = jnp.float32(0.0)
    for bi in range(bb):
        for ci in range(n_chunks):
            rows = slice(ci * c, (ci + 1) * c) if t_valid >= c else slice(None)
            worst = jnp.maximum(worst, jnp.max(-jnp.sum(la_ref[bi, rows, :], axis=0, keepdims=True)))
    safe = worst < GLA_SAFE_DECAY
    pl.when(safe)(lambda: run(True))
    pl.when(jnp.logical_not(safe))(lambda: run(False))

    @pl.when(j == pl.num_programs(1) - 1)
    def _():
        so_ref[...] = s_ref[...]


def _gla(gq, gk, gv, la, s0, tb, bb):
    bsz, t, _ = gq.shape
    t_valid = min(tb, GLA_CHUNK)
    n_chunks = max(tb // GLA_CHUNK, 1)
    has_s0 = s0 is not None
    tok = lambda w: pl.BlockSpec((bb, tb, w), lambda b, j: (b, j, 0))
    st = pl.BlockSpec((bb, GLA_HEADS, GLA_DK, GLA_DV), lambda b, j: (b, 0, 0, 0))
    in_specs = [tok(GLA_QK), tok(GLA_QK), tok(GLA_V), tok(GLA_QK)] + ([st] if has_s0 else [])
    args = (gq, gk, gv, la) + ((s0,) if has_s0 else ())
    return pl.pallas_call(
        functools.partial(_gla_kernel, bb=bb, n_chunks=n_chunks, t_valid=t_valid, has_s0=has_s0),
        grid=(bsz // bb, t // tb),
        in_specs=in_specs,
        out_specs=[tok(GLA_V), st],
        out_shape=[jax.ShapeDtypeStruct((bsz, t, GLA_V), F32),
                   jax.ShapeDtypeStruct((bsz, GLA_HEADS, GLA_DK, GLA_DV), F32)],
        scratch_shapes=[pltpu.VMEM((bb, GLA_HEADS, GLA_DK, GLA_DV), F32)],
        compiler_params=pltpu.CompilerParams(
            dimension_semantics=("parallel", "arbitrary"), vmem_limit_bytes=VMEM_LIMIT),
        name="gla",
    )(*args)


def _swa_kernel(sink_ref, q_ref, kc_ref, vc_ref, kp_ref, vp_ref, o_ref, *, bb, tq, nb, prev_is_block):
    i = pl.program_id(1)
    w = WINDOW
    nk = 2 * w
    pairs = SWA_GROUP // 2
    t_idx = lax.broadcasted_iota(jnp.int32, (tq, nk), 0)
    j_idx = lax.broadcasted_iota(jnp.int32, (tq, nk), 1)
    d = j_idx - t_idx
    band = (d >= 1) & (d <= w)
    lane = lax.broadcasted_iota(jnp.int32, (nk, LANES), 1)
    left = lane < SWA_HEAD_DIM

    def window(prev_ref, cur_ref, bi, blk):
        cur = cur_ref[bi, blk * tq:(blk + 1) * tq, :]
        if tq < w:
            cur = jnp.concatenate([cur, jnp.zeros((w - tq, SWA_KV), F32)], axis=0)
        prev = prev_ref[bi] if blk == 0 else cur_ref[bi, (blk - 1) * tq:blk * tq, :]
        full = jnp.concatenate([prev, cur], axis=0)
        return full, pltpu.roll(full, SWA_HEAD_DIM, 1)

    def block_diag(full, rot, kvh):
        a, b = (full, rot) if kvh == 0 else (rot, full)
        return jnp.concatenate([jnp.where(left, a, 0.0), jnp.where(left, 0.0, b)], axis=0).astype(BF16)

    def mask(blk):
        vis = band
        if prev_is_block and blk == 0:
            vis = vis & ((j_idx >= w) | (i > 0))
        return jnp.concatenate([vis] * pairs, axis=0)

    units = [((bi, blk), kvh) for bi in range(bb) for blk in range(nb) for kvh in range(SWA_KV_HEADS)]
    ug = SWA_KV_HEADS if tq < w else 1
    win, vis = {}, {}
    for g0 in range(0, len(units), ug):
        grp = units[g0:g0 + ug]
        for blk, _ in grp:
            if blk not in win:
                win[blk] = (window(kp_ref, kc_ref, *blk), window(vp_ref, vc_ref, *blk))
                vis[blk] = mask(blk[1])
        kbd = [block_diag(*win[blk][0], kvh) for blk, kvh in grp]
        vbd = [block_diag(*win[blk][1], kvh) for blk, kvh in grp]
        base = [kvh * SWA_GROUP * SWA_HEAD_DIM for _, kvh in grp]
        rows = [(blk[0], slice(blk[1] * tq, (blk[1] + 1) * tq)) for blk, _ in grp]
        qs = [jnp.concatenate([q_ref[bi, r, b0 + p * LANES: b0 + (p + 1) * LANES] for p in range(pairs)], axis=0)
              for (bi, r), b0 in zip(rows, base)]
        s = [_dot_nt(x.astype(BF16), y) * (SWA_HEAD_DIM ** -0.5) for x, y in zip(qs, kbd)]
        probs = [[], []]
        for e in range(2):
            se = [jnp.where(vis[blk], x[:, e * nk:(e + 1) * nk], NEG_BIG) for (blk, _), x in zip(grp, s)]
            sink = [jnp.concatenate(
                [jnp.full((tq, 1), sink_ref[kvh * SWA_GROUP + 2 * p + e], F32) for p in range(pairs)], axis=0)
                for _, kvh in grp]
            m = [jnp.maximum(jnp.max(x, axis=-1, keepdims=True), y) for x, y in zip(se, sink)]
            ex = [jnp.exp(x - y) for x, y in zip(se, m)]
            den = [jnp.sum(x, axis=-1, keepdims=True) + jnp.exp(y - z) for x, y, z in zip(ex, sink, m)]
            probs[e] = [x / y for x, y in zip(ex, den)]
        o = [_dot(jnp.concatenate([p0, p1], axis=1).astype(BF16), y)
             for p0, p1, y in zip(probs[0], probs[1], vbd)]
        for (bi, r), b0, x in zip(rows, base, o):
            for p in range(pairs):
                o_ref[bi, r, b0 + p * LANES: b0 + (p + 1) * LANES] = x[p * tq:(p + 1) * tq]


def _swa(sq, sk, sv, sinks, prev_k=None, prev_v=None, nb=1, bb=1):
    bsz, t, _ = sq.shape
    prev_is_block = prev_k is None
    if prev_is_block:
        tq = WINDOW
        prev_k, prev_v = sk, sv
        prev_spec = pl.BlockSpec((bb, WINDOW, SWA_KV), lambda b, i: (b, jnp.maximum(i * nb - 1, 0), 0))
    else:
        tq = t
        assert nb == 1
        prev_spec = pl.BlockSpec((bb, WINDOW, SWA_KV), lambda b, i: (b, 0, 0))
    cur_spec = pl.BlockSpec((bb, nb * tq, SWA_KV), lambda b, i: (b, i, 0))
    q_spec = pl.BlockSpec((bb, nb * tq, SWA_Q), lambda b, i: (b, i, 0))
    return pl.pallas_call(
        functools.partial(_swa_kernel, bb=bb, tq=tq, nb=nb, prev_is_block=prev_is_block),
        grid=(bsz // bb, t // (nb * tq)),
        in_specs=[pl.BlockSpec(memory_space=pltpu.SMEM), q_spec, cur_spec, cur_spec, prev_spec, prev_spec],
        out_specs=q_spec,
        out_shape=jax.ShapeDtypeStruct((bsz, t, SWA_Q), F32),
        compiler_params=pltpu.CompilerParams(
            dimension_semantics=("parallel", "arbitrary"), vmem_limit_bytes=VMEM_LIMIT),
        name="swa",
    )(sinks, sq, sk, sv, prev_k, prev_v)


def _merge_kernel(x_ref, oa_ref, gg_ref, ob_ref, ga_ref, gb_ref, gn_ref, wa_ref, wb_ref, wo_ref, h_ref):
    parts = []
    for hd in range(GLA_HEADS):
        sl = slice(hd * GLA_DV, (hd + 1) * GLA_DV)
        g = gg_ref[:, sl]
        parts.append(_rms(oa_ref[:, sl], gn_ref[...]) * (g * _sigmoid(g)))
    oa = jnp.concatenate(parts, axis=1).astype(BF16)
    br_a = _dot(oa, wa_ref[...])
    br_b = _dot(ob_ref[...].astype(BF16), wb_ref[...])
    merged = _sigmoid(ga_ref[...]) * br_a + _sigmoid(gb_ref[...]) * br_b
    h_ref[...] = x_ref[...] + _dot(merged.astype(BF16), wo_ref[...])


def _merge(x2d, oa, gg, ob, ga, gb, gn, wa, wb, wo, tn):
    n = x2d.shape[0]
    tok = pl.BlockSpec((tn, D_MODEL), lambda i: (i, 0))
    const = lambda i: (0, 0)
    wspec = pl.BlockSpec((D_MODEL, D_MODEL), const)
    return pl.pallas_call(
        _merge_kernel,
        grid=(n // tn,),
        in_specs=[tok] * 6 + [pl.BlockSpec((1, GLA_DV), const), wspec, wspec, wspec],
        out_specs=tok,
        out_shape=jax.ShapeDtypeStruct((n, D_MODEL), F32),
        compiler_params=pltpu.CompilerParams(
            dimension_semantics=("parallel",), vmem_limit_bytes=VMEM_LIMIT),
        name="merge",
    )(x2d, oa, gg, ob, ga, gb, gn, wa, wb, wo)


def _sort16_pairs():
    pairs = []
    n = 16
    p = 1
    while p < n:
        k = p
        while k >= 1:
            for j in range(k % p, n - k, 2 * k):
                for i in range(min(k, n - j - k)):
                    if (i + j) // (2 * p) == (i + j + k) // (2 * p):
                        pairs.append((i + j, i + j + k))
            k //= 2
        p *= 2
    return pairs


_SORT16 = _sort16_pairs()


def _bitonic_top16(xa, xb):
    z = [jnp.maximum(xa[i], xb[15 - i]) for i in range(16)]
    dist = 8
    while dist >= 1:
        for i in range(16):
            if i & dist == 0:
                hi = jnp.maximum(z[i], z[i + dist])
                lo = jnp.minimum(z[i], z[i + dist])
                z[i], z[i + dist] = hi, lo
        dist //= 2
    return z


def _top16_rows(s):
    x = [s[SUBLANES * v:SUBLANES * (v + 1), :] for v in range(16)]
    for i, j in _SORT16:
        hi = jnp.maximum(x[i], x[j])
        lo = jnp.minimum(x[i], x[j])
        x[i], x[j] = hi, lo
    shift = 1
    while shift < SUBLANES:
        x = _bitonic_top16(x, [pltpu.roll(t, shift, 0) for t in x])
        shift *= 2
    return [t[SUBLANES - 1:SUBLANES, :] for t in x]


_CAND_Q = [PEER_TOPK // (p + 1) for p in range(PEER_TOPK)]


PACK = 2 * SUBLANES
_HS_TAU, _HS_ZINV, _HS_TOP1, _HS_ROWS = PEER_TOPK, PEER_TOPK + 1, PEER_TOPK + 2, PEER_TOPK + SUBLANES


def _peer_kernel(h_ref, nf_ref, nfin_ref, wq_ref, k1_ref, k2_ref, u_ref, vt_ref, y_ref,
                 hn_ref, sa_ref, sb_ref, e1_ref, e2_ref, v1_ref, v2_ref, hs_ref,
                 ca_ref, ea_ref, ht_ref, g_ref, acc_ref, *, tn, ec):
    e = pl.program_id(1)
    na = ec // PEER_NKEYS
    nt = tn // LANES
    rb = 4 * SUBLANES

    @pl.when(e == 0)
    def _routing():
        hn_ref[...] = _rms(h_ref[...], nf_ref[...]).astype(BF16)
        acc_ref[...] = jnp.zeros_like(acc_ref)

        for hd in range(PEER_HEADS):
            wq = wq_ref[hd * PEER_QDIM:(hd + 1) * PEER_QDIM, :]
            qt = _dot_nt(wq, hn_ref[...])
            s1 = _dot(k1_ref[hd], qt[:PEER_HALF].astype(BF16))
            s2 = _dot(k2_ref[hd], qt[PEER_HALF:].astype(BF16))
            sa_ref[hd] = s1
            sb_ref[hd] = s2
            for p, r in enumerate(_top16_rows(s1)):
                v1_ref[p, hd:hd + 1, :] = r
            for p, r in enumerate(_top16_rows(s2)):
                v2_ref[p, hd:hd + 1, :] = r
                hs_ref[hd, p:p + 1, :] = r

        v1 = [v1_ref[p] for p in range(PEER_TOPK)]
        v2 = [v2_ref[q] for q in range(PEER_TOPK)]
        ninf = jnp.full((PEER_HEADS, tn), -jnp.inf, F32)
        best = [v1[0] + v2[q] for q in range(PEER_TOPK)]
        for p in range(1, PEER_TOPK):
            row = [v1[p] + v2[q] if q < _CAND_Q[p] else ninf for q in range(PEER_TOPK)]
            best = _bitonic_top16(best, row)
        tau = best[PEER_TOPK - 1]
        top = best[0]
        zsum = jnp.zeros((PEER_HEADS, tn), F32)
        for p in range(PEER_TOPK):
            for q in range(_CAND_Q[p]):
                cnd = v1[p] + v2[q]
                zsum = zsum + jnp.where(cnd >= tau, jnp.exp(cnd - top), 0.0)
        zinv = 1.0 / zsum
        for hd in range(PEER_HEADS):
            hs_ref[hd, _HS_TAU:_HS_TAU + 1, :] = tau[hd:hd + 1]
            hs_ref[hd, _HS_ZINV:_HS_ZINV + 1, :] = zinv[hd:hd + 1]
            hs_ref[hd, _HS_TOP1:_HS_TOP1 + 1, :] = v1[0][hd:hd + 1]

        def staircase(idx, carry):
            hd = idx // (PEER_NKEYS // rb)
            rows = pl.ds(pl.multiple_of((idx % (PEER_NKEYS // rb)) * rb, rb), rb)
            s1 = sa_ref[hd, rows, :]
            tau_h = hs_ref[hd, _HS_TAU:_HS_TAU + 1, :]
            theta = jnp.full((rb, tn), jnp.inf, F32)
            for q in range(PEER_TOPK):
                v2q = hs_ref[hd, q:q + 1, :]
                theta = jnp.where(s1 + v2q >= tau_h, v2q, theta)
            e2_ref[hd, rows, :] = (jnp.exp(sb_ref[hd, rows, :] - hs_ref[hd, 0:1, :])
                                   * hs_ref[hd, _HS_ZINV:_HS_ZINV + 1, :])
            e1_ref[hd, rows, :] = 0.5 * jnp.exp(s1 - hs_ref[hd, _HS_TOP1:_HS_TOP1 + 1, :])
            sa_ref[hd, rows, :] = theta
            return carry

        lax.fori_loop(0, PEER_HEADS * (PEER_NKEYS // rb), staircase, 0)

    ht_ref[...] = _dot_nt(u_ref[...], hn_ref[...])
    a_rows = pl.ds(pl.multiple_of(e * na, na), na)
    for hd in range(PEER_HEADS):
        ca_ref[hd] = sa_ref[hd, a_rows, :]
        ea_ref[hd] = e1_ref[hd, a_rows, :]

    ga = 4
    gb = 4
    n_bg = PEER_NKEYS // (gb * SUBLANES)

    def gate(idx, carry):
        cols = pl.ds(pl.multiple_of((idx // n_bg) * LANES, LANES), LANES)
        b0 = pl.multiple_of((idx % n_bg) * (gb * SUBLANES), gb * SUBLANES)
        bc = lambda ref, hd, r: jnp.broadcast_to(ref[hd, r:r + 1, cols], (SUBLANES, LANES))
        for a_blk in range(0, na, ga):
            w = [[jnp.zeros((SUBLANES, LANES), F32) for _ in range(gb)] for _ in range(ga)]
            for hd in range(PEER_HEADS):
                s2 = [sb_ref[hd, pl.ds(b0 + t * SUBLANES, SUBLANES), cols] for t in range(gb)]
                e2 = [e2_ref[hd, pl.ds(b0 + t * SUBLANES, SUBLANES), cols] for t in range(gb)]
                for i in range(ga):
                    theta_b = bc(ca_ref, hd, a_blk + i)
                    e1_b = bc(ea_ref, hd, a_blk + i)
                    for t in range(gb):
                        w[i][t] = w[i][t] + jnp.where(s2[t] >= theta_b, e2[t], 0.0) * e1_b
            for i in range(ga):
                for t in range(0, gb, 2):
                    rows = pl.ds(pl.multiple_of((a_blk + i) * PEER_NKEYS + b0 + t * SUBLANES, PACK), PACK)
                    hx = ht_ref[rows, cols]
                    act = hx * (1.0 + lax.erf(hx * (2.0 ** -0.5)))
                    g_ref[rows, cols] = (jnp.concatenate([w[i][t], w[i][t + 1]], axis=0) * act).astype(BF16)
        return carry

    lax.fori_loop(0, nt * n_bg, gate, 0)
    acc_ref[...] += _dot(vt_ref[...], g_ref[...])

    @pl.when(e == pl.num_programs(1) - 1)
    def _finish():
        y_ref[...] = _rms(h_ref[...] + acc_ref[...].T, nfin_ref[...])


def _peer(h2d, nf, nfin, wq_t, k1, k2, u, vt, tn, ec):
    n = h2d.shape[0]
    assert ec % (SUBLANES * PEER_NKEYS) == 0 and tn % LANES == 0
    tok = pl.BlockSpec((tn, D_MODEL), lambda i, e: (i, 0))
    const2 = lambda i, e: (0, 0)
    const3 = lambda i, e: (0, 0, 0)
    vec = pl.BlockSpec((1, D_MODEL), const2)
    kspec = pl.BlockSpec((PEER_HEADS, PEER_NKEYS, PEER_HALF), const3)
    head_f32 = pltpu.VMEM((PEER_HEADS, PEER_NKEYS, tn), F32)
    rank_tok = pltpu.VMEM((PEER_TOPK, PEER_HEADS, tn), F32)
    stage = pltpu.VMEM((PEER_HEADS, ec // PEER_NKEYS, tn), F32)
    return pl.pallas_call(
        functools.partial(_peer_kernel, tn=tn, ec=ec),
        grid=(n // tn, PEER_N_EXPERTS // ec),
        in_specs=[tok, vec, vec,
                  pl.BlockSpec((PEER_HEADS * PEER_QDIM, D_MODEL), const2, pipeline_mode=pl.Buffered(1)),
                  kspec, kspec,
                  pl.BlockSpec((ec, D_MODEL), lambda i, e: (e, 0)),
                  pl.BlockSpec((D_MODEL, ec), lambda i, e: (0, e))],
        out_specs=tok,
        out_shape=jax.ShapeDtypeStruct((n, D_MODEL), F32),
        scratch_shapes=[pltpu.VMEM((tn, D_MODEL), BF16),
                        head_f32, head_f32, head_f32, head_f32, rank_tok, rank_tok,
                        pltpu.VMEM((PEER_HEADS, _HS_ROWS, tn), F32),
                        stage, stage, pltpu.VMEM((ec, tn), F32), pltpu.VMEM((ec, tn), BF16),
                        pltpu.VMEM((D_MODEL, tn), F32)],
        compiler_params=pltpu.CompilerParams(
            dimension_semantics=("parallel", "arbitrary"), vmem_limit_bytes=VMEM_LIMIT),
        name="peer",
    )(h2d, nf, nfin, wq_t, k1, k2, u, vt)


def _pack_params(norm_mix, w_in, gla_w_decay, gla_b_decay, gla_norm, swa_sinks, w_branch_a, w_branch_b,
                 w_out, norm_ffn, peer_w_q, peer_keys1, peer_keys2, peer_u, peer_v):
    off = np.cumsum((0, GLA_QK, GLA_QK, GLA_V, GLA_V, GLA_RANK, SWA_Q, SWA_KV, SWA_KV, D_MODEL, D_MODEL))
    col = lambda i: w_in[:, off[i]:off[i + 1]]
    lr = jnp.pad(col(4), ((0, 0), (0, LANES - GLA_RANK)))
    w_all = jnp.concatenate([col(0), col(1), col(2), col(3), col(5), col(6), col(7), col(8), col(9), lr],
                            axis=1).astype(BF16)
    wdec = jnp.pad(gla_w_decay, ((0, LANES - GLA_RANK), (0, 0))).astype(BF16)
    return dict(
        nm=norm_mix[None], w_all=w_all, wdec=wdec, bdec=gla_b_decay[None], gn=gla_norm[None],
        sinks=swa_sinks, wa=w_branch_a.astype(BF16), wb=w_branch_b.astype(BF16), wo=w_out.astype(BF16),
        nf=norm_ffn[None], wq_t=peer_w_q.T.astype(BF16), k1=peer_keys1.astype(BF16),
        k2=peer_keys2.astype(BF16), u=peer_u.astype(BF16), vt=peer_v.T.astype(BF16))


def _layer(x, p, nfin, s0, win_k, win_v, *, tn, gla_tb, gla_bb, swa_nb, swa_bb, peer_tn, peer_ec):
    bsz, t, _ = x.shape
    x2d = x.reshape(bsz * t, D_MODEL)
    gq, gk, gv, gg, la, sq, sk, sv, ga, gb = _inproj(x2d, p["nm"], p["w_all"], p["wdec"], p["bdec"], tn)
    r3 = lambda a: a.reshape(bsz, t, a.shape[-1])
    o_a, s_new = _gla(r3(gq), r3(gk), r3(gv), r3(la), s0, gla_tb, gla_bb)
    if win_k is None:
        o_b = _swa(r3(sq), r3(sk), r3(sv), p["sinks"], nb=swa_nb, bb=swa_bb)
        ln = min(WINDOW, t)
        new_k, new_v = r3(sk)[:, t - ln:], r3(sv)[:, t - ln:]
    else:
        wk = win_k.reshape(bsz, WINDOW, SWA_KV)
        wv = win_v.reshape(bsz, WINDOW, SWA_KV)
        o_b = _swa(r3(sq), r3(sk), r3(sv), p["sinks"], wk, wv, bb=swa_bb)
        new_k = jnp.concatenate([wk, r3(sk)], axis=1)[:, -WINDOW:]
        new_v = jnp.concatenate([wv, r3(sv)], axis=1)[:, -WINDOW:]
    h = _merge(x2d, o_a.reshape(bsz * t, GLA_V), gg, o_b.reshape(bsz * t, SWA_Q), ga, gb,
               p["gn"], p["wa"], p["wb"], p["wo"], tn)
    y = _peer(h, p["nf"], nfin, p["wq_t"], p["k1"], p["k2"], p["u"], p["vt"], peer_tn, peer_ec)
    kv_shape = (bsz, -1, SWA_KV_HEADS, SWA_HEAD_DIM)
    return y.reshape(bsz, t, D_MODEL), new_k.reshape(kv_shape), new_v.reshape(kv_shape), s_new


def kernel(x_prompt, x_sample, cache_win_k, cache_win_v, state_gla, norm_mix, w_in, gla_w_decay, gla_b_decay,
           gla_norm, swa_sinks, w_branch_a, w_branch_b, w_out, norm_ffn, peer_w_q, peer_keys1, peer_keys2,
           peer_u, peer_v, norm_final):
    depth = w_in.shape[0]
    assert depth == 1, "the final norm is fused into the last layer's channel mixer; one layer supported"
    p = _pack_params(norm_mix[0], w_in[0], gla_w_decay[0], gla_b_decay[0], gla_norm[0], swa_sinks[0],
                     w_branch_a[0], w_branch_b[0], w_out[0], norm_ffn[0], peer_w_q[0], peer_keys1[0],
                     peer_keys2[0], peer_u[0], peer_v[0])
    nfin = norm_final[None]
    yp, pk, pv, ps = _layer(x_prompt, p, nfin, None, None, None,
                            tn=512, gla_tb=512, gla_bb=1, swa_nb=4, swa_bb=1, peer_tn=512, peer_ec=2048)
    ys, sk, sv, ss = _layer(x_sample, p, nfin, state_gla[0], cache_win_k[0], cache_win_v[0],
                            tn=512, gla_tb=x_sample.shape[1], gla_bb=4, swa_nb=1, swa_bb=4, peer_tn=512, peer_ec=2048)
    return (yp, ys, pk[None], pv[None], ps[None], sk[None], sv[None], ss[None])
```

```python
import functools

import jax
import jax.numpy as jnp
import numpy as np
from jax import lax
from jax.experimental import pallas as pl
from jax.experimental.pallas import tpu as pltpu

F32 = jnp.float32
BF16 = jnp.bfloat16

D_MODEL = 1024
GLA_HEADS = 4
GLA_DK = 128
GLA_DV = 256
GLA_RANK = 16
GLA_TAU = 16.0
SWA_HEADS = 16
SWA_KV_HEADS = 2
SWA_GROUP = SWA_HEADS // SWA_KV_HEADS
SWA_HEAD_DIM = 64
WINDOW = 128
PEER_HEADS = 8
PEER_NKEYS = 128
PEER_N_EXPERTS = PEER_NKEYS * PEER_NKEYS
PEER_QDIM = 256
PEER_HALF = PEER_QDIM // 2
PEER_TOPK = 16
EPS = 1e-6

GLA_QK = GLA_HEADS * GLA_DK
GLA_V = GLA_HEADS * GLA_DV
SWA_Q = SWA_HEADS * SWA_HEAD_DIM
SWA_KV = SWA_KV_HEADS * SWA_HEAD_DIM

LANES = 128
SUBLANES = 8
GLA_CHUNK = 128
GLA_HEAD_GROUP = 4
GLA_SAFE_DECAY = 60.0
VMEM_LIMIT = 56 * 1024 * 1024
NEG_BIG = -1e30

_NT = (((1,), (1,)), ((), ()))


def _rms(x, g):
    return x * lax.rsqrt(jnp.mean(x * x, axis=-1, keepdims=True) + EPS) * g


def _dot(a, b):
    return jnp.dot(a, b, preferred_element_type=F32)


def _dot_nt(a, b):
    return lax.dot_general(a, b, _NT, preferred_element_type=F32)


def _sigmoid(x):
    return 1.0 / (1.0 + jnp.exp(-x))


_P_GQ, _P_GK, _P_GV, _P_GG, _P_SQ, _P_SK, _P_SV, _P_GA, _P_GB, _P_LR, _P_END = (
    0, 512, 1024, 2048, 3072, 4096, 4224, 4352, 5376, 6400, 6528)


def _inproj_kernel(x_ref, nm_ref, w_ref, wdec_ref, bdec_ref,
                   gq_ref, gk_ref, gv_ref, gg_ref, la_ref, sq_ref, sk_ref, sv_ref, ga_ref, gb_ref):
    xb = _rms(x_ref[...], nm_ref[...]).astype(BF16)

    def proj(lo, hi):
        return _dot(xb, w_ref[:, lo:hi])

    gq_ref[...] = proj(_P_GQ, _P_GK)
    gk_ref[...] = proj(_P_GK, _P_GV)
    gv_ref[...] = proj(_P_GV, _P_GG)
    gg_ref[...] = proj(_P_GG, _P_SQ)
    sq_ref[...] = proj(_P_SQ, _P_SK)
    sk_ref[...] = proj(_P_SK, _P_SV)
    sv_ref[...] = proj(_P_SV, _P_GA)
    ga_ref[...] = proj(_P_GA, _P_GB)
    gb_ref[...] = proj(_P_GB, _P_LR)
    glr = proj(_P_LR, _P_END)
    z = _dot(glr.astype(BF16), wdec_ref[...]) + bdec_ref[...]
    la_ref[...] = (jnp.minimum(z, 0.0) - jnp.log(1.0 + jnp.exp(-jnp.abs(z)))) * (1.0 / GLA_TAU)


def _inproj(x2d, nm, w_all, wdec, bdec, tn):
    n = x2d.shape[0]
    widths = (GLA_QK, GLA_QK, GLA_V, GLA_V, GLA_QK, SWA_Q, SWA_KV, SWA_KV, D_MODEL, D_MODEL)
    const = lambda i: (0, 0)
    return pl.pallas_call(
        _inproj_kernel,
        grid=(n // tn,),
        in_specs=[
            pl.BlockSpec((tn, D_MODEL), lambda i: (i, 0)),
            pl.BlockSpec((1, D_MODEL), const),
            pl.BlockSpec((D_MODEL, _P_END), const, pipeline_mode=pl.Buffered(1)),
            pl.BlockSpec((LANES, GLA_QK), const),
            pl.BlockSpec((1, GLA_QK), const),
        ],
        out_specs=[pl.BlockSpec((tn, w), lambda i: (i, 0)) for w in widths],
        out_shape=[jax.ShapeDtypeStruct((n, w), F32) for w in widths],
        compiler_params=pltpu.CompilerParams(
            dimension_semantics=("parallel",), vmem_limit_bytes=VMEM_LIMIT),
        name="in_proj",
    )(x2d, nm, w_all, wdec, bdec)


def _gla_kernel(*refs, bb, n_chunks, t_valid, has_s0):
    if has_s0:
        q_ref, k_ref, v_ref, la_ref, s0_ref, o_ref, so_ref, s_ref = refs
    else:
        q_ref, k_ref, v_ref, la_ref, o_ref, so_ref, s_ref = refs
    j = pl.program_id(1)
    c = GLA_CHUNK

    @pl.when(j == 0)
    def _():
        if has_s0:
            s_ref[...] = s0_ref[...]
        else:
            s_ref[...] = jnp.zeros_like(s_ref)

    row = lax.broadcasted_iota(jnp.int32, (c, c), 0)
    col = lax.broadcasted_iota(jnp.int32, (c, c), 1)
    causal = row >= col
    tril = causal.astype(BF16)
    scale = GLA_DK ** -0.5

    def load(ref, bi, r0, lo, w):
        if t_valid >= c:
            return ref[bi, r0:r0 + c, lo:lo + w]
        t = ref[bi, :, lo:lo + w]
        return jnp.concatenate([t, jnp.zeros((c - t_valid, w), F32)], axis=0)

    def pairwise_scores(qs, k, b):
        def score_row(i, a_t):
            pick = row == i
            q_i = jnp.sum(jnp.where(pick, qs, 0.0), axis=0, keepdims=True)
            b_i = jnp.sum(jnp.where(pick, b, 0.0), axis=0, keepdims=True)
            rel = jnp.where(row <= i, b_i - b, -jnp.inf)
            s_col = jnp.sum(k * jnp.exp(rel) * q_i, axis=1, keepdims=True)
            return jnp.where(col == i, s_col, a_t)
        return lax.fori_loop(0, c, score_row, jnp.zeros((c, c), F32)).T

    def run(factored):
        pairs = [(bi, h) for bi in range(bb) for h in range(GLA_HEADS)]
        groups = [pairs[g0:g0 + GLA_HEAD_GROUP] for g0 in range(0, len(pairs), GLA_HEAD_GROUP)]
        for ci in range(n_chunks):
            r0 = ci * c
            for hs in groups:
                q = [load(q_ref, bi, r0, h * GLA_DK, GLA_DK) for bi, h in hs]
                k = [load(k_ref, bi, r0, h * GLA_DK, GLA_DK) for bi, h in hs]
                v = [load(v_ref, bi, r0, h * GLA_DV, GLA_DV).astype(BF16) for bi, h in hs]
                la = [load(la_ref, bi, r0, h * GLA_DK, GLA_DK) for bi, h in hs]
                la_hi = [x.astype(BF16) for x in la]
                la_lo = [(x - y.astype(F32)).astype(BF16) for x, y in zip(la, la_hi)]
                b = [_dot(tril, x) + _dot(tril, y) for x, y in zip(la_hi, la_lo)]
                qs = [x * scale for x in q]
                qd = [(x * jnp.exp(y)).astype(BF16) for x, y in zip(qs, b)]
                if factored:
                    kn = [(x * jnp.exp(-y)).astype(BF16) for x, y in zip(k, b)]
                    a = [jnp.where(causal, _dot_nt(x, y), 0.0) for x, y in zip(qd, kn)]
                else:
                    a = [pairwise_scores(x, y, z) for x, y, z in zip(qs, k, b)]
                s_old = [s_ref[bi, h] for bi, h in hs]
                o = [_dot(x, s.astype(BF16)) + _dot(y.astype(BF16), z) for x, s, y, z in zip(qd, s_old, a, v)]
                for (bi, h), x in zip(hs, o):
                    if t_valid >= c:
                        o_ref[bi, r0:r0 + c, h * GLA_DV:(h + 1) * GLA_DV] = x
                    else:
                        o_ref[bi, :, h * GLA_DV:(h + 1) * GLA_DV] = x[:t_valid]
                bl = [y[c - 1:c, :] for y in b]
                kd_t = [(x * jnp.exp(l - y)).T.astype(BF16) for x, l, y in zip(k, bl, b)]
                dec_t = [jnp.broadcast_to(jnp.exp(l), (c, GLA_DK)).T for l in bl]
                for (bi, h), d, s, x, z in zip(hs, dec_t, s_old, kd_t, v):
                    s_ref[bi, h] = jnp.concatenate([d] * (GLA_DV // c), axis=1) * s + _dot(x, z)

    worst = jnp.float32(0.0)
    for bi in range(bb):
        for ci in range(n_chunks):
            rows = slice(ci * c, (ci + 1) * c) if t_valid >= c else slice(None)
            worst = jnp.maximum(worst, jnp.max(-jnp.sum(la_ref[bi, rows, :], axis=0, keepdims=True)))
    safe = worst < GLA_SAFE_DECAY
    pl.when(safe)(lambda: run(True))
    pl.when(jnp.logical_not(safe))(lambda: run(False))

    @pl.when(j == pl.num_programs(1) - 1)
    def _():
        so_ref[...] = s_ref[...]


def _gla(gq, gk, gv, la, s0, tb, bb):
    bsz, t, _ = gq.shape
    t_valid = min(tb, GLA_CHUNK)
    n_chunks = max(tb // GLA_CHUNK, 1)
    has_s0 = s0 is not None
    tok = lambda w: pl.BlockSpec((bb, tb, w), lambda b, j: (b, j, 0))
    st = pl.BlockSpec((bb, GLA_HEADS, GLA_DK, GLA_DV), lambda b, j: (b, 0, 0, 0))
    in_specs = [tok(GLA_QK), tok(GLA_QK), tok(GLA_V), tok(GLA_QK)] + ([st] if has_s0 else [])
    args = (gq, gk, gv, la) + ((s0,) if has_s0 else ())
    return pl.pallas_call(
        functools.partial(_gla_kernel, bb=bb, n_chunks=n_chunks, t_valid=t_valid, has_s0=has_s0),
        grid=(bsz // bb, t // tb),
        in_specs=in_specs,
        out_specs=[tok(GLA_V), st],
        out_shape=[jax.ShapeDtypeStruct((bsz, t, GLA_V), F32),
                   jax.ShapeDtypeStruct((bsz, GLA_HEADS, GLA_DK, GLA_DV), F32)],
        scratch_shapes=[pltpu.VMEM((bb, GLA_HEADS, GLA_DK, GLA_DV), F32)],
        compiler_params=pltpu.CompilerParams(
            dimension_semantics=("parallel", "arbitrary"), vmem_limit_bytes=VMEM_LIMIT),
        name="gla",
    )(*args)


def _swa_kernel(sink_ref, q_ref, kc_ref, vc_ref, kp_ref, vp_ref, o_ref, s_ref, p_ref, *, bb, tq, nb, prev_is_block):
    i = pl.program_id(1)
    w = WINDOW
    nk = 2 * w
    pairs = SWA_GROUP // 2
    t_idx = lax.broadcasted_iota(jnp.int32, (tq, nk), 0)
    j_idx = lax.broadcasted_iota(jnp.int32, (tq, nk), 1)
    d = j_idx - t_idx
    band = (d >= 1) & (d <= w)
    lane = lax.broadcasted_iota(jnp.int32, (nk, LANES), 1)
    left = lane < SWA_HEAD_DIM

    def window(prev_ref, cur_ref, bi, blk):
        cur = cur_ref[bi, blk * tq:(blk + 1) * tq, :]
        if tq < w:
            cur = jnp.concatenate([cur, jnp.zeros((w - tq, SWA_KV), F32)], axis=0)
        prev = prev_ref[bi] if blk == 0 else cur_ref[bi, (blk - 1) * tq:blk * tq, :]
        full = jnp.concatenate([prev, cur], axis=0)
        return full, pltpu.roll(full, SWA_HEAD_DIM, 1)

    def block_diag(full, rot, kvh):
        a, b = (full, rot) if kvh == 0 else (rot, full)
        return jnp.concatenate([jnp.where(left, a, 0.0), jnp.where(left, 0.0, b)], axis=0).astype(BF16)

    def mask(blk):
        if prev_is_block and blk == 0:
            return band & ((j_idx >= w) | (i > 0))
        return band

    units = [((bi, blk), kvh) for bi in range(bb) for blk in range(nb) for kvh in range(SWA_KV_HEADS)]
    ug = s_ref.shape[0]
    win, vis = {}, {}
    for g0 in range(0, len(units), ug):
        grp = units[g0:g0 + ug]
        for blk, _ in grp:
            if blk not in win:
                win[blk] = (window(kp_ref, kc_ref, *blk), window(vp_ref, vc_ref, *blk))
                vis[blk] = mask(blk[1])
        kbd = [block_diag(*win[blk][0], kvh) for blk, kvh in grp]
        vbd = [block_diag(*win[blk][1], kvh) for blk, kvh in grp]
        base = [kvh * SWA_GROUP * SWA_HEAD_DIM for _, kvh in grp]
        rows = [(blk[0], slice(blk[1] * tq, (blk[1] + 1) * tq)) for blk, _ in grp]
        for gi, ((bi, r), b0) in enumerate(zip(rows, base)):
            qs = jnp.concatenate([q_ref[bi, r, b0 + p * LANES: b0 + (p + 1) * LANES] for p in range(pairs)], axis=0)
            s_ref[gi] = _dot_nt(qs.astype(BF16), kbd[gi]) * (SWA_HEAD_DIM ** -0.5)
        for p in range(pairs):
            for e in range(2):
                pr, seg = slice(p * tq, (p + 1) * tq), slice(e * nk, (e + 1) * nk)
                se = [jnp.where(vis[blk], s_ref[gi, pr, seg], NEG_BIG) for gi, (blk, _) in enumerate(grp)]
                sink = [sink_ref[kvh * SWA_GROUP + 2 * p + e] for _, kvh in grp]
                m = [jnp.maximum(jnp.max(x, axis=-1, keepdims=True), y) for x, y in zip(se, sink)]
                ex = [jnp.exp(x - y) for x, y in zip(se, m)]
                den = [jnp.sum(x, axis=-1, keepdims=True) + jnp.exp(y - z) for x, y, z in zip(ex, sink, m)]
                for gi, (x, y) in enumerate(zip(ex, den)):
                    p_ref[gi, pr, seg] = (x / y).astype(BF16)
        for gi, ((bi, r), b0) in enumerate(zip(rows, base)):
            o = _dot(p_ref[gi], vbd[gi])
            for p in range(pairs):
                o_ref[bi, r, b0 + p * LANES: b0 + (p + 1) * LANES] = o[p * tq:(p + 1) * tq]


def _swa(sq, sk, sv, sinks, prev_k=None, prev_v=None, nb=1, bb=1):
    bsz, t, _ = sq.shape
    prev_is_block = prev_k is None
    if prev_is_block:
        tq = WINDOW
        prev_k, prev_v = sk, sv
        prev_spec = pl.BlockSpec((bb, WINDOW, SWA_KV), lambda b, i: (b, jnp.maximum(i * nb - 1, 0), 0))
    else:
        tq = t
        assert nb == 1
        prev_spec = pl.BlockSpec((bb, WINDOW, SWA_KV), lambda b, i: (b, 0, 0))
    ug = SWA_KV_HEADS
    rows = (SWA_GROUP // 2) * tq
    cur_spec = pl.BlockSpec((bb, nb * tq, SWA_KV), lambda b, i: (b, i, 0))
    q_spec = pl.BlockSpec((bb, nb * tq, SWA_Q), lambda b, i: (b, i, 0))
    return pl.pallas_call(
        functools.partial(_swa_kernel, bb=bb, tq=tq, nb=nb, prev_is_block=prev_is_block),
        grid=(bsz // bb, t // (nb * tq)),
        in_specs=[pl.BlockSpec(memory_space=pltpu.SMEM), q_spec, cur_spec, cur_spec, prev_spec, prev_spec],
        out_specs=q_spec,
        out_shape=jax.ShapeDtypeStruct((bsz, t, SWA_Q), F32),
        scratch_shapes=[pltpu.VMEM((ug, rows, 4 * WINDOW), F32), pltpu.VMEM((ug, rows, 4 * WINDOW), BF16)],
        compiler_params=pltpu.CompilerParams(
            dimension_semantics=("parallel", "arbitrary"), vmem_limit_bytes=VMEM_LIMIT),
        name="swa",
    )(sinks, sq, sk, sv, prev_k, prev_v)


def _merge_kernel(x_ref, oa_ref, gg_ref, ob_ref, ga_ref, gb_ref, gn_ref, wa_ref, wb_ref, wo_ref, h_ref):
    parts = []
    for hd in range(GLA_HEADS):
        sl = slice(hd * GLA_DV, (hd + 1) * GLA_DV)
        g = gg_ref[:, sl]
        parts.append(_rms(oa_ref[:, sl], gn_ref[...]) * (g * _sigmoid(g)))
    oa = jnp.concatenate(parts, axis=1).astype(BF16)
    br_a = _dot(oa, wa_ref[...])
    br_b = _dot(ob_ref[...].astype(BF16), wb_ref[...])
    merged = _sigmoid(ga_ref[...]) * br_a + _sigmoid(gb_ref[...]) * br_b
    h_ref[...] = x_ref[...] + _dot(merged.astype(BF16), wo_ref[...])


def _merge(x2d, oa, gg, ob, ga, gb, gn, wa, wb, wo, tn):
    n = x2d.shape[0]
    tok = pl.BlockSpec((tn, D_MODEL), lambda i: (i, 0))
    const = lambda i: (0, 0)
    wspec = pl.BlockSpec((D_MODEL, D_MODEL), const)
    return pl.pallas_call(
        _merge_kernel,
        grid=(n // tn,),
        in_specs=[tok] * 6 + [pl.BlockSpec((1, GLA_DV), const), wspec, wspec, wspec],
        out_specs=tok,
        out_shape=jax.ShapeDtypeStruct((n, D_MODEL), F32),
        compiler_params=pltpu.CompilerParams(
            dimension_semantics=("parallel",), vmem_limit_bytes=VMEM_LIMIT),
        name="merge",
    )(x2d, oa, gg, ob, ga, gb, gn, wa, wb, wo)


def _sort16_pairs():
    pairs = []
    n = 16
    p = 1
    while p < n:
        k = p
        while k >= 1:
            for j in range(k % p, n - k, 2 * k):
                for i in range(min(k, n - j - k)):
                    if (i + j) // (2 * p) == (i + j + k) // (2 * p):
                        pairs.append((i + j, i + j + k))
            k //= 2
        p *= 2
    return pairs


_SORT16 = _sort16_pairs()


def _bitonic_top16(xa, xb):
    z = [jnp.maximum(xa[i], xb[15 - i]) for i in range(16)]
    dist = 8
    while dist >= 1:
        for i in range(16):
            if i & dist == 0:
                hi = jnp.maximum(z[i], z[i + dist])
                lo = jnp.minimum(z[i], z[i + dist])
                z[i], z[i + dist] = hi, lo
        dist //= 2
    return z


def _top16_rows(s):
    x = [s[SUBLANES * v:SUBLANES * (v + 1), :] for v in range(16)]
    for i, j in _SORT16:
        hi = jnp.maximum(x[i], x[j])
        lo = jnp.minimum(x[i], x[j])
        x[i], x[j] = hi, lo
    shift = 1
    while shift < SUBLANES:
        x = _bitonic_top16(x, [pltpu.roll(t, shift, 0) for t in x])
        shift *= 2
    return [t[SUBLANES - 1:SUBLANES, :] for t in x]


_CAND_Q = [PEER_TOPK // (p + 1) for p in range(PEER_TOPK)]


PACK = 2 * SUBLANES
_HS_TAU, _HS_ZINV, _HS_TOP1, _HS_ROWS = PEER_TOPK, PEER_TOPK + 1, PEER_TOPK + 2, PEER_TOPK + SUBLANES


def _peer_kernel(h_ref, nf_ref, nfin_ref, wq_ref, k1_ref, k2_ref, u_ref, vt_ref, y_ref,
                 hn_ref, sa_ref, sb_ref, e1_ref, e2_ref, v1_ref, v2_ref, hs_ref,
                 ca_ref, ea_ref, ht_ref, g_ref, acc_ref, *, tn, ec):
    e = pl.program_id(1)
    na = ec // PEER_NKEYS
    nt = tn // LANES
    rb = 4 * SUBLANES

    @pl.when(e == 0)
    def _routing():
        hn_ref[...] = _rms(h_ref[...], nf_ref[...]).astype(BF16)
        acc_ref[...] = jnp.zeros_like(acc_ref)

        for hd in range(PEER_HEADS):
            wq = wq_ref[hd * PEER_QDIM:(hd + 1) * PEER_QDIM, :]
            qt = _dot_nt(wq, hn_ref[...])
            s1 = _dot(k1_ref[hd], qt[:PEER_HALF].astype(BF16))
            s2 = _dot(k2_ref[hd], qt[PEER_HALF:].astype(BF16))
            sa_ref[hd] = s1
            sb_ref[hd] = s2
            for p, r in enumerate(_top16_rows(s1)):
                v1_ref[p, hd:hd + 1, :] = r
            for p, r in enumerate(_top16_rows(s2)):
                v2_ref[p, hd:hd + 1, :] = r
                hs_ref[hd, p:p + 1, :] = r

        v1 = [v1_ref[p] for p in range(PEER_TOPK)]
        v2 = [v2_ref[q] for q in range(PEER_TOPK)]
        ninf = jnp.full((PEER_HEADS, tn), -jnp.inf, F32)
        best = [v1[0] + v2[q] for q in range(PEER_TOPK)]
        for p in range(1, PEER_TOPK):
            row = [v1[p] + v2[q] if q < _CAND_Q[p] else ninf for q in range(PEER_TOPK)]
            best = _bitonic_top16(best, row)
        tau = best[PEER_TOPK - 1]
        top = best[0]
        zsum = jnp.zeros((PEER_HEADS, tn), F32)
        for p in range(PEER_TOPK):
            for q in range(_CAND_Q[p]):
                cnd = v1[p] + v2[q]
                zsum = zsum + jnp.where(cnd >= tau, jnp.exp(cnd - top), 0.0)
        zinv = 1.0 / zsum
        for hd in range(PEER_HEADS):
            hs_ref[hd, _HS_TAU:_HS_TAU + 1, :] = tau[hd:hd + 1]
            hs_ref[hd, _HS_ZINV:_HS_ZINV + 1, :] = zinv[hd:hd + 1]
            hs_ref[hd, _HS_TOP1:_HS_TOP1 + 1, :] = v1[0][hd:hd + 1]

        def staircase(idx, carry):
            hd = idx // (PEER_NKEYS // rb)
            rows = pl.ds(pl.multiple_of((idx % (PEER_NKEYS // rb)) * rb, rb), rb)
            s1 = sa_ref[hd, rows, :]
            tau_h = hs_ref[hd, _HS_TAU:_HS_TAU + 1, :]
            theta = jnp.full((rb, tn), jnp.inf, F32)
            for q in range(PEER_TOPK):
                v2q = hs_ref[hd, q:q + 1, :]
                theta = jnp.where(s1 + v2q >= tau_h, v2q, theta)
            e2_ref[hd, rows, :] = (jnp.exp(sb_ref[hd, rows, :] - hs_ref[hd, 0:1, :])
                                   * hs_ref[hd, _HS_ZINV:_HS_ZINV + 1, :])
            e1_ref[hd, rows, :] = 0.5 * jnp.exp(s1 - hs_ref[hd, _HS_TOP1:_HS_TOP1 + 1, :])
            sa_ref[hd, rows, :] = theta
            return carry

        lax.fori_loop(0, PEER_HEADS * (PEER_NKEYS // rb), staircase, 0)

    ht_ref[...] = _dot_nt(u_ref[...], hn_ref[...])
    a_rows = pl.ds(pl.multiple_of(e * na, na), na)
    for hd in range(PEER_HEADS):
        ca_ref[hd] = sa_ref[hd, a_rows, :]
        ea_ref[hd] = e1_ref[hd, a_rows, :]

    ga = 4
    gb = 4
    n_bg = PEER_NKEYS // (gb * SUBLANES)

    def gate(idx, carry):
        cols = pl.ds(pl.multiple_of((idx // n_bg) * LANES, LANES), LANES)
        b0 = pl.multiple_of((idx % n_bg) * (gb * SUBLANES), gb * SUBLANES)
        bc = lambda ref, hd, r: jnp.broadcast_to(ref[hd, r:r + 1, cols], (SUBLANES, LANES))
        for a_blk in range(0, na, ga):
            w = [[jnp.zeros((SUBLANES, LANES), F32) for _ in range(gb)] for _ in range(ga)]
            for hd in range(PEER_HEADS):
                s2 = [sb_ref[hd, pl.ds(b0 + t * SUBLANES, SUBLANES), cols] for t in range(gb)]
                e2 = [e2_ref[hd, pl.ds(b0 + t * SUBLANES, SUBLANES), cols] for t in range(gb)]
                for i in range(ga):
                    theta_b = bc(ca_ref, hd, a_blk + i)
                    e1_b = bc(ea_ref, hd, a_blk + i)
                    for t in range(gb):
                        w[i][t] = w[i][t] + jnp.where(s2[t] >= theta_b, e2[t], 0.0) * e1_b
            for i in range(ga):
                for t in range(0, gb, 2):
                    rows = pl.ds(pl.multiple_of((a_blk + i) * PEER_NKEYS + b0 + t * SUBLANES, PACK), PACK)
                    hx = ht_ref[rows, cols]
                    act = hx * (1.0 + lax.erf(hx * (2.0 ** -0.5)))
                    g_ref[rows, cols] = (jnp.concatenate([w[i][t], w[i][t + 1]], axis=0) * act).astype(BF16)
        return carry

    lax.fori_loop(0, nt * n_bg, gate, 0)
    acc_ref[...] += _dot(vt_ref[...], g_ref[...])

    @pl.when(e == pl.num_programs(1) - 1)
    def _finish():
        y_ref[...] = _rms(h_ref[...] + acc_ref[...].T, nfin_ref[...])


def _peer(h2d, nf, nfin, wq_t, k1, k2, u, vt, tn, ec):
    n = h2d.shape[0]
    assert ec % (SUBLANES * PEER_NKEYS) == 0 and tn % LANES == 0
    tok = pl.BlockSpec((tn, D_MODEL), lambda i, e: (i, 0))
    const2 = lambda i, e: (0, 0)
    const3 = lambda i, e: (0, 0, 0)
    vec = pl.BlockSpec((1, D_MODEL), const2)
    kspec = pl.BlockSpec((PEER_HEADS, PEER_NKEYS, PEER_HALF), const3)
    head_f32 = pltpu.VMEM((PEER_HEADS, PEER_NKEYS, tn), F32)
    rank_tok = pltpu.VMEM((PEER_TOPK, PEER_HEADS, tn), F32)
    stage = pltpu.VMEM((PEER_HEADS, ec // PEER_NKEYS, tn), F32)
    return pl.pallas_call(
        functools.partial(_peer_kernel, tn=tn, ec=ec),
        grid=(n // tn, PEER_N_EXPERTS // ec),
        in_specs=[tok, vec, vec,
                  pl.BlockSpec((PEER_HEADS * PEER_QDIM, D_MODEL), const2, pipeline_mode=pl.Buffered(1)),
                  kspec, kspec,
                  pl.BlockSpec((ec, D_MODEL), lambda i, e: (e, 0)),
                  pl.BlockSpec((D_MODEL, ec), lambda i, e: (0, e))],
        out_specs=tok,
        out_shape=jax.ShapeDtypeStruct((n, D_MODEL), F32),
        scratch_shapes=[pltpu.VMEM((tn, D_MODEL), BF16),
                        head_f32, head_f32, head_f32, head_f32, rank_tok, rank_tok,
                        pltpu.VMEM((PEER_HEADS, _HS_ROWS, tn), F32),
                        stage, stage, pltpu.VMEM((ec, tn), F32), pltpu.VMEM((ec, tn), BF16),
                        pltpu.VMEM((D_MODEL, tn), F32)],
        compiler_params=pltpu.CompilerParams(
            dimension_semantics=("parallel", "arbitrary"), vmem_limit_bytes=VMEM_LIMIT),
        name="peer",
    )(h2d, nf, nfin, wq_t, k1, k2, u, vt)


def _pack_params(norm_mix, w_in, gla_w_decay, gla_b_decay, gla_norm, swa_sinks, w_branch_a, w_branch_b,
                 w_out, norm_ffn, peer_w_q, peer_keys1, peer_keys2, peer_u, peer_v):
    off = np.cumsum((0, GLA_QK, GLA_QK, GLA_V, GLA_V, GLA_RANK, SWA_Q, SWA_KV, SWA_KV, D_MODEL, D_MODEL))
    col = lambda i: w_in[:, off[i]:off[i + 1]]
    lr = jnp.pad(col(4), ((0, 0), (0, LANES - GLA_RANK)))
    w_all = jnp.concatenate([col(0), col(1), col(2), col(3), col(5), col(6), col(7), col(8), col(9), lr],
                            axis=1).astype(BF16)
    wdec = jnp.pad(gla_w_decay, ((0, LANES - GLA_RANK), (0, 0))).astype(BF16)
    return dict(
        nm=norm_mix[None], w_all=w_all, wdec=wdec, bdec=gla_b_decay[None], gn=gla_norm[None],
        sinks=swa_sinks, wa=w_branch_a.astype(BF16), wb=w_branch_b.astype(BF16), wo=w_out.astype(BF16),
        nf=norm_ffn[None], wq_t=peer_w_q.T.astype(BF16), k1=peer_keys1.astype(BF16),
        k2=peer_keys2.astype(BF16), u=peer_u.astype(BF16), vt=peer_v.T.astype(BF16))


def _layer(x, p, nfin, s0, win_k, win_v, *, tn, gla_tb, gla_bb, swa_nb, swa_bb, peer_tn, peer_ec):
    bsz, t, _ = x.shape
    x2d = x.reshape(bsz * t, D_MODEL)
    gq, gk, gv, gg, la, sq, sk, sv, ga, gb = _inproj(x2d, p["nm"], p["w_all"], p["wdec"], p["bdec"], tn)
    r3 = lambda a: a.reshape(bsz, t, a.shape[-1])
    o_a, s_new = _gla(r3(gq), r3(gk), r3(gv), r3(la), s0, gla_tb, gla_bb)
    if win_k is None:
        o_b = _swa(r3(sq), r3(sk), r3(sv), p["sinks"], nb=swa_nb, bb=swa_bb)
        ln = min(WINDOW, t)
        new_k, new_v = r3(sk)[:, t - ln:], r3(sv)[:, t - ln:]
    else:
        wk = win_k.reshape(bsz, WINDOW, SWA_KV)
        wv = win_v.reshape(bsz, WINDOW, SWA_KV)
        o_b = _swa(r3(sq), r3(sk), r3(sv), p["sinks"], wk, wv, bb=swa_bb)
        new_k = jnp.concatenate([wk, r3(sk)], axis=1)[:, -WINDOW:]
        new_v = jnp.concatenate([wv, r3(sv)], axis=1)[:, -WINDOW:]
    h = _merge(x2d, o_a.reshape(bsz * t, GLA_V), gg, o_b.reshape(bsz * t, SWA_Q), ga, gb,
               p["gn"], p["wa"], p["wb"], p["wo"], tn)
    y = _peer(h, p["nf"], nfin, p["wq_t"], p["k1"], p["k2"], p["u"], p["vt"], peer_tn, peer_ec)
    kv_shape = (bsz, -1, SWA_KV_HEADS, SWA_HEAD_DIM)
    return y.reshape(bsz, t, D_MODEL), new_k.reshape(kv_shape), new_v.reshape(kv_shape), s_new


def kernel(x_prompt, x_sample, cache_win_k, cache_win_v, state_gla, norm_mix, w_in, gla_w_decay, gla_b_decay,
           gla_norm, swa_sinks, w_branch_a, w_branch_b, w_out, norm_ffn, peer_w_q, peer_keys1, peer_keys2,
           peer_u, peer_v, norm_final):
    depth = w_in.shape[0]
    assert depth == 1, "the final norm is fused into the last layer's channel mixer; one layer supported"
    p = _pack_params(norm_mix[0], w_in[0], gla_w_decay[0], gla_b_decay[0], gla_norm[0], swa_sinks[0],
                     w_branch_a[0], w_branch_b[0], w_out[0], norm_ffn[0], peer_w_q[0], peer_keys1[0],
                     peer_keys2[0], peer_u[0], peer_v[0])
    nfin = norm_final[None]
    yp, pk, pv, ps = _layer(x_prompt, p, nfin, None, None, None,
                            tn=512, gla_tb=512, gla_bb=1, swa_nb=4, swa_bb=1, peer_tn=512, peer_ec=2048)
    ys, sk, sv, ss = _layer(x_sample, p, nfin, state_gla[0], cache_win_k[0], cache_win_v[0],
                            tn=512, gla_tb=x_sample.shape[1], gla_bb=4, swa_nb=1, swa_bb=4, peer_tn=512, peer_ec=2048)
    return (yp, ys, pk[None], pv[None], ps[None], sk[None], sv[None], ss[None])
```

```python
import functools

import jax
import jax.numpy as jnp
import numpy as np
from jax import lax
from jax.experimental import pallas as pl
from jax.experimental.pallas import tpu as pltpu

F32 = jnp.float32
BF16 = jnp.bfloat16

D_MODEL = 1024
GLA_HEADS = 4
GLA_DK = 128
GLA_DV = 256
GLA_RANK = 16
GLA_TAU = 16.0
SWA_HEADS = 16
SWA_KV_HEADS = 2
SWA_GROUP = SWA_HEADS // SWA_KV_HEADS
SWA_HEAD_DIM = 64
WINDOW = 128
PEER_HEADS = 8
PEER_NKEYS = 128
PEER_N_EXPERTS = PEER_NKEYS * PEER_NKEYS
PEER_QDIM = 256
PEER_HALF = PEER_QDIM // 2
PEER_TOPK = 16
EPS = 1e-6

GLA_QK = GLA_HEADS * GLA_DK
GLA_V = GLA_HEADS * GLA_DV
SWA_Q = SWA_HEADS * SWA_HEAD_DIM
SWA_KV = SWA_KV_HEADS * SWA_HEAD_DIM

LANES = 128
SUBLANES = 8
GLA_CHUNK = 128
GLA_HEAD_GROUP = 4
GLA_SAFE_DECAY = 60.0
VMEM_LIMIT = 56 * 1024 * 1024
NEG_BIG = -1e30

_NT = (((1,), (1,)), ((), ()))


def _rms(x, g):
    return x * lax.rsqrt(jnp.mean(x * x, axis=-1, keepdims=True) + EPS) * g


def _dot(a, b):
    return jnp.dot(a, b, preferred_element_type=F32)


def _dot_nt(a, b):
    return lax.dot_general(a, b, _NT, preferred_element_type=F32)


def _sigmoid(x):
    return 1.0 / (1.0 + jnp.exp(-x))


_P_GQ, _P_GK, _P_GV, _P_GG, _P_SQ, _P_SK, _P_SV, _P_GA, _P_GB, _P_LR, _P_END = (
    0, 512, 1024, 2048, 3072, 4096, 4224, 4352, 5376, 6400, 6528)


def _inproj_kernel(x_ref, nm_ref, w_ref, wdec_ref, bdec_ref,
                   gq_ref, gk_ref, gv_ref, gg_ref, la_ref, sq_ref, sk_ref, sv_ref, ga_ref, gb_ref):
    xb = _rms(x_ref[...], nm_ref[...]).astype(BF16)

    def proj(lo, hi):
        return _dot(xb, w_ref[:, lo:hi])

    gq_ref[...] = proj(_P_GQ, _P_GK)
    gk_ref[...] = proj(_P_GK, _P_GV)
    gv_ref[...] = proj(_P_GV, _P_GG).astype(gv_ref.dtype)
    gg_ref[...] = proj(_P_GG, _P_SQ)
    sq_ref[...] = proj(_P_SQ, _P_SK).astype(sq_ref.dtype)
    sk_ref[...] = proj(_P_SK, _P_SV)
    sv_ref[...] = proj(_P_SV, _P_GA)
    ga_ref[...] = proj(_P_GA, _P_GB)
    gb_ref[...] = proj(_P_GB, _P_LR)
    glr = proj(_P_LR, _P_END)
    z = _dot(glr.astype(BF16), wdec_ref[...]) + bdec_ref[...]
    la_ref[...] = (jnp.minimum(z, 0.0) - jnp.log(1.0 + jnp.exp(-jnp.abs(z)))) * (1.0 / GLA_TAU)


def _inproj(x2d, nm, w_all, wdec, bdec, tn, act_dtype):
    n = x2d.shape[0]
    widths = (GLA_QK, GLA_QK, GLA_V, GLA_V, GLA_QK, SWA_Q, SWA_KV, SWA_KV, D_MODEL, D_MODEL)
    dtypes = (F32, F32, act_dtype, F32, F32, act_dtype, F32, F32, F32, F32)
    const = lambda i: (0, 0)
    return pl.pallas_call(
        _inproj_kernel,
        grid=(n // tn,),
        in_specs=[
            pl.BlockSpec((tn, D_MODEL), lambda i: (i, 0)),
            pl.BlockSpec((1, D_MODEL), const),
            pl.BlockSpec((D_MODEL, _P_END), const, pipeline_mode=pl.Buffered(1)),
            pl.BlockSpec((LANES, GLA_QK), const),
            pl.BlockSpec((1, GLA_QK), const),
        ],
        out_specs=[pl.BlockSpec((tn, w), lambda i: (i, 0)) for w in widths],
        out_shape=[jax.ShapeDtypeStruct((n, w), d) for w, d in zip(widths, dtypes)],
        compiler_params=pltpu.CompilerParams(
            dimension_semantics=("parallel",), vmem_limit_bytes=VMEM_LIMIT),
        name="in_proj",
    )(x2d, nm, w_all, wdec, bdec)


def _gla_kernel(*refs, bb, n_chunks, t_valid, has_s0):
    if has_s0:
        q_ref, k_ref, v_ref, la_ref, s0_ref, o_ref, so_ref, s_ref = refs
    else:
        q_ref, k_ref, v_ref, la_ref, o_ref, so_ref, s_ref = refs
    j = pl.program_id(1)
    c = GLA_CHUNK

    @pl.when(j == 0)
    def _():
        if has_s0:
            s_ref[...] = s0_ref[...]
        else:
            s_ref[...] = jnp.zeros_like(s_ref)

    row = lax.broadcasted_iota(jnp.int32, (c, c), 0)
    col = lax.broadcasted_iota(jnp.int32, (c, c), 1)
    causal = row >= col
    tril = causal.astype(BF16)
    scale = GLA_DK ** -0.5

    def load(ref, bi, r0, lo, w):
        if t_valid >= c:
            return ref[bi, r0:r0 + c, lo:lo + w]
        t = ref[bi, :, lo:lo + w]
        return jnp.concatenate([t, jnp.zeros((c - t_valid, w), t.dtype)], axis=0)

    def pairwise_scores(qs, k, b):
        def score_row(i, a_t):
            pick = row == i
            q_i = jnp.sum(jnp.where(pick, qs, 0.0), axis=0, keepdims=True)
            b_i = jnp.sum(jnp.where(pick, b, 0.0), axis=0, keepdims=True)
            rel = jnp.where(row <= i, b_i - b, -jnp.inf)
            s_col = jnp.sum(k * jnp.exp(rel) * q_i, axis=1, keepdims=True)
            return jnp.where(col == i, s_col, a_t)
        return lax.fori_loop(0, c, score_row, jnp.zeros((c, c), F32)).T

    def run(factored):
        pairs = [(bi, h) for bi in range(bb) for h in range(GLA_HEADS)]
        groups = [pairs[g0:g0 + GLA_HEAD_GROUP] for g0 in range(0, len(pairs), GLA_HEAD_GROUP)]
        for ci in range(n_chunks):
            r0 = ci * c
            for hs in groups:
                q = [load(q_ref, bi, r0, h * GLA_DK, GLA_DK) for bi, h in hs]
                k = [load(k_ref, bi, r0, h * GLA_DK, GLA_DK) for bi, h in hs]
                v = [load(v_ref, bi, r0, h * GLA_DV, GLA_DV).astype(BF16) for bi, h in hs]
                la = [load(la_ref, bi, r0, h * GLA_DK, GLA_DK) for bi, h in hs]
                la_hi = [x.astype(BF16) for x in la]
                la_lo = [(x - y.astype(F32)).astype(BF16) for x, y in zip(la, la_hi)]
                b = [_dot(tril, x) + _dot(tril, y) for x, y in zip(la_hi, la_lo)]
                qs = [x * scale for x in q]
                qd = [(x * jnp.exp(y)).astype(BF16) for x, y in zip(qs, b)]
                if factored:
                    kn = [(x * jnp.exp(-y)).astype(BF16) for x, y in zip(k, b)]
                    a = [jnp.where(causal, _dot_nt(x, y), 0.0) for x, y in zip(qd, kn)]
                else:
                    a = [pairwise_scores(x, y, z) for x, y, z in zip(qs, k, b)]
                s_old = [s_ref[bi, h] for bi, h in hs]
                o = [_dot(x, s.astype(BF16)) + _dot(y.astype(BF16), z) for x, s, y, z in zip(qd, s_old, a, v)]
                for (bi, h), x in zip(hs, o):
                    if t_valid >= c:
                        o_ref[bi, r0:r0 + c, h * GLA_DV:(h + 1) * GLA_DV] = x
                    else:
                        o_ref[bi, :, h * GLA_DV:(h + 1) * GLA_DV] = x[:t_valid]
                bl = [y[c - 1:c, :] for y in b]
                kd_t = [(x * jnp.exp(l - y)).T.astype(BF16) for x, l, y in zip(k, bl, b)]
                dec_t = [jnp.broadcast_to(jnp.exp(l), (c, GLA_DK)).T for l in bl]
                for (bi, h), d, s, x, z in zip(hs, dec_t, s_old, kd_t, v):
                    s_ref[bi, h] = jnp.concatenate([d] * (GLA_DV // c), axis=1) * s + _dot(x, z)

    worst = jnp.float32(0.0)
    for bi in range(bb):
        for ci in range(n_chunks):
            rows = slice(ci * c, (ci + 1) * c) if t_valid >= c else slice(None)
            worst = jnp.maximum(worst, jnp.max(-jnp.sum(la_ref[bi, rows, :], axis=0, keepdims=True)))
    safe = worst < GLA_SAFE_DECAY
    pl.when(safe)(lambda: run(True))
    pl.when(jnp.logical_not(safe))(lambda: run(False))

    @pl.when(j == pl.num_programs(1) - 1)
    def _():
        so_ref[...] = s_ref[...]


def _gla(gq, gk, gv, la, s0, tb, bb):
    bsz, t, _ = gq.shape
    t_valid = min(tb, GLA_CHUNK)
    n_chunks = max(tb // GLA_CHUNK, 1)
    has_s0 = s0 is not None
    tok = lambda w: pl.BlockSpec((bb, tb, w), lambda b, j: (b, j, 0))
    st = pl.BlockSpec((bb, GLA_HEADS, GLA_DK, GLA_DV), lambda b, j: (b, 0, 0, 0))
    in_specs = [tok(GLA_QK), tok(GLA_QK), tok(GLA_V), tok(GLA_QK)] + ([st] if has_s0 else [])
    args = (gq, gk, gv, la) + ((s0,) if has_s0 else ())
    return pl.pallas_call(
        functools.partial(_gla_kernel, bb=bb, n_chunks=n_chunks, t_valid=t_valid, has_s0=has_s0),
        grid=(bsz // bb, t // tb),
        in_specs=in_specs,
        out_specs=[tok(GLA_V), st],
        out_shape=[jax.ShapeDtypeStruct((bsz, t, GLA_V), F32),
                   jax.ShapeDtypeStruct((bsz, GLA_HEADS, GLA_DK, GLA_DV), F32)],
        scratch_shapes=[pltpu.VMEM((bb, GLA_HEADS, GLA_DK, GLA_DV), F32)],
        compiler_params=pltpu.CompilerParams(
            dimension_semantics=("parallel", "arbitrary"), vmem_limit_bytes=VMEM_LIMIT),
        name="gla",
    )(*args)


def _swa_kernel(sink_ref, q_ref, kc_ref, vc_ref, kp_ref, vp_ref, o_ref, s_ref, p_ref, *, bb, tq, nb, prev_is_block):
    i = pl.program_id(1)
    w = WINDOW
    nk = 2 * w
    pairs = SWA_GROUP // 2
    t_idx = lax.broadcasted_iota(jnp.int32, (tq, nk), 0)
    j_idx = lax.broadcasted_iota(jnp.int32, (tq, nk), 1)
    d = j_idx - t_idx
    band = (d >= 1) & (d <= w)
    lane = lax.broadcasted_iota(jnp.int32, (nk, LANES), 1)
    left = lane < SWA_HEAD_DIM

    def window(prev_ref, cur_ref, bi, blk):
        cur = cur_ref[bi, blk * tq:(blk + 1) * tq, :]
        if tq < w:
            cur = jnp.concatenate([cur, jnp.zeros((w - tq, SWA_KV), F32)], axis=0)
        prev = prev_ref[bi] if blk == 0 else cur_ref[bi, (blk - 1) * tq:blk * tq, :]
        full = jnp.concatenate([prev, cur], axis=0)
        return full, pltpu.roll(full, SWA_HEAD_DIM, 1)

    def block_diag(full, rot, kvh):
        a, b = (full, rot) if kvh == 0 else (rot, full)
        return jnp.concatenate([jnp.where(left, a, 0.0), jnp.where(left, 0.0, b)], axis=0).astype(BF16)

    def mask(blk):
        if prev_is_block and blk == 0:
            return band & ((j_idx >= w) | (i > 0))
        return band

    units = [((bi, blk), kvh) for bi in range(bb) for blk in range(nb) for kvh in range(SWA_KV_HEADS)]
    ug = s_ref.shape[0]
    win, vis = {}, {}
    for g0 in range(0, len(units), ug):
        grp = units[g0:g0 + ug]
        for blk, _ in grp:
            if blk not in win:
                win[blk] = (window(kp_ref, kc_ref, *blk), window(vp_ref, vc_ref, *blk))
                vis[blk] = mask(blk[1])
        kbd = [block_diag(*win[blk][0], kvh) for blk, kvh in grp]
        vbd = [block_diag(*win[blk][1], kvh) for blk, kvh in grp]
        base = [kvh * SWA_GROUP * SWA_HEAD_DIM for _, kvh in grp]
        rows = [(blk[0], slice(blk[1] * tq, (blk[1] + 1) * tq)) for blk, _ in grp]
        for gi, ((bi, r), b0) in enumerate(zip(rows, base)):
            qs = jnp.concatenate([q_ref[bi, r, b0 + p * LANES: b0 + (p + 1) * LANES] for p in range(pairs)], axis=0)
            s_ref[gi] = _dot_nt(qs.astype(BF16), kbd[gi]) * (SWA_HEAD_DIM ** -0.5)
        for p in range(pairs):
            for e in range(2):
                pr, seg = slice(p * tq, (p + 1) * tq), slice(e * nk, (e + 1) * nk)
                se = [jnp.where(vis[blk], s_ref[gi, pr, seg], NEG_BIG) for gi, (blk, _) in enumerate(grp)]
                sink = [sink_ref[kvh * SWA_GROUP + 2 * p + e] for _, kvh in grp]
                m = [jnp.maximum(jnp.max(x, axis=-1, keepdims=True), y) for x, y in zip(se, sink)]
                ex = [jnp.exp(x - y) for x, y in zip(se, m)]
                den = [jnp.sum(x, axis=-1, keepdims=True) + jnp.exp(y - z) for x, y, z in zip(ex, sink, m)]
                for gi, (x, y) in enumerate(zip(ex, den)):
                    p_ref[gi, pr, seg] = (x / y).astype(BF16)
        for gi, ((bi, r), b0) in enumerate(zip(rows, base)):
            o = _dot(p_ref[gi], vbd[gi])
            for p in range(pairs):
                o_ref[bi, r, b0 + p * LANES: b0 + (p + 1) * LANES] = o[p * tq:(p + 1) * tq].astype(o_ref.dtype)


def _swa(sq, sk, sv, sinks, prev_k=None, prev_v=None, nb=1, bb=1):
    bsz, t, _ = sq.shape
    prev_is_block = prev_k is None
    if prev_is_block:
        tq = WINDOW
        prev_k, prev_v = sk, sv
        prev_spec = pl.BlockSpec((bb, WINDOW, SWA_KV), lambda b, i: (b, jnp.maximum(i * nb - 1, 0), 0))
    else:
        tq = t
        assert nb == 1
        prev_spec = pl.BlockSpec((bb, WINDOW, SWA_KV), lambda b, i: (b, 0, 0))
    ug = SWA_KV_HEADS
    rows = (SWA_GROUP // 2) * tq
    cur_spec = pl.BlockSpec((bb, nb * tq, SWA_KV), lambda b, i: (b, i, 0))
    q_spec = pl.BlockSpec((bb, nb * tq, SWA_Q), lambda b, i: (b, i, 0))
    return pl.pallas_call(
        functools.partial(_swa_kernel, bb=bb, tq=tq, nb=nb, prev_is_block=prev_is_block),
        grid=(bsz // bb, t // (nb * tq)),
        in_specs=[pl.BlockSpec(memory_space=pltpu.SMEM), q_spec, cur_spec, cur_spec, prev_spec, prev_spec],
        out_specs=q_spec,
        out_shape=jax.ShapeDtypeStruct((bsz, t, SWA_Q), sq.dtype),
        scratch_shapes=[pltpu.VMEM((ug, rows, 4 * WINDOW), F32), pltpu.VMEM((ug, rows, 4 * WINDOW), BF16)],
        compiler_params=pltpu.CompilerParams(
            dimension_semantics=("parallel", "arbitrary"), vmem_limit_bytes=VMEM_LIMIT),
        name="swa",
    )(sinks, sq, sk, sv, prev_k, prev_v)


def _merge_kernel(x_ref, oa_ref, gg_ref, ob_ref, ga_ref, gb_ref, gn_ref, wa_ref, wb_ref, wo_ref, h_ref):
    parts = []
    for hd in range(GLA_HEADS):
        sl = slice(hd * GLA_DV, (hd + 1) * GLA_DV)
        g = gg_ref[:, sl]
        parts.append(_rms(oa_ref[:, sl], gn_ref[...]) * (g * _sigmoid(g)))
    oa = jnp.concatenate(parts, axis=1).astype(BF16)
    br_a = _dot(oa, wa_ref[...])
    br_b = _dot(ob_ref[...].astype(BF16), wb_ref[...])
    merged = _sigmoid(ga_ref[...]) * br_a + _sigmoid(gb_ref[...]) * br_b
    h_ref[...] = x_ref[...] + _dot(merged.astype(BF16), wo_ref[...])


def _merge(x2d, oa, gg, ob, ga, gb, gn, wa, wb, wo, tn):
    n = x2d.shape[0]
    tok = pl.BlockSpec((tn, D_MODEL), lambda i: (i, 0))
    const = lambda i: (0, 0)
    wspec = pl.BlockSpec((D_MODEL, D_MODEL), const)
    return pl.pallas_call(
        _merge_kernel,
        grid=(n // tn,),
        in_specs=[tok] * 6 + [pl.BlockSpec((1, GLA_DV), const), wspec, wspec, wspec],
        out_specs=tok,
        out_shape=jax.ShapeDtypeStruct((n, D_MODEL), F32),
        compiler_params=pltpu.CompilerParams(
            dimension_semantics=("parallel",), vmem_limit_bytes=VMEM_LIMIT),
        name="merge",
    )(x2d, oa, gg, ob, ga, gb, gn, wa, wb, wo)


def _sort16_pairs():
    pairs = []
    n = 16
    p = 1
    while p < n:
        k = p
        while k >= 1:
            for j in range(k % p, n - k, 2 * k):
                for i in range(min(k, n - j - k)):
                    if (i + j) // (2 * p) == (i + j + k) // (2 * p):
                        pairs.append((i + j, i + j + k))
            k //= 2
        p *= 2
    return pairs


_SORT16 = _sort16_pairs()


def _bitonic_top16(xa, xb):
    z = [jnp.maximum(xa[i], xb[15 - i]) for i in range(16)]
    dist = 8
    while dist >= 1:
        for i in range(16):
            if i & dist == 0:
                hi = jnp.maximum(z[i], z[i + dist])
                lo = jnp.minimum(z[i], z[i + dist])
                z[i], z[i + dist] = hi, lo
        dist //= 2
    return z


def _top16_rows(s):
    x = [s[SUBLANES * v:SUBLANES * (v + 1), :] for v in range(16)]
    for i, j in _SORT16:
        hi = jnp.maximum(x[i], x[j])
        lo = jnp.minimum(x[i], x[j])
        x[i], x[j] = hi, lo
    shift = 1
    while shift < SUBLANES:
        x = _bitonic_top16(x, [pltpu.roll(t, shift, 0) for t in x])
        shift *= 2
    return [t[SUBLANES - 1:SUBLANES, :] for t in x]


_CAND_Q = [PEER_TOPK // (p + 1) for p in range(PEER_TOPK)]


PACK = 2 * SUBLANES
_HS_TAU, _HS_ZINV, _HS_TOP1, _HS_ROWS = PEER_TOPK, PEER_TOPK + 1, PEER_TOPK + 2, PEER_TOPK + SUBLANES


def _peer_kernel(h_ref, nf_ref, nfin_ref, wq_ref, k1_ref, k2_ref, u_ref, vt_ref, y_ref,
                 hn_ref, sa_ref, sb_ref, e1_ref, e2_ref, v1_ref, v2_ref, hs_ref,
                 ca_ref, ea_ref, ht_ref, g_ref, acc_ref, *, tn, ec):
    e = pl.program_id(1)
    na = ec // PEER_NKEYS
    nt = tn // LANES
    rb = 4 * SUBLANES

    @pl.when(e == 0)
    def _routing():
        hn_ref[...] = _rms(h_ref[...], nf_ref[...]).astype(BF16)
        acc_ref[...] = jnp.zeros_like(acc_ref)

        for hd in range(PEER_HEADS):
            wq = wq_ref[hd * PEER_QDIM:(hd + 1) * PEER_QDIM, :]
            qt = _dot_nt(wq, hn_ref[...])
            s1 = _dot(k1_ref[hd], qt[:PEER_HALF].astype(BF16))
            s2 = _dot(k2_ref[hd], qt[PEER_HALF:].astype(BF16))
            sa_ref[hd] = s1
            sb_ref[hd] = s2
            for p, r in enumerate(_top16_rows(s1)):
                v1_ref[p, hd:hd + 1, :] = r
            for p, r in enumerate(_top16_rows(s2)):
                v2_ref[p, hd:hd + 1, :] = r
                hs_ref[hd, p:p + 1, :] = r

        v1 = [v1_ref[p] for p in range(PEER_TOPK)]
        v2 = [v2_ref[q] for q in range(PEER_TOPK)]
        ninf = jnp.full((PEER_HEADS, tn), -jnp.inf, F32)
        best = [v1[0] + v2[q] for q in range(PEER_TOPK)]
        for p in range(1, PEER_TOPK):
            row = [v1[p] + v2[q] if q < _CAND_Q[p] else ninf for q in range(PEER_TOPK)]
            best = _bitonic_top16(best, row)
        tau = best[PEER_TOPK - 1]
        top = best[0]
        zsum = jnp.zeros((PEER_HEADS, tn), F32)
        for p in range(PEER_TOPK):
            for q in range(_CAND_Q[p]):
                cnd = v1[p] + v2[q]
                zsum = zsum + jnp.where(cnd >= tau, jnp.exp(cnd - top), 0.0)
        zinv = 1.0 / zsum
        for hd in range(PEER_HEADS):
            hs_ref[hd, _HS_TAU:_HS_TAU + 1, :] = tau[hd:hd + 1]
            hs_ref[hd, _HS_ZINV:_HS_ZINV + 1, :] = zinv[hd:hd + 1]
            hs_ref[hd, _HS_TOP1:_HS_TOP1 + 1, :] = v1[0][hd:hd + 1]

        def staircase(idx, carry):
            hd = idx // (PEER_NKEYS // rb)
            rows = pl.ds(pl.multiple_of((idx % (PEER_NKEYS // rb)) * rb, rb), rb)
            s1 = sa_ref[hd, rows, :]
            tau_h = hs_ref[hd, _HS_TAU:_HS_TAU + 1, :]
            theta = jnp.full((rb, tn), jnp.inf, F32)
            for q in range(PEER_TOPK):
                v2q = hs_ref[hd, q:q + 1, :]
                theta = jnp.where(s1 + v2q >= tau_h, v2q, theta)
            e2_ref[hd, rows, :] = (jnp.exp(sb_ref[hd, rows, :] - hs_ref[hd, 0:1, :])
                                   * hs_ref[hd, _HS_ZINV:_HS_ZINV + 1, :])
            e1_ref[hd, rows, :] = 0.5 * jnp.exp(s1 - hs_ref[hd, _HS_TOP1:_HS_TOP1 + 1, :])
            sa_ref[hd, rows, :] = theta
            return carry

        lax.fori_loop(0, PEER_HEADS * (PEER_NKEYS // rb), staircase, 0)

    ht_ref[...] = _dot_nt(u_ref[...], hn_ref[...])
    a_rows = pl.ds(pl.multiple_of(e * na, na), na)
    for hd in range(PEER_HEADS):
        ca_ref[hd] = sa_ref[hd, a_rows, :]
        ea_ref[hd] = e1_ref[hd, a_rows, :]

    ga = 4
    gb = 4
    n_bg = PEER_NKEYS // (gb * SUBLANES)

    def gate(idx, carry):
        cols = pl.ds(pl.multiple_of((idx // n_bg) * LANES, LANES), LANES)
        b0 = pl.multiple_of((idx % n_bg) * (gb * SUBLANES), gb * SUBLANES)
        bc = lambda ref, hd, r: jnp.broadcast_to(ref[hd, r:r + 1, cols], (SUBLANES, LANES))
        for a_blk in range(0, na, ga):
            w = [[jnp.zeros((SUBLANES, LANES), F32) for _ in range(gb)] for _ in range(ga)]
            for hd in range(PEER_HEADS):
                s2 = [sb_ref[hd, pl.ds(b0 + t * SUBLANES, SUBLANES), cols] for t in range(gb)]
                e2 = [e2_ref[hd, pl.ds(b0 + t * SUBLANES, SUBLANES), cols] for t in range(gb)]
                for i in range(ga):
                    theta_b = bc(ca_ref, hd, a_blk + i)
                    e1_b = bc(ea_ref, hd, a_blk + i)
                    for t in range(gb):
                        w[i][t] = w[i][t] + jnp.where(s2[t] >= theta_b, e2[t], 0.0) * e1_b
            for i in range(ga):
                for t in range(0, gb, 2):
                    rows = pl.ds(pl.multiple_of((a_blk + i) * PEER_NKEYS + b0 + t * SUBLANES, PACK), PACK)
                    hx = ht_ref[rows, cols]
                    act = hx * (1.0 + lax.erf(hx * (2.0 ** -0.5)))
                    g_ref[rows, cols] = (jnp.concatenate([w[i][t], w[i][t + 1]], axis=0) * act).astype(BF16)
        return carry

    lax.fori_loop(0, nt * n_bg, gate, 0)
    acc_ref[...] += _dot(vt_ref[...], g_ref[...])

    @pl.when(e == pl.num_programs(1) - 1)
    def _finish():
        y_ref[...] = _rms(h_ref[...] + acc_ref[...].T, nfin_ref[...])


def _peer(h2d, nf, nfin, wq_t, k1, k2, u, vt, tn, ec):
    n = h2d.shape[0]
    assert ec % (SUBLANES * PEER_NKEYS) == 0 and tn % LANES == 0
    tok = pl.BlockSpec((tn, D_MODEL), lambda i, e: (i, 0))
    const2 = lambda i, e: (0, 0)
    const3 = lambda i, e: (0, 0, 0)
    vec = pl.BlockSpec((1, D_MODEL), const2)
    kspec = pl.BlockSpec((PEER_HEADS, PEER_NKEYS, PEER_HALF), const3)
    head_f32 = pltpu.VMEM((PEER_HEADS, PEER_NKEYS, tn), F32)
    rank_tok = pltpu.VMEM((PEER_TOPK, PEER_HEADS, tn), F32)
    stage = pltpu.VMEM((PEER_HEADS, ec // PEER_NKEYS, tn), F32)
    return pl.pallas_call(
        functools.partial(_peer_kernel, tn=tn, ec=ec),
        grid=(n // tn, PEER_N_EXPERTS // ec),
        in_specs=[tok, vec, vec,
                  pl.BlockSpec((PEER_HEADS * PEER_QDIM, D_MODEL), const2, pipeline_mode=pl.Buffered(1)),
                  kspec, kspec,
                  pl.BlockSpec((ec, D_MODEL), lambda i, e: (e, 0)),
                  pl.BlockSpec((D_MODEL, ec), lambda i, e: (0, e))],
        out_specs=tok,
        out_shape=jax.ShapeDtypeStruct((n, D_MODEL), F32),
        scratch_shapes=[pltpu.VMEM((tn, D_MODEL), BF16),
                        head_f32, head_f32, head_f32, head_f32, rank_tok, rank_tok,
                        pltpu.VMEM((PEER_HEADS, _HS_ROWS, tn), F32),
                        stage, stage, pltpu.VMEM((ec, tn), F32), pltpu.VMEM((ec, tn), BF16),
                        pltpu.VMEM((D_MODEL, tn), F32)],
        compiler_params=pltpu.CompilerParams(
            dimension_semantics=("parallel", "arbitrary"), vmem_limit_bytes=VMEM_LIMIT),
        name="peer",
    )(h2d, nf, nfin, wq_t, k1, k2, u, vt)


def _pack_params(norm_mix, w_in, gla_w_decay, gla_b_decay, gla_norm, swa_sinks, w_branch_a, w_branch_b,
                 w_out, norm_ffn, peer_w_q, peer_keys1, peer_keys2, peer_u, peer_v):
    off = np.cumsum((0, GLA_QK, GLA_QK, GLA_V, GLA_V, GLA_RANK, SWA_Q, SWA_KV, SWA_KV, D_MODEL, D_MODEL))
    col = lambda i: w_in[:, off[i]:off[i + 1]]
    lr = jnp.pad(col(4), ((0, 0), (0, LANES - GLA_RANK)))
    w_all = jnp.concatenate([col(0), col(1), col(2), col(3), col(5), col(6), col(7), col(8), col(9), lr],
                            axis=1).astype(BF16)
    wdec = jnp.pad(gla_w_decay, ((0, LANES - GLA_RANK), (0, 0))).astype(BF16)
    return dict(
        nm=norm_mix[None], w_all=w_all, wdec=wdec, bdec=gla_b_decay[None], gn=gla_norm[None],
        sinks=swa_sinks, wa=w_branch_a.astype(BF16), wb=w_branch_b.astype(BF16), wo=w_out.astype(BF16),
        nf=norm_ffn[None], wq_t=peer_w_q.T.astype(BF16), k1=peer_keys1.astype(BF16),
        k2=peer_keys2.astype(BF16), u=peer_u.astype(BF16), vt=peer_v.T.astype(BF16))


def _layer(x, p, nfin, s0, win_k, win_v, *, tn, act_dtype, gla_tb, gla_bb, swa_nb, swa_bb, peer_tn, peer_ec):
    bsz, t, _ = x.shape
    x2d = x.reshape(bsz * t, D_MODEL)
    gq, gk, gv, gg, la, sq, sk, sv, ga, gb = _inproj(x2d, p["nm"], p["w_all"], p["wdec"], p["bdec"], tn, act_dtype)
    r3 = lambda a: a.reshape(bsz, t, a.shape[-1])
    o_a, s_new = _gla(r3(gq), r3(gk), r3(gv), r3(la), s0, gla_tb, gla_bb)
    if win_k is None:
        o_b = _swa(r3(sq), r3(sk), r3(sv), p["sinks"], nb=swa_nb, bb=swa_bb)
        ln = min(WINDOW, t)
        new_k, new_v = r3(sk)[:, t - ln:], r3(sv)[:, t - ln:]
    else:
        wk = win_k.reshape(bsz, WINDOW, SWA_KV)
        wv = win_v.reshape(bsz, WINDOW, SWA_KV)
        o_b = _swa(r3(sq), r3(sk), r3(sv), p["sinks"], wk, wv, bb=swa_bb)
        new_k = jnp.concatenate([wk, r3(sk)], axis=1)[:, -WINDOW:]
        new_v = jnp.concatenate([wv, r3(sv)], axis=1)[:, -WINDOW:]
    h = _merge(x2d, o_a.reshape(bsz * t, GLA_V), gg, o_b.reshape(bsz * t, SWA_Q), ga, gb,
               p["gn"], p["wa"], p["wb"], p["wo"], tn)
    y = _peer(h, p["nf"], nfin, p["wq_t"], p["k1"], p["k2"], p["u"], p["vt"], peer_tn, peer_ec)
    kv_shape = (bsz, -1, SWA_KV_HEADS, SWA_HEAD_DIM)
    return y.reshape(bsz, t, D_MODEL), new_k.reshape(kv_shape), new_v.reshape(kv_shape), s_new


def kernel(x_prompt, x_sample, cache_win_k, cache_win_v, state_gla, norm_mix, w_in, gla_w_decay, gla_b_decay,
           gla_norm, swa_sinks, w_branch_a, w_branch_b, w_out, norm_ffn, peer_w_q, peer_keys1, peer_keys2,
           peer_u, peer_v, norm_final):
    depth = w_in.shape[0]
    assert depth == 1, "the final norm is fused into the last layer's channel mixer; one layer supported"
    p = _pack_params(norm_mix[0], w_in[0], gla_w_decay[0], gla_b_decay[0], gla_norm[0], swa_sinks[0],
                     w_branch_a[0], w_branch_b[0], w_out[0], norm_ffn[0], peer_w_q[0], peer_keys1[0],
                     peer_keys2[0], peer_u[0], peer_v[0])
    nfin = norm_final[None]
    yp, pk, pv, ps = _layer(x_prompt, p, nfin, None, None, None,
                            tn=512, act_dtype=BF16, gla_tb=512, gla_bb=1, swa_nb=4, swa_bb=1, peer_tn=512, peer_ec=2048)
    ys, sk, sv, ss = _layer(x_sample, p, nfin, state_gla[0], cache_win_k[0], cache_win_v[0],
                            tn=512, act_dtype=F32, gla_tb=x_sample.shape[1], gla_bb=4, swa_nb=1, swa_bb=4, peer_tn=512, peer_ec=2048)
    return (yp, ys, pk[None], pv[None], ps[None], sk[None], sv[None], ss[None])
```

```python
import functools

import jax
import jax.numpy as jnp
import numpy as np
from jax import lax
from jax.experimental import pallas as pl
from jax.experimental.pallas import tpu as pltpu

F32 = jnp.float32
BF16 = jnp.bfloat16

D_MODEL = 1024
GLA_HEADS = 4
GLA_DK = 128
GLA_DV = 256
GLA_RANK = 16
GLA_TAU = 16.0
SWA_HEADS = 16
SWA_KV_HEADS = 2
SWA_GROUP = SWA_HEADS // SWA_KV_HEADS
SWA_HEAD_DIM = 64
WINDOW = 128
PEER_HEADS = 8
PEER_NKEYS = 128
PEER_N_EXPERTS = PEER_NKEYS * PEER_NKEYS
PEER_QDIM = 256
PEER_HALF = PEER_QDIM // 2
PEER_TOPK = 16
EPS = 1e-6

GLA_QK = GLA_HEADS * GLA_DK
GLA_V = GLA_HEADS * GLA_DV
SWA_Q = SWA_HEADS * SWA_HEAD_DIM
SWA_KV = SWA_KV_HEADS * SWA_HEAD_DIM

LANES = 128
SUBLANES = 8
GLA_CHUNK = 128
GLA_HEAD_GROUP = 4
GLA_SAFE_DECAY = 60.0
VMEM_LIMIT = 56 * 1024 * 1024
NEG_BIG = -1e30

_NT = (((1,), (1,)), ((), ()))


def _rms(x, g):
    return x * lax.rsqrt(jnp.mean(x * x, axis=-1, keepdims=True) + EPS) * g


def _dot(a, b):
    return jnp.dot(a, b, preferred_element_type=F32)


def _dot_nt(a, b):
    return lax.dot_general(a, b, _NT, preferred_element_type=F32)


def _sigmoid(x):
    return 1.0 / (1.0 + jnp.exp(-x))


_P_GQ, _P_GK, _P_GV, _P_GG, _P_SQ, _P_SK, _P_SV, _P_GA, _P_GB, _P_LR, _P_END = (
    0, 512, 1024, 2048, 3072, 4096, 4224, 4352, 5376, 6400, 6528)


def _inproj_kernel(x_ref, nm_ref, w_ref, wdec_ref, bdec_ref,
                   gq_ref, gk_ref, gv_ref, gg_ref, la_ref, sq_ref, sk_ref, sv_ref, ga_ref, gb_ref):
    xb = _rms(x_ref[...], nm_ref[...]).astype(BF16)

    def proj(lo, hi):
        return _dot(xb, w_ref[:, lo:hi])

    gq_ref[...] = proj(_P_GQ, _P_GK)
    gk_ref[...] = proj(_P_GK, _P_GV)
    gv_ref[...] = proj(_P_GV, _P_GG).astype(gv_ref.dtype)
    gg_ref[...] = proj(_P_GG, _P_SQ).astype(gg_ref.dtype)
    sq_ref[...] = proj(_P_SQ, _P_SK).astype(sq_ref.dtype)
    sk_ref[...] = proj(_P_SK, _P_SV)
    sv_ref[...] = proj(_P_SV, _P_GA)
    ga_ref[...] = proj(_P_GA, _P_GB).astype(ga_ref.dtype)
    gb_ref[...] = proj(_P_GB, _P_LR).astype(gb_ref.dtype)
    glr = proj(_P_LR, _P_END)
    z = _dot(glr.astype(BF16), wdec_ref[...]) + bdec_ref[...]
    la_ref[...] = (jnp.minimum(z, 0.0) - jnp.log(1.0 + jnp.exp(-jnp.abs(z)))) * (1.0 / GLA_TAU)


def _inproj(x2d, nm, w_all, wdec, bdec, tn, act_dtype):
    n = x2d.shape[0]
    widths = (GLA_QK, GLA_QK, GLA_V, GLA_V, GLA_QK, SWA_Q, SWA_KV, SWA_KV, D_MODEL, D_MODEL)
    dtypes = (F32, F32, act_dtype, act_dtype, F32, act_dtype, F32, F32, act_dtype, act_dtype)
    const = lambda i: (0, 0)
    return pl.pallas_call(
        _inproj_kernel,
        grid=(n // tn,),
        in_specs=[
            pl.BlockSpec((tn, D_MODEL), lambda i: (i, 0)),
            pl.BlockSpec((1, D_MODEL), const),
            pl.BlockSpec((D_MODEL, _P_END), const, pipeline_mode=pl.Buffered(1)),
            pl.BlockSpec((LANES, GLA_QK), const),
            pl.BlockSpec((1, GLA_QK), const),
        ],
        out_specs=[pl.BlockSpec((tn, w), lambda i: (i, 0)) for w in widths],
        out_shape=[jax.ShapeDtypeStruct((n, w), d) for w, d in zip(widths, dtypes)],
        compiler_params=pltpu.CompilerParams(
            dimension_semantics=("parallel",), vmem_limit_bytes=VMEM_LIMIT),
        name="in_proj",
    )(x2d, nm, w_all, wdec, bdec)


def _gla_kernel(*refs, bb, n_chunks, t_valid, has_s0):
    if has_s0:
        q_ref, k_ref, v_ref, la_ref, s0_ref, o_ref, so_ref, s_ref = refs
    else:
        q_ref, k_ref, v_ref, la_ref, o_ref, so_ref, s_ref = refs
    j = pl.program_id(1)
    c = GLA_CHUNK

    @pl.when(j == 0)
    def _():
        if has_s0:
            s_ref[...] = s0_ref[...]
        else:
            s_ref[...] = jnp.zeros_like(s_ref)

    row = lax.broadcasted_iota(jnp.int32, (c, c), 0)
    col = lax.broadcasted_iota(jnp.int32, (c, c), 1)
    causal = row >= col
    tril = causal.astype(BF16)
    scale = GLA_DK ** -0.5

    def load(ref, bi, r0, lo, w):
        if t_valid >= c:
            return ref[bi, r0:r0 + c, lo:lo + w]
        t = ref[bi, :, lo:lo + w]
        return jnp.concatenate([t, jnp.zeros((c - t_valid, w), t.dtype)], axis=0)

    def pairwise_scores(qs, k, b):
        def score_row(i, a_t):
            pick = row == i
            q_i = jnp.sum(jnp.where(pick, qs, 0.0), axis=0, keepdims=True)
            b_i = jnp.sum(jnp.where(pick, b, 0.0), axis=0, keepdims=True)
            rel = jnp.where(row <= i, b_i - b, -jnp.inf)
            s_col = jnp.sum(k * jnp.exp(rel) * q_i, axis=1, keepdims=True)
            return jnp.where(col == i, s_col, a_t)
        return lax.fori_loop(0, c, score_row, jnp.zeros((c, c), F32)).T

    def run(factored):
        pairs = [(bi, h) for bi in range(bb) for h in range(GLA_HEADS)]
        groups = [pairs[g0:g0 + GLA_HEAD_GROUP] for g0 in range(0, len(pairs), GLA_HEAD_GROUP)]
        for ci in range(n_chunks):
            r0 = ci * c
            for hs in groups:
                q = [load(q_ref, bi, r0, h * GLA_DK, GLA_DK) for bi, h in hs]
                k = [load(k_ref, bi, r0, h * GLA_DK, GLA_DK) for bi, h in hs]
                v = [load(v_ref, bi, r0, h * GLA_DV, GLA_DV).astype(BF16) for bi, h in hs]
                la = [load(la_ref, bi, r0, h * GLA_DK, GLA_DK) for bi, h in hs]
                la_hi = [x.astype(BF16) for x in la]
                la_lo = [(x - y.astype(F32)).astype(BF16) for x, y in zip(la, la_hi)]
                b = [_dot(tril, x) + _dot(tril, y) for x, y in zip(la_hi, la_lo)]
                qs = [x * scale for x in q]
                qd = [(x * jnp.exp(y)).astype(BF16) for x, y in zip(qs, b)]
                if factored:
                    kn = [(x * jnp.exp(-y)).astype(BF16) for x, y in zip(k, b)]
                    a = [jnp.where(causal, _dot_nt(x, y), 0.0) for x, y in zip(qd, kn)]
                else:
                    a = [pairwise_scores(x, y, z) for x, y, z in zip(qs, k, b)]
                s_old = [s_ref[bi, h] for bi, h in hs]
                o = [_dot(x, s.astype(BF16)) + _dot(y.astype(BF16), z) for x, s, y, z in zip(qd, s_old, a, v)]
                for (bi, h), x in zip(hs, o):
                    if t_valid >= c:
                        o_ref[bi, r0:r0 + c, h * GLA_DV:(h + 1) * GLA_DV] = x.astype(o_ref.dtype)
                    else:
                        o_ref[bi, :, h * GLA_DV:(h + 1) * GLA_DV] = x[:t_valid].astype(o_ref.dtype)
                bl = [y[c - 1:c, :] for y in b]
                kd_t = [(x * jnp.exp(l - y)).T.astype(BF16) for x, l, y in zip(k, bl, b)]
                dec_t = [jnp.broadcast_to(jnp.exp(l), (c, GLA_DK)).T for l in bl]
                for (bi, h), d, s, x, z in zip(hs, dec_t, s_old, kd_t, v):
                    s_ref[bi, h] = jnp.concatenate([d] * (GLA_DV // c), axis=1) * s + _dot(x, z)

    worst = jnp.float32(0.0)
    for bi in range(bb):
        for ci in range(n_chunks):
            rows = slice(ci * c, (ci + 1) * c) if t_valid >= c else slice(None)
            worst = jnp.maximum(worst, jnp.max(-jnp.sum(la_ref[bi, rows, :], axis=0, keepdims=True)))
    safe = worst < GLA_SAFE_DECAY
    pl.when(safe)(lambda: run(True))
    pl.when(jnp.logical_not(safe))(lambda: run(False))

    @pl.when(j == pl.num_programs(1) - 1)
    def _():
        so_ref[...] = s_ref[...]


def _gla(gq, gk, gv, la, s0, tb, bb):
    bsz, t, _ = gq.shape
    t_valid = min(tb, GLA_CHUNK)
    n_chunks = max(tb // GLA_CHUNK, 1)
    has_s0 = s0 is not None
    tok = lambda w: pl.BlockSpec((bb, tb, w), lambda b, j: (b, j, 0))
    st = pl.BlockSpec((bb, GLA_HEADS, GLA_DK, GLA_DV), lambda b, j: (b, 0, 0, 0))
    in_specs = [tok(GLA_QK), tok(GLA_QK), tok(GLA_V), tok(GLA_QK)] + ([st] if has_s0 else [])
    args = (gq, gk, gv, la) + ((s0,) if has_s0 else ())
    return pl.pallas_call(
        functools.partial(_gla_kernel, bb=bb, n_chunks=n_chunks, t_valid=t_valid, has_s0=has_s0),
        grid=(bsz // bb, t // tb),
        in_specs=in_specs,
        out_specs=[tok(GLA_V), st],
        out_shape=[jax.ShapeDtypeStruct((bsz, t, GLA_V), gv.dtype),
                   jax.ShapeDtypeStruct((bsz, GLA_HEADS, GLA_DK, GLA_DV), F32)],
        scratch_shapes=[pltpu.VMEM((bb, GLA_HEADS, GLA_DK, GLA_DV), F32)],
        compiler_params=pltpu.CompilerParams(
            dimension_semantics=("parallel", "arbitrary"), vmem_limit_bytes=VMEM_LIMIT),
        name="gla",
    )(*args)


def _swa_kernel(sink_ref, q_ref, kc_ref, vc_ref, kp_ref, vp_ref, o_ref, s_ref, p_ref, *, bb, tq, nb, prev_is_block):
    i = pl.program_id(1)
    w = WINDOW
    nk = 2 * w
    pairs = SWA_GROUP // 2
    t_idx = lax.broadcasted_iota(jnp.int32, (tq, nk), 0)
    j_idx = lax.broadcasted_iota(jnp.int32, (tq, nk), 1)
    d = j_idx - t_idx
    band = (d >= 1) & (d <= w)
    lane = lax.broadcasted_iota(jnp.int32, (nk, LANES), 1)
    left = lane < SWA_HEAD_DIM

    def window(prev_ref, cur_ref, bi, blk):
        cur = cur_ref[bi, blk * tq:(blk + 1) * tq, :]
        if tq < w:
            cur = jnp.concatenate([cur, jnp.zeros((w - tq, SWA_KV), F32)], axis=0)
        prev = prev_ref[bi] if blk == 0 else cur_ref[bi, (blk - 1) * tq:blk * tq, :]
        full = jnp.concatenate([prev, cur], axis=0)
        return full, pltpu.roll(full, SWA_HEAD_DIM, 1)

    def block_diag(full, rot, kvh):
        a, b = (full, rot) if kvh == 0 else (rot, full)
        return jnp.concatenate([jnp.where(left, a, 0.0), jnp.where(left, 0.0, b)], axis=0).astype(BF16)

    def mask(blk):
        if prev_is_block and blk == 0:
            return band & ((j_idx >= w) | (i > 0))
        return band

    units = [((bi, blk), kvh) for bi in range(bb) for blk in range(nb) for kvh in range(SWA_KV_HEADS)]
    ug = s_ref.shape[0]
    win, vis = {}, {}
    for g0 in range(0, len(units), ug):
        grp = units[g0:g0 + ug]
        for blk, _ in grp:
            if blk not in win:
                win[blk] = (window(kp_ref, kc_ref, *blk), window(vp_ref, vc_ref, *blk))
                vis[blk] = mask(blk[1])
        kbd = [block_diag(*win[blk][0], kvh) for blk, kvh in grp]
        vbd = [block_diag(*win[blk][1], kvh) for blk, kvh in grp]
        base = [kvh * SWA_GROUP * SWA_HEAD_DIM for _, kvh in grp]
        rows = [(blk[0], slice(blk[1] * tq, (blk[1] + 1) * tq)) for blk, _ in grp]
        for gi, ((bi, r), b0) in enumerate(zip(rows, base)):
            qs = jnp.concatenate([q_ref[bi, r, b0 + p * LANES: b0 + (p + 1) * LANES] for p in range(pairs)], axis=0)
            s_ref[gi] = _dot_nt(qs.astype(BF16), kbd[gi]) * (SWA_HEAD_DIM ** -0.5)
        for p in range(pairs):
            for e in range(2):
                pr, seg = slice(p * tq, (p + 1) * tq), slice(e * nk, (e + 1) * nk)
                se = [jnp.where(vis[blk], s_ref[gi, pr, seg], NEG_BIG) for gi, (blk, _) in enumerate(grp)]
                sink = [sink_ref[kvh * SWA_GROUP + 2 * p + e] for _, kvh in grp]
                m = [jnp.maximum(jnp.max(x, axis=-1, keepdims=True), y) for x, y in zip(se, sink)]
                ex = [jnp.exp(x - y) for x, y in zip(se, m)]
                den = [jnp.sum(x, axis=-1, keepdims=True) + jnp.exp(y - z) for x, y, z in zip(ex, sink, m)]
                for gi, (x, y) in enumerate(zip(ex, den)):
                    p_ref[gi, pr, seg] = (x / y).astype(BF16)
        for gi, ((bi, r), b0) in enumerate(zip(rows, base)):
            o = _dot(p_ref[gi], vbd[gi])
            for p in range(pairs):
                o_ref[bi, r, b0 + p * LANES: b0 + (p + 1) * LANES] = o[p * tq:(p + 1) * tq].astype(o_ref.dtype)


def _swa(sq, sk, sv, sinks, prev_k=None, prev_v=None, nb=1, bb=1):
    bsz, t, _ = sq.shape
    prev_is_block = prev_k is None
    if prev_is_block:
        tq = WINDOW
        prev_k, prev_v = sk, sv
        prev_spec = pl.BlockSpec((bb, WINDOW, SWA_KV), lambda b, i: (b, jnp.maximum(i * nb - 1, 0), 0))
    else:
        tq = t
        assert nb == 1
        prev_spec = pl.BlockSpec((bb, WINDOW, SWA_KV), lambda b, i: (b, 0, 0))
    ug = SWA_KV_HEADS
    rows = (SWA_GROUP // 2) * tq
    cur_spec = pl.BlockSpec((bb, nb * tq, SWA_KV), lambda b, i: (b, i, 0))
    q_spec = pl.BlockSpec((bb, nb * tq, SWA_Q), lambda b, i: (b, i, 0))
    return pl.pallas_call(
        functools.partial(_swa_kernel, bb=bb, tq=tq, nb=nb, prev_is_block=prev_is_block),
        grid=(bsz // bb, t // (nb * tq)),
        in_specs=[pl.BlockSpec(memory_space=pltpu.SMEM), q_spec, cur_spec, cur_spec, prev_spec, prev_spec],
        out_specs=q_spec,
        out_shape=jax.ShapeDtypeStruct((bsz, t, SWA_Q), sq.dtype),
        scratch_shapes=[pltpu.VMEM((ug, rows, 4 * WINDOW), F32), pltpu.VMEM((ug, rows, 4 * WINDOW), BF16)],
        compiler_params=pltpu.CompilerParams(
            dimension_semantics=("parallel", "arbitrary"), vmem_limit_bytes=VMEM_LIMIT),
        name="swa",
    )(sinks, sq, sk, sv, prev_k, prev_v)


def _merge_kernel(x_ref, oa_ref, gg_ref, ob_ref, ga_ref, gb_ref, gn_ref, wa_ref, wb_ref, wo_ref, h_ref):
    parts = []
    for hd in range(GLA_HEADS):
        sl = slice(hd * GLA_DV, (hd + 1) * GLA_DV)
        g = gg_ref[:, sl].astype(F32)
        parts.append(_rms(oa_ref[:, sl].astype(F32), gn_ref[...]) * (g * _sigmoid(g)))
    oa = jnp.concatenate(parts, axis=1).astype(BF16)
    br_a = _dot(oa, wa_ref[...])
    br_b = _dot(ob_ref[...].astype(BF16), wb_ref[...])
    merged = _sigmoid(ga_ref[...].astype(F32)) * br_a + _sigmoid(gb_ref[...].astype(F32)) * br_b
    h_ref[...] = x_ref[...] + _dot(merged.astype(BF16), wo_ref[...])


def _merge(x2d, oa, gg, ob, ga, gb, gn, wa, wb, wo, tn):
    n = x2d.shape[0]
    tok = pl.BlockSpec((tn, D_MODEL), lambda i: (i, 0))
    const = lambda i: (0, 0)
    wspec = pl.BlockSpec((D_MODEL, D_MODEL), const)
    return pl.pallas_call(
        _merge_kernel,
        grid=(n // tn,),
        in_specs=[tok] * 6 + [pl.BlockSpec((1, GLA_DV), const), wspec, wspec, wspec],
        out_specs=tok,
        out_shape=jax.ShapeDtypeStruct((n, D_MODEL), F32),
        compiler_params=pltpu.CompilerParams(
            dimension_semantics=("parallel",), vmem_limit_bytes=VMEM_LIMIT),
        name="merge",
    )(x2d, oa, gg, ob, ga, gb, gn, wa, wb, wo)


def _sort16_pairs():
    pairs = []
    n = 16
    p = 1
    while p < n:
        k = p
        while k >= 1:
            for j in range(k % p, n - k, 2 * k):
                for i in range(min(k, n - j - k)):
                    if (i + j) // (2 * p) == (i + j + k) // (2 * p):
                        pairs.append((i + j, i + j + k))
            k //= 2
        p *= 2
    return pairs


_SORT16 = _sort16_pairs()


def _bitonic_top16(xa, xb):
    z = [jnp.maximum(xa[i], xb[15 - i]) for i in range(16)]
    dist = 8
    while dist >= 1:
        for i in range(16):
            if i & dist == 0:
                hi = jnp.maximum(z[i], z[i + dist])
                lo = jnp.minimum(z[i], z[i + dist])
                z[i], z[i + dist] = hi, lo
        dist //= 2
    return z


def _top16_rows(s):
    x = [s[SUBLANES * v:SUBLANES * (v + 1), :] for v in range(16)]
    for i, j in _SORT16:
        hi = jnp.maximum(x[i], x[j])
        lo = jnp.minimum(x[i], x[j])
        x[i], x[j] = hi, lo
    shift = 1
    while shift < SUBLANES:
        x = _bitonic_top16(x, [pltpu.roll(t, shift, 0) for t in x])
        shift *= 2
    return [t[SUBLANES - 1:SUBLANES, :] for t in x]


_CAND_Q = [PEER_TOPK // (p + 1) for p in range(PEER_TOPK)]


PACK = 2 * SUBLANES
_HS_TAU, _HS_ZINV, _HS_TOP1, _HS_ROWS = PEER_TOPK, PEER_TOPK + 1, PEER_TOPK + 2, PEER_TOPK + SUBLANES


def _peer_kernel(h_ref, nf_ref, nfin_ref, wq_ref, k1_ref, k2_ref, u_ref, vt_ref, y_ref,
                 hn_ref, sa_ref, sb_ref, e1_ref, e2_ref, v1_ref, v2_ref, hs_ref,
                 ca_ref, ea_ref, ht_ref, g_ref, acc_ref, *, tn, ec):
    e = pl.program_id(1)
    na = ec // PEER_NKEYS
    nt = tn // LANES
    rb = 4 * SUBLANES

    @pl.when(e == 0)
    def _routing():
        hn_ref[...] = _rms(h_ref[...], nf_ref[...]).astype(BF16)
        acc_ref[...] = jnp.zeros_like(acc_ref)

        for hd in range(PEER_HEADS):
            wq = wq_ref[hd * PEER_QDIM:(hd + 1) * PEER_QDIM, :]
            qt = _dot_nt(wq, hn_ref[...])
            s1 = _dot(k1_ref[hd], qt[:PEER_HALF].astype(BF16))
            s2 = _dot(k2_ref[hd], qt[PEER_HALF:].astype(BF16))
            sa_ref[hd] = s1
            sb_ref[hd] = s2
            for p, r in enumerate(_top16_rows(s1)):
                v1_ref[p, hd:hd + 1, :] = r
            for p, r in enumerate(_top16_rows(s2)):
                v2_ref[p, hd:hd + 1, :] = r
                hs_ref[hd, p:p + 1, :] = r

        v1 = [v1_ref[p] for p in range(PEER_TOPK)]
        v2 = [v2_ref[q] for q in range(PEER_TOPK)]
        ninf = jnp.full((PEER_HEADS, tn), -jnp.inf, F32)
        best = [v1[0] + v2[q] for q in range(PEER_TOPK)]
        for p in range(1, PEER_TOPK):
            row = [v1[p] + v2[q] if q < _CAND_Q[p] else ninf for q in range(PEER_TOPK)]
            best = _bitonic_top16(best, row)
        tau = best[PEER_TOPK - 1]
        top = best[0]
        zsum = jnp.zeros((PEER_HEADS, tn), F32)
        for p in range(PEER_TOPK):
            for q in range(_CAND_Q[p]):
                cnd = v1[p] + v2[q]
                zsum = zsum + jnp.where(cnd >= tau, jnp.exp(cnd - top), 0.0)
        zinv = 1.0 / zsum
        for hd in range(PEER_HEADS):
            hs_ref[hd, _HS_TAU:_HS_TAU + 1, :] = tau[hd:hd + 1]
            hs_ref[hd, _HS_ZINV:_HS_ZINV + 1, :] = zinv[hd:hd + 1]
            hs_ref[hd, _HS_TOP1:_HS_TOP1 + 1, :] = v1[0][hd:hd + 1]

        def staircase(idx, carry):
            hd = idx // (PEER_NKEYS // rb)
            rows = pl.ds(pl.multiple_of((idx % (PEER_NKEYS // rb)) * rb, rb), rb)
            s1 = sa_ref[hd, rows, :]
            tau_h = hs_ref[hd, _HS_TAU:_HS_TAU + 1, :]
            theta = jnp.full((rb, tn), jnp.inf, F32)
            for q in range(PEER_TOPK):
                v2q = hs_ref[hd, q:q + 1, :]
                theta = jnp.where(s1 + v2q >= tau_h, v2q, theta)
            e2_ref[hd, rows, :] = (jnp.exp(sb_ref[hd, rows, :] - hs_ref[hd, 0:1, :])
                                   * hs_ref[hd, _HS_ZINV:_HS_ZINV + 1, :])
            e1_ref[hd, rows, :] = 0.5 * jnp.exp(s1 - hs_ref[hd, _HS_TOP1:_HS_TOP1 + 1, :])
            sa_ref[hd, rows, :] = theta
            return carry

        lax.fori_loop(0, PEER_HEADS * (PEER_NKEYS // rb), staircase, 0)

    ht_ref[...] = _dot_nt(u_ref[...], hn_ref[...])
    a_rows = pl.ds(pl.multiple_of(e * na, na), na)
    for hd in range(PEER_HEADS):
        ca_ref[hd] = sa_ref[hd, a_rows, :]
        ea_ref[hd] = e1_ref[hd, a_rows, :]

    ga = 4
    gb = 4
    n_bg = PEER_NKEYS // (gb * SUBLANES)

    def gate(idx, carry):
        cols = pl.ds(pl.multiple_of((idx // n_bg) * LANES, LANES), LANES)
        b0 = pl.multiple_of((idx % n_bg) * (gb * SUBLANES), gb * SUBLANES)
        bc = lambda ref, hd, r: jnp.broadcast_to(ref[hd, r:r + 1, cols], (SUBLANES, LANES))
        for a_blk in range(0, na, ga):
            w = [[jnp.zeros((SUBLANES, LANES), F32) for _ in range(gb)] for _ in range(ga)]
            for hd in range(PEER_HEADS):
                s2 = [sb_ref[hd, pl.ds(b0 + t * SUBLANES, SUBLANES), cols] for t in range(gb)]
                e2 = [e2_ref[hd, pl.ds(b0 + t * SUBLANES, SUBLANES), cols] for t in range(gb)]
                for i in range(ga):
                    theta_b = bc(ca_ref, hd, a_blk + i)
                    e1_b = bc(ea_ref, hd, a_blk + i)
                    for t in range(gb):
                        w[i][t] = w[i][t] + jnp.where(s2[t] >= theta_b, e2[t], 0.0) * e1_b
            for i in range(ga):
                for t in range(0, gb, 2):
                    rows = pl.ds(pl.multiple_of((a_blk + i) * PEER_NKEYS + b0 + t * SUBLANES, PACK), PACK)
                    hx = ht_ref[rows, cols]
                    act = hx * (1.0 + lax.erf(hx * (2.0 ** -0.5)))
                    g_ref[rows, cols] = (jnp.concatenate([w[i][t], w[i][t + 1]], axis=0) * act).astype(BF16)
        return carry

    lax.fori_loop(0, nt * n_bg, gate, 0)
    acc_ref[...] += _dot(vt_ref[...], g_ref[...])

    @pl.when(e == pl.num_programs(1) - 1)
    def _finish():
        y_ref[...] = _rms(h_ref[...] + acc_ref[...].T, nfin_ref[...])


def _peer(h2d, nf, nfin, wq_t, k1, k2, u, vt, tn, ec):
    n = h2d.shape[0]
    assert ec % (SUBLANES * PEER_NKEYS) == 0 and tn % LANES == 0
    tok = pl.BlockSpec((tn, D_MODEL), lambda i, e: (i, 0))
    const2 = lambda i, e: (0, 0)
    const3 = lambda i, e: (0, 0, 0)
    vec = pl.BlockSpec((1, D_MODEL), const2)
    kspec = pl.BlockSpec((PEER_HEADS, PEER_NKEYS, PEER_HALF), const3)
    head_f32 = pltpu.VMEM((PEER_HEADS, PEER_NKEYS, tn), F32)
    rank_tok = pltpu.VMEM((PEER_TOPK, PEER_HEADS, tn), F32)
    stage = pltpu.VMEM((PEER_HEADS, ec // PEER_NKEYS, tn), F32)
    return pl.pallas_call(
        functools.partial(_peer_kernel, tn=tn, ec=ec),
        grid=(n // tn, PEER_N_EXPERTS // ec),
        in_specs=[tok, vec, vec,
                  pl.BlockSpec((PEER_HEADS * PEER_QDIM, D_MODEL), const2, pipeline_mode=pl.Buffered(1)),
                  kspec, kspec,
                  pl.BlockSpec((ec, D_MODEL), lambda i, e: (e, 0)),
                  pl.BlockSpec((D_MODEL, ec), lambda i, e: (0, e))],
        out_specs=tok,
        out_shape=jax.ShapeDtypeStruct((n, D_MODEL), F32),
        scratch_shapes=[pltpu.VMEM((tn, D_MODEL), BF16),
                        head_f32, head_f32, head_f32, head_f32, rank_tok, rank_tok,
                        pltpu.VMEM((PEER_HEADS, _HS_ROWS, tn), F32),
                        stage, stage, pltpu.VMEM((ec, tn), F32), pltpu.VMEM((ec, tn), BF16),
                        pltpu.VMEM((D_MODEL, tn), F32)],
        compiler_params=pltpu.CompilerParams(
            dimension_semantics=("parallel", "arbitrary"), vmem_limit_bytes=VMEM_LIMIT),
        name="peer",
    )(h2d, nf, nfin, wq_t, k1, k2, u, vt)


def _pack_params(norm_mix, w_in, gla_w_decay, gla_b_decay, gla_norm, swa_sinks, w_branch_a, w_branch_b,
                 w_out, norm_ffn, peer_w_q, peer_keys1, peer_keys2, peer_u, peer_v):
    off = np.cumsum((0, GLA_QK, GLA_QK, GLA_V, GLA_V, GLA_RANK, SWA_Q, SWA_KV, SWA_KV, D_MODEL, D_MODEL))
    col = lambda i: w_in[:, off[i]:off[i + 1]]
    lr = jnp.pad(col(4), ((0, 0), (0, LANES - GLA_RANK)))
    w_all = jnp.concatenate([col(0), col(1), col(2), col(3), col(5), col(6), col(7), col(8), col(9), lr],
                            axis=1).astype(BF16)
    wdec = jnp.pad(gla_w_decay, ((0, LANES - GLA_RANK), (0, 0))).astype(BF16)
    return dict(
        nm=norm_mix[None], w_all=w_all, wdec=wdec, bdec=gla_b_decay[None], gn=gla_norm[None],
        sinks=swa_sinks, wa=w_branch_a.astype(BF16), wb=w_branch_b.astype(BF16), wo=w_out.astype(BF16),
        nf=norm_ffn[None], wq_t=peer_w_q.T.astype(BF16), k1=peer_keys1.astype(BF16),
        k2=peer_keys2.astype(BF16), u=peer_u.astype(BF16), vt=peer_v.T.astype(BF16))


def _layer(x, p, nfin, s0, win_k, win_v, *, tn, act_dtype, gla_tb, gla_bb, swa_nb, swa_bb, peer_tn, peer_ec):
    bsz, t, _ = x.shape
    x2d = x.reshape(bsz * t, D_MODEL)
    gq, gk, gv, gg, la, sq, sk, sv, ga, gb = _inproj(x2d, p["nm"], p["w_all"], p["wdec"], p["bdec"], tn, act_dtype)
    r3 = lambda a: a.reshape(bsz, t, a.shape[-1])
    o_a, s_new = _gla(r3(gq), r3(gk), r3(gv), r3(la), s0, gla_tb, gla_bb)
    if win_k is None:
        o_b = _swa(r3(sq), r3(sk), r3(sv), p["sinks"], nb=swa_nb, bb=swa_bb)
        ln = min(WINDOW, t)
        new_k, new_v = r3(sk)[:, t - ln:], r3(sv)[:, t - ln:]
    else:
        wk = win_k.reshape(bsz, WINDOW, SWA_KV)
        wv = win_v.reshape(bsz, WINDOW, SWA_KV)
        o_b = _swa(r3(sq), r3(sk), r3(sv), p["sinks"], wk, wv, bb=swa_bb)
        new_k = jnp.concatenate([wk, r3(sk)], axis=1)[:, -WINDOW:]
        new_v = jnp.concatenate([wv, r3(sv)], axis=1)[:, -WINDOW:]
    h = _merge(x2d, o_a.reshape(bsz * t, GLA_V), gg, o_b.reshape(bsz * t, SWA_Q), ga, gb,
               p["gn"], p["wa"], p["wb"], p["wo"], tn)
    y = _peer(h, p["nf"], nfin, p["wq_t"], p["k1"], p["k2"], p["u"], p["vt"], peer_tn, peer_ec)
    kv_shape = (bsz, -1, SWA_KV_HEADS, SWA_HEAD_DIM)
    return y.reshape(bsz, t, D_MODEL), new_k.reshape(kv_shape), new_v.reshape(kv_shape), s_new


def kernel(x_prompt, x_sample, cache_win_k, cache_win_v, state_gla, norm_mix, w_in, gla_w_decay, gla_b_decay,
           gla_norm, swa_sinks, w_branch_a, w_branch_b, w_out, norm_ffn, peer_w_q, peer_keys1, peer_keys2,
           peer_u, peer_v, norm_final):
    depth = w_in.shape[0]
    assert depth == 1, "the final norm is fused into the last layer's channel mixer; one layer supported"
    p = _pack_params(norm_mix[0], w_in[0], gla_w_decay[0], gla_b_decay[0], gla_norm[0], swa_sinks[0],
                     w_branch_a[0], w_branch_b[0], w_out[0], norm_ffn[0], peer_w_q[0], peer_keys1[0],
                     peer_keys2[0], peer_u[0], peer_v[0])
    nfin = norm_final[None]
    yp, pk, pv, ps = _layer(x_prompt, p, nfin, None, None, None,
                            tn=512, act_dtype=BF16, gla_tb=512, gla_bb=1, swa_nb=4, swa_bb=1, peer_tn=512, peer_ec=2048)
    ys, sk, sv, ss = _layer(x_sample, p, nfin, state_gla[0], cache_win_k[0], cache_win_v[0],
                            tn=512, act_dtype=F32, gla_tb=x_sample.shape[1], gla_bb=4, swa_nb=1, swa_bb=4, peer_tn=512, peer_ec=2048)
    return (yp, ys, pk[None], pv[None], ps[None], sk[None], sv[None], ss[None])
```

```python
import functools

import jax
import jax.numpy as jnp
import numpy as np
from jax import lax
from jax.experimental import pallas as pl
from jax.experimental.pallas import tpu as pltpu

F32 = jnp.float32
BF16 = jnp.bfloat16

D_MODEL = 1024
GLA_HEADS = 4
GLA_DK = 128
GLA_DV = 256
GLA_RANK = 16
GLA_TAU = 16.0
SWA_HEADS = 16
SWA_KV_HEADS = 2
SWA_GROUP = SWA_HEADS // SWA_KV_HEADS
SWA_HEAD_DIM = 64
WINDOW = 128
PEER_HEADS = 8
PEER_NKEYS = 128
PEER_N_EXPERTS = PEER_NKEYS * PEER_NKEYS
PEER_QDIM = 256
PEER_HALF = PEER_QDIM // 2
PEER_TOPK = 16
EPS = 1e-6

GLA_QK = GLA_HEADS * GLA_DK
GLA_V = GLA_HEADS * GLA_DV
SWA_Q = SWA_HEADS * SWA_HEAD_DIM
SWA_KV = SWA_KV_HEADS * SWA_HEAD_DIM

LANES = 128
SUBLANES = 8
GLA_CHUNK = 128
GLA_HEAD_GROUP = 4
GLA_SAFE_DECAY = 60.0
VMEM_LIMIT = 56 * 1024 * 1024
NEG_BIG = -1e30

_NT = (((1,), (1,)), ((), ()))


def _rms(x, g):
    return x * lax.rsqrt(jnp.mean(x * x, axis=-1, keepdims=True) + EPS) * g


def _dot(a, b):
    return jnp.dot(a, b, preferred_element_type=F32)


def _dot_nt(a, b):
    return lax.dot_general(a, b, _NT, preferred_element_type=F32)


def _sigmoid(x):
    return 1.0 / (1.0 + jnp.exp(-x))


_P_GQ, _P_GK, _P_GV, _P_GG, _P_SQ, _P_SK, _P_SV, _P_GA, _P_GB, _P_LR, _P_END = (
    0, 512, 1024, 2048, 3072, 4096, 4224, 4352, 5376, 6400, 6528)


def _inproj_kernel(x_ref, nm_ref, w_ref, wdec_ref, bdec_ref,
                   gq_ref, gk_ref, gv_ref, gg_ref, la_ref, sq_ref, sk_ref, sv_ref, ga_ref, gb_ref):
    xb = _rms(x_ref[...], nm_ref[...]).astype(BF16)

    def proj(lo, hi):
        return _dot(xb, w_ref[:, lo:hi])

    gq_ref[...] = proj(_P_GQ, _P_GK)
    gk_ref[...] = proj(_P_GK, _P_GV)
    gv_ref[...] = proj(_P_GV, _P_GG).astype(gv_ref.dtype)
    gg_ref[...] = proj(_P_GG, _P_SQ).astype(gg_ref.dtype)
    sq_ref[...] = proj(_P_SQ, _P_SK).astype(sq_ref.dtype)
    sk_ref[...] = proj(_P_SK, _P_SV)
    sv_ref[...] = proj(_P_SV, _P_GA)
    ga_ref[...] = proj(_P_GA, _P_GB).astype(ga_ref.dtype)
    gb_ref[...] = proj(_P_GB, _P_LR).astype(gb_ref.dtype)
    glr = proj(_P_LR, _P_END)
    z = _dot(glr.astype(BF16), wdec_ref[...]) + bdec_ref[...]
    la_ref[...] = (jnp.minimum(z, 0.0) - jnp.log(1.0 + jnp.exp(-jnp.abs(z)))) * (1.0 / GLA_TAU)


def _inproj(x2d, nm, w_all, wdec, bdec, tn, act_dtype):
    n = x2d.shape[0]
    widths = (GLA_QK, GLA_QK, GLA_V, GLA_V, GLA_QK, SWA_Q, SWA_KV, SWA_KV, D_MODEL, D_MODEL)
    dtypes = (F32, F32, act_dtype, act_dtype, F32, act_dtype, F32, F32, act_dtype, act_dtype)
    const = lambda i: (0, 0)
    return pl.pallas_call(
        _inproj_kernel,
        grid=(n // tn,),
        in_specs=[
            pl.BlockSpec((tn, D_MODEL), lambda i: (i, 0)),
            pl.BlockSpec((1, D_MODEL), const),
            pl.BlockSpec((D_MODEL, _P_END), const, pipeline_mode=pl.Buffered(1)),
            pl.BlockSpec((LANES, GLA_QK), const),
            pl.BlockSpec((1, GLA_QK), const),
        ],
        out_specs=[pl.BlockSpec((tn, w), lambda i: (i, 0)) for w in widths],
        out_shape=[jax.ShapeDtypeStruct((n, w), d) for w, d in zip(widths, dtypes)],
        compiler_params=pltpu.CompilerParams(
            dimension_semantics=("parallel",), vmem_limit_bytes=VMEM_LIMIT),
        name="in_proj",
    )(x2d, nm, w_all, wdec, bdec)


def _gla_kernel(*refs, bb, n_chunks, t_valid, has_s0):
    if has_s0:
        q_ref, k_ref, v_ref, la_ref, s0_ref, o_ref, so_ref, s_ref = refs
    else:
        q_ref, k_ref, v_ref, la_ref, o_ref, so_ref, s_ref = refs
    j = pl.program_id(1)
    c = GLA_CHUNK

    @pl.when(j == 0)
    def _():
        if has_s0:
            s_ref[...] = s0_ref[...]
        else:
            s_ref[...] = jnp.zeros_like(s_ref)

    row = lax.broadcasted_iota(jnp.int32, (c, c), 0)
    col = lax.broadcasted_iota(jnp.int32, (c, c), 1)
    causal = row >= col
    tril = causal.astype(BF16)
    scale = GLA_DK ** -0.5

    def load(ref, bi, r0, lo, w):
        if t_valid >= c:
            return ref[bi, r0:r0 + c, lo:lo + w]
        t = ref[bi, :, lo:lo + w]
        return jnp.concatenate([t, jnp.zeros((c - t_valid, w), t.dtype)], axis=0)

    def pairwise_scores(qs, k, b):
        def score_row(i, a_t):
            pick = row == i
            q_i = jnp.sum(jnp.where(pick, qs, 0.0), axis=0, keepdims=True)
            b_i = jnp.sum(jnp.where(pick, b, 0.0), axis=0, keepdims=True)
            rel = jnp.where(row <= i, b_i - b, -jnp.inf)
            s_col = jnp.sum(k * jnp.exp(rel) * q_i, axis=1, keepdims=True)
            return jnp.where(col == i, s_col, a_t)
        return lax.fori_loop(0, c, score_row, jnp.zeros((c, c), F32)).T

    def run(factored):
        pairs = [(bi, h) for bi in range(bb) for h in range(GLA_HEADS)]
        groups = [pairs[g0:g0 + GLA_HEAD_GROUP] for g0 in range(0, len(pairs), GLA_HEAD_GROUP)]
        for ci in range(n_chunks):
            r0 = ci * c
            for hs in groups:
                q = [load(q_ref, bi, r0, h * GLA_DK, GLA_DK) for bi, h in hs]
                k = [load(k_ref, bi, r0, h * GLA_DK, GLA_DK) for bi, h in hs]
                v = [load(v_ref, bi, r0, h * GLA_DV, GLA_DV).astype(BF16) for bi, h in hs]
                la = [load(la_ref, bi, r0, h * GLA_DK, GLA_DK) for bi, h in hs]
                la_hi = [x.astype(BF16) for x in la]
                la_lo = [(x - y.astype(F32)).astype(BF16) for x, y in zip(la, la_hi)]
                b = [_dot(tril, x) + _dot(tril, y) for x, y in zip(la_hi, la_lo)]
                qs = [x * scale for x in q]
                qd = [(x * jnp.exp(y)).astype(BF16) for x, y in zip(qs, b)]
                if factored:
                    kn = [(x * jnp.exp(-y)).astype(BF16) for x, y in zip(k, b)]
                    a = [jnp.where(causal, _dot_nt(x, y), 0.0) for x, y in zip(qd, kn)]
                else:
                    a = [pairwise_scores(x, y, z) for x, y, z in zip(qs, k, b)]
                s_old = [s_ref[bi, h] for bi, h in hs]
                o = [_dot(x, s.astype(BF16)) + _dot(y.astype(BF16), z) for x, s, y, z in zip(qd, s_old, a, v)]
                for (bi, h), x in zip(hs, o):
                    if t_valid >= c:
                        o_ref[bi, r0:r0 + c, h * GLA_DV:(h + 1) * GLA_DV] = x.astype(o_ref.dtype)
                    else:
                        o_ref[bi, :, h * GLA_DV:(h + 1) * GLA_DV] = x[:t_valid].astype(o_ref.dtype)
                bl = [y[c - 1:c, :] for y in b]
                kd_t = [(x * jnp.exp(l - y)).T.astype(BF16) for x, l, y in zip(k, bl, b)]
                dec_t = [jnp.broadcast_to(jnp.exp(l), (c, GLA_DK)).T for l in bl]
                for (bi, h), d, s, x, z in zip(hs, dec_t, s_old, kd_t, v):
                    s_ref[bi, h] = jnp.concatenate([d] * (GLA_DV // c), axis=1) * s + _dot(x, z)

    worst = jnp.float32(0.0)
    for bi in range(bb):
        for ci in range(n_chunks):
            rows = slice(ci * c, (ci + 1) * c) if t_valid >= c else slice(None)
            worst = jnp.maximum(worst, jnp.max(-jnp.sum(la_ref[bi, rows, :], axis=0, keepdims=True)))
    safe = worst < GLA_SAFE_DECAY
    pl.when(safe)(lambda: run(True))
    pl.when(jnp.logical_not(safe))(lambda: run(False))

    @pl.when(j == pl.num_programs(1) - 1)
    def _():
        so_ref[...] = s_ref[...]


def _gla(gq, gk, gv, la, s0, tb, bb):
    bsz, t, _ = gq.shape
    t_valid = min(tb, GLA_CHUNK)
    n_chunks = max(tb // GLA_CHUNK, 1)
    has_s0 = s0 is not None
    tok = lambda w: pl.BlockSpec((bb, tb, w), lambda b, j: (b, j, 0))
    st = pl.BlockSpec((bb, GLA_HEADS, GLA_DK, GLA_DV), lambda b, j: (b, 0, 0, 0))
    in_specs = [tok(GLA_QK), tok(GLA_QK), tok(GLA_V), tok(GLA_QK)] + ([st] if has_s0 else [])
    args = (gq, gk, gv, la) + ((s0,) if has_s0 else ())
    return pl.pallas_call(
        functools.partial(_gla_kernel, bb=bb, n_chunks=n_chunks, t_valid=t_valid, has_s0=has_s0),
        grid=(bsz // bb, t // tb),
        in_specs=in_specs,
        out_specs=[tok(GLA_V), st],
        out_shape=[jax.ShapeDtypeStruct((bsz, t, GLA_V), gv.dtype),
                   jax.ShapeDtypeStruct((bsz, GLA_HEADS, GLA_DK, GLA_DV), F32)],
        scratch_shapes=[pltpu.VMEM((bb, GLA_HEADS, GLA_DK, GLA_DV), F32)],
        compiler_params=pltpu.CompilerParams(
            dimension_semantics=("parallel", "arbitrary"), vmem_limit_bytes=VMEM_LIMIT),
        name="gla",
    )(*args)


def _swa_kernel(sink_ref, q_ref, kc_ref, vc_ref, kp_ref, vp_ref, o_ref, s_ref, p_ref, *, bb, tq, nb, prev_is_block):
    i = pl.program_id(1)
    w = WINDOW
    nk = 2 * w
    pairs = SWA_GROUP // 2
    t_idx = lax.broadcasted_iota(jnp.int32, (tq, nk), 0)
    j_idx = lax.broadcasted_iota(jnp.int32, (tq, nk), 1)
    d = j_idx - t_idx
    band = (d >= 1) & (d <= w)
    lane = lax.broadcasted_iota(jnp.int32, (nk, LANES), 1)
    left = lane < SWA_HEAD_DIM

    def window(prev_ref, cur_ref, bi, blk):
        cur = cur_ref[bi, blk * tq:(blk + 1) * tq, :]
        if tq < w:
            cur = jnp.concatenate([cur, jnp.zeros((w - tq, SWA_KV), F32)], axis=0)
        prev = prev_ref[bi] if blk == 0 else cur_ref[bi, (blk - 1) * tq:blk * tq, :]
        full = jnp.concatenate([prev, cur], axis=0)
        return full, pltpu.roll(full, SWA_HEAD_DIM, 1)

    def block_diag(full, rot, kvh):
        a, b = (full, rot) if kvh == 0 else (rot, full)
        return jnp.concatenate([jnp.where(left, a, 0.0), jnp.where(left, 0.0, b)], axis=0).astype(BF16)

    def mask(blk):
        if prev_is_block and blk == 0:
            return band & ((j_idx >= w) | (i > 0))
        return band

    units = [((bi, blk), kvh) for bi in range(bb) for blk in range(nb) for kvh in range(SWA_KV_HEADS)]
    ug = s_ref.shape[0]
    win, vis = {}, {}
    for g0 in range(0, len(units), ug):
        grp = units[g0:g0 + ug]
        for blk, _ in grp:
            if blk not in win:
                win[blk] = (window(kp_ref, kc_ref, *blk), window(vp_ref, vc_ref, *blk))
                vis[blk] = mask(blk[1])
        kbd = [block_diag(*win[blk][0], kvh) for blk, kvh in grp]
        vbd = [block_diag(*win[blk][1], kvh) for blk, kvh in grp]
        base = [kvh * SWA_GROUP * SWA_HEAD_DIM for _, kvh in grp]
        rows = [(blk[0], slice(blk[1] * tq, (blk[1] + 1) * tq)) for blk, _ in grp]
        for gi, ((bi, r), b0) in enumerate(zip(rows, base)):
            qs = jnp.concatenate([q_ref[bi, r, b0 + p * LANES: b0 + (p + 1) * LANES] for p in range(pairs)], axis=0)
            s_ref[gi] = _dot_nt(qs.astype(BF16), kbd[gi]) * (SWA_HEAD_DIM ** -0.5)
        for p in range(pairs):
            for e in range(2):
                pr, seg = slice(p * tq, (p + 1) * tq), slice(e * nk, (e + 1) * nk)
                se = [jnp.where(vis[blk], s_ref[gi, pr, seg], NEG_BIG) for gi, (blk, _) in enumerate(grp)]
                sink = [sink_ref[kvh * SWA_GROUP + 2 * p + e] for _, kvh in grp]
                m = [jnp.maximum(jnp.max(x, axis=-1, keepdims=True), y) for x, y in zip(se, sink)]
                ex = [jnp.exp(x - y) for x, y in zip(se, m)]
                den = [jnp.sum(x, axis=-1, keepdims=True) + jnp.exp(y - z) for x, y, z in zip(ex, sink, m)]
                for gi, (x, y) in enumerate(zip(ex, den)):
                    p_ref[gi, pr, seg] = (x / y).astype(BF16)
        for gi, ((bi, r), b0) in enumerate(zip(rows, base)):
            o = _dot(p_ref[gi], vbd[gi])
            for p in range(pairs):
                o_ref[bi, r, b0 + p * LANES: b0 + (p + 1) * LANES] = o[p * tq:(p + 1) * tq].astype(o_ref.dtype)


def _swa(sq, sk, sv, sinks, prev_k=None, prev_v=None, nb=1, bb=1):
    bsz, t, _ = sq.shape
    prev_is_block = prev_k is None
    if prev_is_block:
        tq = WINDOW
        prev_k, prev_v = sk, sv
        prev_spec = pl.BlockSpec((bb, WINDOW, SWA_KV), lambda b, i: (b, jnp.maximum(i * nb - 1, 0), 0))
    else:
        tq = t
        assert nb == 1
        prev_spec = pl.BlockSpec((bb, WINDOW, SWA_KV), lambda b, i: (b, 0, 0))
    ug = SWA_KV_HEADS
    rows = (SWA_GROUP // 2) * tq
    cur_spec = pl.BlockSpec((bb, nb * tq, SWA_KV), lambda b, i: (b, i, 0))
    q_spec = pl.BlockSpec((bb, nb * tq, SWA_Q), lambda b, i: (b, i, 0))
    return pl.pallas_call(
        functools.partial(_swa_kernel, bb=bb, tq=tq, nb=nb, prev_is_block=prev_is_block),
        grid=(bsz // bb, t // (nb * tq)),
        in_specs=[pl.BlockSpec(memory_space=pltpu.SMEM), q_spec, cur_spec, cur_spec, prev_spec, prev_spec],
        out_specs=q_spec,
        out_shape=jax.ShapeDtypeStruct((bsz, t, SWA_Q), sq.dtype),
        scratch_shapes=[pltpu.VMEM((ug, rows, 4 * WINDOW), F32), pltpu.VMEM((ug, rows, 4 * WINDOW), BF16)],
        compiler_params=pltpu.CompilerParams(
            dimension_semantics=("parallel", "arbitrary"), vmem_limit_bytes=VMEM_LIMIT),
        name="swa",
    )(sinks, sq, sk, sv, prev_k, prev_v)


def _merge_kernel(x_ref, oa_ref, gg_ref, ob_ref, ga_ref, gb_ref, gn_ref, wa_ref, wb_ref, wo_ref, h_ref):
    parts = []
    for hd in range(GLA_HEADS):
        sl = slice(hd * GLA_DV, (hd + 1) * GLA_DV)
        g = gg_ref[:, sl].astype(F32)
        parts.append(_rms(oa_ref[:, sl].astype(F32), gn_ref[...]) * (g * _sigmoid(g)))
    oa = jnp.concatenate(parts, axis=1).astype(BF16)
    br_a = _dot(oa, wa_ref[...])
    br_b = _dot(ob_ref[...].astype(BF16), wb_ref[...])
    merged = _sigmoid(ga_ref[...].astype(F32)) * br_a + _sigmoid(gb_ref[...].astype(F32)) * br_b
    h_ref[...] = x_ref[...] + _dot(merged.astype(BF16), wo_ref[...])


def _merge(x2d, oa, gg, ob, ga, gb, gn, wa, wb, wo, tn):
    n = x2d.shape[0]
    tok = pl.BlockSpec((tn, D_MODEL), lambda i: (i, 0))
    const = lambda i: (0, 0)
    wspec = pl.BlockSpec((D_MODEL, D_MODEL), const)
    return pl.pallas_call(
        _merge_kernel,
        grid=(n // tn,),
        in_specs=[tok] * 6 + [pl.BlockSpec((1, GLA_DV), const), wspec, wspec, wspec],
        out_specs=tok,
        out_shape=jax.ShapeDtypeStruct((n, D_MODEL), F32),
        compiler_params=pltpu.CompilerParams(
            dimension_semantics=("parallel",), vmem_limit_bytes=VMEM_LIMIT),
        name="merge",
    )(x2d, oa, gg, ob, ga, gb, gn, wa, wb, wo)


def _sort16_pairs():
    pairs = []
    n = 16
    p = 1
    while p < n:
        k = p
        while k >= 1:
            for j in range(k % p, n - k, 2 * k):
                for i in range(min(k, n - j - k)):
                    if (i + j) // (2 * p) == (i + j + k) // (2 * p):
                        pairs.append((i + j, i + j + k))
            k //= 2
        p *= 2
    return pairs


_SORT16 = _sort16_pairs()


def _bitonic_top16(xa, xb):
    z = [jnp.maximum(xa[i], xb[15 - i]) for i in range(16)]
    dist = 8
    while dist >= 1:
        for i in range(16):
            if i & dist == 0:
                hi = jnp.maximum(z[i], z[i + dist])
                lo = jnp.minimum(z[i], z[i + dist])
                z[i], z[i + dist] = hi, lo
        dist //= 2
    return z


def _top16_rows(s):
    x = [s[SUBLANES * v:SUBLANES * (v + 1), :] for v in range(16)]
    for i, j in _SORT16:
        hi = jnp.maximum(x[i], x[j])
        lo = jnp.minimum(x[i], x[j])
        x[i], x[j] = hi, lo
    shift = 1
    while shift < SUBLANES:
        x = _bitonic_top16(x, [pltpu.roll(t, shift, 0) for t in x])
        shift *= 2
    return [t[SUBLANES - 1:SUBLANES, :] for t in x]


_CAND_Q = [PEER_TOPK // (p + 1) for p in range(PEER_TOPK)]


PACK = 2 * SUBLANES
_HS_TAU, _HS_ZINV, _HS_TOP1, _HS_ROWS = PEER_TOPK, PEER_TOPK + 1, PEER_TOPK + 2, PEER_TOPK + SUBLANES


def _peer_kernel(h_ref, nf_ref, nfin_ref, wq_ref, k1_ref, k2_ref, u_ref, vt_ref, y_ref,
                 hn_ref, sa_ref, sb_ref, e1_ref, e2_ref, v1_ref, v2_ref, hs_ref,
                 ca_ref, ea_ref, ht_ref, g_ref, acc_ref, *, tn, ec):
    e = pl.program_id(1)
    na = ec // PEER_NKEYS
    nt = tn // LANES
    rb = 4 * SUBLANES

    @pl.when(e == 0)
    def _routing():
        hn_ref[...] = _rms(h_ref[...], nf_ref[...]).astype(BF16)
        acc_ref[...] = jnp.zeros_like(acc_ref)

        for hd in range(PEER_HEADS):
            wq = wq_ref[hd * PEER_QDIM:(hd + 1) * PEER_QDIM, :]
            qt = _dot_nt(wq, hn_ref[...])
            s1 = _dot(k1_ref[hd], qt[:PEER_HALF].astype(BF16))
            s2 = _dot(k2_ref[hd], qt[PEER_HALF:].astype(BF16))
            sa_ref[hd] = s1
            sb_ref[hd] = s2
            for p, r in enumerate(_top16_rows(s1)):
                v1_ref[p, hd:hd + 1, :] = r
            for p, r in enumerate(_top16_rows(s2)):
                v2_ref[p, hd:hd + 1, :] = r
                hs_ref[hd, p:p + 1, :] = r

        v1 = [v1_ref[p] for p in range(PEER_TOPK)]
        v2 = [v2_ref[q] for q in range(PEER_TOPK)]
        ninf = jnp.full((PEER_HEADS, tn), -jnp.inf, F32)
        best = [v1[0] + v2[q] for q in range(PEER_TOPK)]
        for p in range(1, PEER_TOPK):
            row = [v1[p] + v2[q] if q < _CAND_Q[p] else ninf for q in range(PEER_TOPK)]
            best = _bitonic_top16(best, row)
        tau = best[PEER_TOPK - 1]
        top = best[0]
        zsum = jnp.zeros((PEER_HEADS, tn), F32)
        for p in range(PEER_TOPK):
            for q in range(_CAND_Q[p]):
                cnd = v1[p] + v2[q]
                zsum = zsum + jnp.where(cnd >= tau, jnp.exp(cnd - top), 0.0)
        zinv = 1.0 / zsum
        for hd in range(PEER_HEADS):
            hs_ref[hd, _HS_TAU:_HS_TAU + 1, :] = tau[hd:hd + 1]
            hs_ref[hd, _HS_ZINV:_HS_ZINV + 1, :] = zinv[hd:hd + 1]
            hs_ref[hd, _HS_TOP1:_HS_TOP1 + 1, :] = v1[0][hd:hd + 1]

        def staircase(idx, carry):
            hd = idx // (PEER_NKEYS // rb)
            rows = pl.ds(pl.multiple_of((idx % (PEER_NKEYS // rb)) * rb, rb), rb)
            s1 = sa_ref[hd, rows, :]
            tau_h = hs_ref[hd, _HS_TAU:_HS_TAU + 1, :]
            theta = jnp.full((rb, tn), jnp.inf, F32)
            for q in range(PEER_TOPK):
                v2q = hs_ref[hd, q:q + 1, :]
                theta = jnp.where(s1 + v2q >= tau_h, v2q, theta)
            e2_ref[hd, rows, :] = (jnp.exp(sb_ref[hd, rows, :] - hs_ref[hd, 0:1, :])
                                   * hs_ref[hd, _HS_ZINV:_HS_ZINV + 1, :])
            e1_ref[hd, rows, :] = 0.5 * jnp.exp(s1 - hs_ref[hd, _HS_TOP1:_HS_TOP1 + 1, :])
            sa_ref[hd, rows, :] = theta
            return carry

        lax.fori_loop(0, PEER_HEADS * (PEER_NKEYS // rb), staircase, 0)

    ht_ref[...] = _dot_nt(u_ref[...], hn_ref[...])
    a_rows = pl.ds(pl.multiple_of(e * na, na), na)
    for hd in range(PEER_HEADS):
        ca_ref[hd] = sa_ref[hd, a_rows, :]
        ea_ref[hd] = e1_ref[hd, a_rows, :]

    ga = 4
    gb = 8
    n_bg = PEER_NKEYS // (gb * SUBLANES)

    def gate(idx, carry):
        cols = pl.ds(pl.multiple_of((idx // n_bg) * LANES, LANES), LANES)
        b0 = pl.multiple_of((idx % n_bg) * (gb * SUBLANES), gb * SUBLANES)
        bc = lambda ref, hd, r: jnp.broadcast_to(ref[hd, r:r + 1, cols], (SUBLANES, LANES))
        for a_blk in range(0, na, ga):
            w = [[jnp.zeros((SUBLANES, LANES), F32) for _ in range(gb)] for _ in range(ga)]
            for hd in range(PEER_HEADS):
                s2 = [sb_ref[hd, pl.ds(b0 + t * SUBLANES, SUBLANES), cols] for t in range(gb)]
                e2 = [e2_ref[hd, pl.ds(b0 + t * SUBLANES, SUBLANES), cols] for t in range(gb)]
                for i in range(ga):
                    theta_b = bc(ca_ref, hd, a_blk + i)
                    e1_b = bc(ea_ref, hd, a_blk + i)
                    for t in range(gb):
                        w[i][t] = w[i][t] + jnp.where(s2[t] >= theta_b, e2[t], 0.0) * e1_b
            for i in range(ga):
                for t in range(0, gb, 2):
                    rows = pl.ds(pl.multiple_of((a_blk + i) * PEER_NKEYS + b0 + t * SUBLANES, PACK), PACK)
                    hx = ht_ref[rows, cols]
                    act = hx * (1.0 + lax.erf(hx * (2.0 ** -0.5)))
                    g_ref[rows, cols] = (jnp.concatenate([w[i][t], w[i][t + 1]], axis=0) * act).astype(BF16)
        return carry

    lax.fori_loop(0, nt * n_bg, gate, 0)
    acc_ref[...] += _dot(vt_ref[...], g_ref[...])

    @pl.when(e == pl.num_programs(1) - 1)
    def _finish():
        y_ref[...] = _rms(h_ref[...] + acc_ref[...].T, nfin_ref[...])


def _peer(h2d, nf, nfin, wq_t, k1, k2, u, vt, tn, ec):
    n = h2d.shape[0]
    assert ec % (SUBLANES * PEER_NKEYS) == 0 and tn % LANES == 0
    tok = pl.BlockSpec((tn, D_MODEL), lambda i, e: (i, 0))
    const2 = lambda i, e: (0, 0)
    const3 = lambda i, e: (0, 0, 0)
    vec = pl.BlockSpec((1, D_MODEL), const2)
    kspec = pl.BlockSpec((PEER_HEADS, PEER_NKEYS, PEER_HALF), const3)
    head_f32 = pltpu.VMEM((PEER_HEADS, PEER_NKEYS, tn), F32)
    rank_tok = pltpu.VMEM((PEER_TOPK, PEER_HEADS, tn), F32)
    stage = pltpu.VMEM((PEER_HEADS, ec // PEER_NKEYS, tn), F32)
    return pl.pallas_call(
        functools.partial(_peer_kernel, tn=tn, ec=ec),
        grid=(n // tn, PEER_N_EXPERTS // ec),
        in_specs=[tok, vec, vec,
                  pl.BlockSpec((PEER_HEADS * PEER_QDIM, D_MODEL), const2, pipeline_mode=pl.Buffered(1)),
                  kspec, kspec,
                  pl.BlockSpec((ec, D_MODEL), lambda i, e: (e, 0)),
                  pl.BlockSpec((D_MODEL, ec), lambda i, e: (0, e))],
        out_specs=tok,
        out_shape=jax.ShapeDtypeStruct((n, D_MODEL), F32),
        scratch_shapes=[pltpu.VMEM((tn, D_MODEL), BF16),
                        head_f32, head_f32, head_f32, head_f32, rank_tok, rank_tok,
                        pltpu.VMEM((PEER_HEADS, _HS_ROWS, tn), F32),
                        stage, stage, pltpu.VMEM((ec, tn), F32), pltpu.VMEM((ec, tn), BF16),
                        pltpu.VMEM((D_MODEL, tn), F32)],
        compiler_params=pltpu.CompilerParams(
            dimension_semantics=("parallel", "arbitrary"), vmem_limit_bytes=VMEM_LIMIT),
        name="peer",
    )(h2d, nf, nfin, wq_t, k1, k2, u, vt)


def _pack_params(norm_mix, w_in, gla_w_decay, gla_b_decay, gla_norm, swa_sinks, w_branch_a, w_branch_b,
                 w_out, norm_ffn, peer_w_q, peer_keys1, peer_keys2, peer_u, peer_v):
    off = np.cumsum((0, GLA_QK, GLA_QK, GLA_V, GLA_V, GLA_RANK, SWA_Q, SWA_KV, SWA_KV, D_MODEL, D_MODEL))
    col = lambda i: w_in[:, off[i]:off[i + 1]]
    lr = jnp.pad(col(4), ((0, 0), (0, LANES - GLA_RANK)))
    w_all = jnp.concatenate([col(0), col(1), col(2), col(3), col(5), col(6), col(7), col(8), col(9), lr],
                            axis=1).astype(BF16)
    wdec = jnp.pad(gla_w_decay, ((0, LANES - GLA_RANK), (0, 0))).astype(BF16)
    return dict(
        nm=norm_mix[None], w_all=w_all, wdec=wdec, bdec=gla_b_decay[None], gn=gla_norm[None],
        sinks=swa_sinks, wa=w_branch_a.astype(BF16), wb=w_branch_b.astype(BF16), wo=w_out.astype(BF16),
        nf=norm_ffn[None], wq_t=peer_w_q.T.astype(BF16), k1=peer_keys1.astype(BF16),
        k2=peer_keys2.astype(BF16), u=peer_u.astype(BF16), vt=peer_v.T.astype(BF16))


def _layer(x, p, nfin, s0, win_k, win_v, *, tn, act_dtype, gla_tb, gla_bb, swa_nb, swa_bb, peer_tn, peer_ec):
    bsz, t, _ = x.shape
    x2d = x.reshape(bsz * t, D_MODEL)
    gq, gk, gv, gg, la, sq, sk, sv, ga, gb = _inproj(x2d, p["nm"], p["w_all"], p["wdec"], p["bdec"], tn, act_dtype)
    r3 = lambda a: a.reshape(bsz, t, a.shape[-1])
    o_a, s_new = _gla(r3(gq), r3(gk), r3(gv), r3(la), s0, gla_tb, gla_bb)
    if win_k is None:
        o_b = _swa(r3(sq), r3(sk), r3(sv), p["sinks"], nb=swa_nb, bb=swa_bb)
        ln = min(WINDOW, t)
        new_k, new_v = r3(sk)[:, t - ln:], r3(sv)[:, t - ln:]
    else:
        wk = win_k.reshape(bsz, WINDOW, SWA_KV)
        wv = win_v.reshape(bsz, WINDOW, SWA_KV)
        o_b = _swa(r3(sq), r3(sk), r3(sv), p["sinks"], wk, wv, bb=swa_bb)
        new_k = jnp.concatenate([wk, r3(sk)], axis=1)[:, -WINDOW:]
        new_v = jnp.concatenate([wv, r3(sv)], axis=1)[:, -WINDOW:]
    h = _merge(x2d, o_a.reshape(bsz * t, GLA_V), gg, o_b.reshape(bsz * t, SWA_Q), ga, gb,
               p["gn"], p["wa"], p["wb"], p["wo"], tn)
    y = _peer(h, p["nf"], nfin, p["wq_t"], p["k1"], p["k2"], p["u"], p["vt"], peer_tn, peer_ec)
    kv_shape = (bsz, -1, SWA_KV_HEADS, SWA_HEAD_DIM)
    return y.reshape(bsz, t, D_MODEL), new_k.reshape(kv_shape), new_v.reshape(kv_shape), s_new


def kernel(x_prompt, x_sample, cache_win_k, cache_win_v, state_gla, norm_mix, w_in, gla_w_decay, gla_b_decay,
           gla_norm, swa_sinks, w_branch_a, w_branch_b, w_out, norm_ffn, peer_w_q, peer_keys1, peer_keys2,
           peer_u, peer_v, norm_final):
    depth = w_in.shape[0]
    assert depth == 1, "the final norm is fused into the last layer's channel mixer; one layer supported"
    p = _pack_params(norm_mix[0], w_in[0], gla_w_decay[0], gla_b_decay[0], gla_norm[0], swa_sinks[0],
                     w_branch_a[0], w_branch_b[0], w_out[0], norm_ffn[0], peer_w_q[0], peer_keys1[0],
                     peer_keys2[0], peer_u[0], peer_v[0])
    nfin = norm_final[None]
    yp, pk, pv, ps = _layer(x_prompt, p, nfin, None, None, None,
                            tn=512, act_dtype=BF16, gla_tb=512, gla_bb=1, swa_nb=4, swa_bb=1, peer_tn=512, peer_ec=2048)
    ys, sk, sv, ss = _layer(x_sample, p, nfin, state_gla[0], cache_win_k[0], cache_win_v[0],
                            tn=512, act_dtype=F32, gla_tb=x_sample.shape[1], gla_bb=4, swa_nb=1, swa_bb=4, peer_tn=512, peer_ec=2048)
    return (yp, ys, pk[None], pv[None], ps[None], sk[None], sv[None], ss[None])
```

```python
import functools

import jax
import jax.numpy as jnp
import numpy as np
from jax import lax
from jax.experimental import pallas as pl
from jax.experimental.pallas import tpu as pltpu

F32 = jnp.float32
BF16 = jnp.bfloat16

D_MODEL = 1024
GLA_HEADS = 4
GLA_DK = 128
GLA_DV = 256
GLA_RANK = 16
GLA_TAU = 16.0
SWA_HEADS = 16
SWA_KV_HEADS = 2
SWA_GROUP = SWA_HEADS // SWA_KV_HEADS
SWA_HEAD_DIM = 64
WINDOW = 128
PEER_HEADS = 8
PEER_NKEYS = 128
PEER_N_EXPERTS = PEER_NKEYS * PEER_NKEYS
PEER_QDIM = 256
PEER_HALF = PEER_QDIM // 2
PEER_TOPK = 16
EPS = 1e-6

GLA_QK = GLA_HEADS * GLA_DK
GLA_V = GLA_HEADS * GLA_DV
SWA_Q = SWA_HEADS * SWA_HEAD_DIM
SWA_KV = SWA_KV_HEADS * SWA_HEAD_DIM

LANES = 128
SUBLANES = 8
GLA_CHUNK = 128
GLA_HEAD_GROUP = 4
GLA_SAFE_DECAY = 60.0
VMEM_LIMIT = 56 * 1024 * 1024
NEG_BIG = -1e30

_NT = (((1,), (1,)), ((), ()))


def _rms(x, g):
    return x * lax.rsqrt(jnp.mean(x * x, axis=-1, keepdims=True) + EPS) * g


def _dot(a, b):
    return jnp.dot(a, b, preferred_element_type=F32)


def _dot_nt(a, b):
    return lax.dot_general(a, b, _NT, preferred_element_type=F32)


def _sigmoid(x):
    return 1.0 / (1.0 + jnp.exp(-x))


_P_GQ, _P_GK, _P_GV, _P_GG, _P_SQ, _P_SK, _P_SV, _P_GA, _P_GB, _P_LR, _P_END = (
    0, 512, 1024, 2048, 3072, 4096, 4224, 4352, 5376, 6400, 6528)


def _inproj_kernel(x_ref, nm_ref, w_ref, wdec_ref, bdec_ref,
                   gq_ref, gk_ref, gv_ref, gg_ref, la_ref, sq_ref, sk_ref, sv_ref, ga_ref, gb_ref):
    xb = _rms(x_ref[...], nm_ref[...]).astype(BF16)

    def proj(lo, hi):
        return _dot(xb, w_ref[:, lo:hi])

    gq_ref[...] = proj(_P_GQ, _P_GK)
    gk_ref[...] = proj(_P_GK, _P_GV)
    gv_ref[...] = proj(_P_GV, _P_GG).astype(gv_ref.dtype)
    gg_ref[...] = proj(_P_GG, _P_SQ).astype(gg_ref.dtype)
    sq_ref[...] = proj(_P_SQ, _P_SK).astype(sq_ref.dtype)
    sk_ref[...] = proj(_P_SK, _P_SV)
    sv_ref[...] = proj(_P_SV, _P_GA)
    ga_ref[...] = proj(_P_GA, _P_GB).astype(ga_ref.dtype)
    gb_ref[...] = proj(_P_GB, _P_LR).astype(gb_ref.dtype)
    glr = proj(_P_LR, _P_END)
    z = _dot(glr.astype(BF16), wdec_ref[...]) + bdec_ref[...]
    la_ref[...] = (jnp.minimum(z, 0.0) - jnp.log(1.0 + jnp.exp(-jnp.abs(z)))) * (1.0 / GLA_TAU)


def _inproj(x2d, nm, w_all, wdec, bdec, tn, act_dtype):
    n = x2d.shape[0]
    widths = (GLA_QK, GLA_QK, GLA_V, GLA_V, GLA_QK, SWA_Q, SWA_KV, SWA_KV, D_MODEL, D_MODEL)
    dtypes = (F32, F32, act_dtype, act_dtype, F32, act_dtype, F32, F32, act_dtype, act_dtype)
    const = lambda i: (0, 0)
    return pl.pallas_call(
        _inproj_kernel,
        grid=(n // tn,),
        in_specs=[
            pl.BlockSpec((tn, D_MODEL), lambda i: (i, 0)),
            pl.BlockSpec((1, D_MODEL), const),
            pl.BlockSpec((D_MODEL, _P_END), const, pipeline_mode=pl.Buffered(1)),
            pl.BlockSpec((LANES, GLA_QK), const),
            pl.BlockSpec((1, GLA_QK), const),
        ],
        out_specs=[pl.BlockSpec((tn, w), lambda i: (i, 0)) for w in widths],
        out_shape=[jax.ShapeDtypeStruct((n, w), d) for w, d in zip(widths, dtypes)],
        compiler_params=pltpu.CompilerParams(
            dimension_semantics=("parallel",), vmem_limit_bytes=VMEM_LIMIT),
        name="in_proj",
    )(x2d, nm, w_all, wdec, bdec)


def _gla_kernel(*refs, bb, n_chunks, t_valid, has_s0):
    if has_s0:
        q_ref, k_ref, v_ref, la_ref, s0_ref, o_ref, so_ref, s_ref = refs
    else:
        q_ref, k_ref, v_ref, la_ref, o_ref, so_ref, s_ref = refs
    j = pl.program_id(1)
    c = GLA_CHUNK

    @pl.when(j == 0)
    def _():
        if has_s0:
            s_ref[...] = s0_ref[...]
        else:
            s_ref[...] = jnp.zeros_like(s_ref)

    row = lax.broadcasted_iota(jnp.int32, (c, c), 0)
    col = lax.broadcasted_iota(jnp.int32, (c, c), 1)
    causal = row >= col
    tril = causal.astype(BF16)
    scale = GLA_DK ** -0.5

    def load(ref, bi, r0, lo, w):
        if t_valid >= c:
            return ref[bi, r0:r0 + c, lo:lo + w]
        t = ref[bi, :, lo:lo + w]
        return jnp.concatenate([t, jnp.zeros((c - t_valid, w), t.dtype)], axis=0)

    def pairwise_scores(qs, k, b):
        def score_row(i, a_t):
            pick = row == i
            q_i = jnp.sum(jnp.where(pick, qs, 0.0), axis=0, keepdims=True)
            b_i = jnp.sum(jnp.where(pick, b, 0.0), axis=0, keepdims=True)
            rel = jnp.where(row <= i, b_i - b, -jnp.inf)
            s_col = jnp.sum(k * jnp.exp(rel) * q_i, axis=1, keepdims=True)
            return jnp.where(col == i, s_col, a_t)
        return lax.fori_loop(0, c, score_row, jnp.zeros((c, c), F32)).T

    def run(factored):
        pairs = [(bi, h) for bi in range(bb) for h in range(GLA_HEADS)]
        groups = [pairs[g0:g0 + GLA_HEAD_GROUP] for g0 in range(0, len(pairs), GLA_HEAD_GROUP)]
        for ci in range(n_chunks):
            r0 = ci * c
            for hs in groups:
                q = [load(q_ref, bi, r0, h * GLA_DK, GLA_DK) for bi, h in hs]
                k = [load(k_ref, bi, r0, h * GLA_DK, GLA_DK) for bi, h in hs]
                v = [load(v_ref, bi, r0, h * GLA_DV, GLA_DV).astype(BF16) for bi, h in hs]
                la = [load(la_ref, bi, r0, h * GLA_DK, GLA_DK) for bi, h in hs]
                la_hi = [x.astype(BF16) for x in la]
                la_lo = [(x - y.astype(F32)).astype(BF16) for x, y in zip(la, la_hi)]
                b = [_dot(tril, x) + _dot(tril, y) for x, y in zip(la_hi, la_lo)]
                qs = [x * scale for x in q]
                qd = [(x * jnp.exp(y)).astype(BF16) for x, y in zip(qs, b)]
                if factored:
                    kn = [(x * jnp.exp(-y)).astype(BF16) for x, y in zip(k, b)]
                    a = [jnp.where(causal, _dot_nt(x, y), 0.0) for x, y in zip(qd, kn)]
                else:
                    a = [pairwise_scores(x, y, z) for x, y, z in zip(qs, k, b)]
                s_old = [s_ref[bi, h] for bi, h in hs]
                o = [_dot(x, s.astype(BF16)) + _dot(y.astype(BF16), z) for x, s, y, z in zip(qd, s_old, a, v)]
                for (bi, h), x in zip(hs, o):
                    if t_valid >= c:
                        o_ref[bi, r0:r0 + c, h * GLA_DV:(h + 1) * GLA_DV] = x.astype(o_ref.dtype)
                    else:
                        o_ref[bi, :, h * GLA_DV:(h + 1) * GLA_DV] = x[:t_valid].astype(o_ref.dtype)
                bl = [y[c - 1:c, :] for y in b]
                kd_t = [(x * jnp.exp(l - y)).T.astype(BF16) for x, l, y in zip(k, bl, b)]
                dec_t = [jnp.broadcast_to(jnp.exp(l), (c, GLA_DK)).T for l in bl]
                for (bi, h), d, s, x, z in zip(hs, dec_t, s_old, kd_t, v):
                    s_ref[bi, h] = jnp.concatenate([d] * (GLA_DV // c), axis=1) * s + _dot(x, z)

    worst = jnp.float32(0.0)
    for bi in range(bb):
        for ci in range(n_chunks):
            rows = slice(ci * c, (ci + 1) * c) if t_valid >= c else slice(None)
            worst = jnp.maximum(worst, jnp.max(-jnp.sum(la_ref[bi, rows, :], axis=0, keepdims=True)))
    safe = worst < GLA_SAFE_DECAY
    pl.when(safe)(lambda: run(True))
    pl.when(jnp.logical_not(safe))(lambda: run(False))

    @pl.when(j == pl.num_programs(1) - 1)
    def _():
        so_ref[...] = s_ref[...]


def _gla(gq, gk, gv, la, s0, tb, bb):
    bsz, t, _ = gq.shape
    t_valid = min(tb, GLA_CHUNK)
    n_chunks = max(tb // GLA_CHUNK, 1)
    has_s0 = s0 is not None
    tok = lambda w: pl.BlockSpec((bb, tb, w), lambda b, j: (b, j, 0))
    st = pl.BlockSpec((bb, GLA_HEADS, GLA_DK, GLA_DV), lambda b, j: (b, 0, 0, 0))
    in_specs = [tok(GLA_QK), tok(GLA_QK), tok(GLA_V), tok(GLA_QK)] + ([st] if has_s0 else [])
    args = (gq, gk, gv, la) + ((s0,) if has_s0 else ())
    return pl.pallas_call(
        functools.partial(_gla_kernel, bb=bb, n_chunks=n_chunks, t_valid=t_valid, has_s0=has_s0),
        grid=(bsz // bb, t // tb),
        in_specs=in_specs,
        out_specs=[tok(GLA_V), st],
        out_shape=[jax.ShapeDtypeStruct((bsz, t, GLA_V), gv.dtype),
                   jax.ShapeDtypeStruct((bsz, GLA_HEADS, GLA_DK, GLA_DV), F32)],
        scratch_shapes=[pltpu.VMEM((bb, GLA_HEADS, GLA_DK, GLA_DV), F32)],
        compiler_params=pltpu.CompilerParams(
            dimension_semantics=("parallel", "arbitrary"), vmem_limit_bytes=VMEM_LIMIT),
        name="gla",
    )(*args)


def _swa_kernel(sink_ref, q_ref, kc_ref, vc_ref, kp_ref, vp_ref, o_ref, s_ref, p_ref, *, bb, tq, nb, prev_is_block):
    i = pl.program_id(1)
    w = WINDOW
    nk = 2 * w
    pairs = SWA_GROUP // 2
    t_idx = lax.broadcasted_iota(jnp.int32, (tq, nk), 0)
    j_idx = lax.broadcasted_iota(jnp.int32, (tq, nk), 1)
    d = j_idx - t_idx
    band = (d >= 1) & (d <= w)
    lane = lax.broadcasted_iota(jnp.int32, (nk, LANES), 1)
    left = lane < SWA_HEAD_DIM

    def window(prev_ref, cur_ref, bi, blk):
        cur = cur_ref[bi, blk * tq:(blk + 1) * tq, :]
        if tq < w:
            cur = jnp.concatenate([cur, jnp.zeros((w - tq, SWA_KV), F32)], axis=0)
        prev = prev_ref[bi] if blk == 0 else cur_ref[bi, (blk - 1) * tq:blk * tq, :]
        full = jnp.concatenate([prev, cur], axis=0)
        return full, pltpu.roll(full, SWA_HEAD_DIM, 1)

    def block_diag(full, rot, kvh):
        a, b = (full, rot) if kvh == 0 else (rot, full)
        return jnp.concatenate([jnp.where(left, a, 0.0), jnp.where(left, 0.0, b)], axis=0).astype(BF16)

    def mask(blk):
        if prev_is_block and blk == 0:
            return band & ((j_idx >= w) | (i > 0))
        return band

    units = [((bi, blk), kvh) for bi in range(bb) for blk in range(nb) for kvh in range(SWA_KV_HEADS)]
    ug = s_ref.shape[0]
    win, vis = {}, {}
    for g0 in range(0, len(units), ug):
        grp = units[g0:g0 + ug]
        for blk, _ in grp:
            if blk not in win:
                win[blk] = (window(kp_ref, kc_ref, *blk), window(vp_ref, vc_ref, *blk))
                vis[blk] = mask(blk[1])
        kbd = [block_diag(*win[blk][0], kvh) for blk, kvh in grp]
        vbd = [block_diag(*win[blk][1], kvh) for blk, kvh in grp]
        base = [kvh * SWA_GROUP * SWA_HEAD_DIM for _, kvh in grp]
        rows = [(blk[0], slice(blk[1] * tq, (blk[1] + 1) * tq)) for blk, _ in grp]
        for gi, ((bi, r), b0) in enumerate(zip(rows, base)):
            qs = jnp.concatenate([q_ref[bi, r, b0 + p * LANES: b0 + (p + 1) * LANES] for p in range(pairs)], axis=0)
            s_ref[gi] = _dot_nt((qs * (SWA_HEAD_DIM ** -0.5)).astype(BF16), kbd[gi])
        for p in range(pairs):
            for e in range(2):
                pr, seg = slice(p * tq, (p + 1) * tq), slice(e * nk, (e + 1) * nk)
                se = [jnp.where(vis[blk], s_ref[gi, pr, seg], NEG_BIG) for gi, (blk, _) in enumerate(grp)]
                sink = [sink_ref[kvh * SWA_GROUP + 2 * p + e] for _, kvh in grp]
                m = [jnp.maximum(jnp.max(x, axis=-1, keepdims=True), y) for x, y in zip(se, sink)]
                ex = [jnp.exp(x - y) for x, y in zip(se, m)]
                den = [jnp.sum(x, axis=-1, keepdims=True) + jnp.exp(y - z) for x, y, z in zip(ex, sink, m)]
                for gi, (x, y) in enumerate(zip(ex, den)):
                    p_ref[gi, pr, seg] = (x / y).astype(BF16)
        for gi, ((bi, r), b0) in enumerate(zip(rows, base)):
            o = _dot(p_ref[gi], vbd[gi])
            for p in range(pairs):
                o_ref[bi, r, b0 + p * LANES: b0 + (p + 1) * LANES] = o[p * tq:(p + 1) * tq].astype(o_ref.dtype)


def _swa(sq, sk, sv, sinks, prev_k=None, prev_v=None, nb=1, bb=1):
    bsz, t, _ = sq.shape
    prev_is_block = prev_k is None
    if prev_is_block:
        tq = WINDOW
        prev_k, prev_v = sk, sv
        prev_spec = pl.BlockSpec((bb, WINDOW, SWA_KV), lambda b, i: (b, jnp.maximum(i * nb - 1, 0), 0))
    else:
        tq = t
        assert nb == 1
        prev_spec = pl.BlockSpec((bb, WINDOW, SWA_KV), lambda b, i: (b, 0, 0))
    ug = 2 * SWA_KV_HEADS
    rows = (SWA_GROUP // 2) * tq
    cur_spec = pl.BlockSpec((bb, nb * tq, SWA_KV), lambda b, i: (b, i, 0))
    q_spec = pl.BlockSpec((bb, nb * tq, SWA_Q), lambda b, i: (b, i, 0))
    return pl.pallas_call(
        functools.partial(_swa_kernel, bb=bb, tq=tq, nb=nb, prev_is_block=prev_is_block),
        grid=(bsz // bb, t // (nb * tq)),
        in_specs=[pl.BlockSpec(memory_space=pltpu.SMEM), q_spec, cur_spec, cur_spec, prev_spec, prev_spec],
        out_specs=q_spec,
        out_shape=jax.ShapeDtypeStruct((bsz, t, SWA_Q), sq.dtype),
        scratch_shapes=[pltpu.VMEM((ug, rows, 4 * WINDOW), F32), pltpu.VMEM((ug, rows, 4 * WINDOW), BF16)],
        compiler_params=pltpu.CompilerParams(
            dimension_semantics=("parallel", "arbitrary"), vmem_limit_bytes=VMEM_LIMIT),
        name="swa",
    )(sinks, sq, sk, sv, prev_k, prev_v)


def _merge_kernel(x_ref, oa_ref, gg_ref, ob_ref, ga_ref, gb_ref, gn_ref, wa_ref, wb_ref, wo_ref, h_ref):
    parts = []
    for hd in range(GLA_HEADS):
        sl = slice(hd * GLA_DV, (hd + 1) * GLA_DV)
        g = gg_ref[:, sl].astype(F32)
        parts.append(_rms(oa_ref[:, sl].astype(F32), gn_ref[...]) * (g * _sigmoid(g)))
    oa = jnp.concatenate(parts, axis=1).astype(BF16)
    br_a = _dot(oa, wa_ref[...])
    br_b = _dot(ob_ref[...].astype(BF16), wb_ref[...])
    merged = _sigmoid(ga_ref[...].astype(F32)) * br_a + _sigmoid(gb_ref[...].astype(F32)) * br_b
    h_ref[...] = x_ref[...] + _dot(merged.astype(BF16), wo_ref[...])


def _merge(x2d, oa, gg, ob, ga, gb, gn, wa, wb, wo, tn):
    n = x2d.shape[0]
    tok = pl.BlockSpec((tn, D_MODEL), lambda i: (i, 0))
    const = lambda i: (0, 0)
    wspec = pl.BlockSpec((D_MODEL, D_MODEL), const)
    return pl.pallas_call(
        _merge_kernel,
        grid=(n // tn,),
        in_specs=[tok] * 6 + [pl.BlockSpec((1, GLA_DV), const), wspec, wspec, wspec],
        out_specs=tok,
        out_shape=jax.ShapeDtypeStruct((n, D_MODEL), F32),
        compiler_params=pltpu.CompilerParams(
            dimension_semantics=("parallel",), vmem_limit_bytes=VMEM_LIMIT),
        name="merge",
    )(x2d, oa, gg, ob, ga, gb, gn, wa, wb, wo)


def _sort16_pairs():
    pairs = []
    n = 16
    p = 1
    while p < n:
        k = p
        while k >= 1:
            for j in range(k % p, n - k, 2 * k):
                for i in range(min(k, n - j - k)):
                    if (i + j) // (2 * p) == (i + j + k) // (2 * p):
                        pairs.append((i + j, i + j + k))
            k //= 2
        p *= 2
    return pairs


_SORT16 = _sort16_pairs()


def _bitonic_top16(xa, xb):
    z = [jnp.maximum(xa[i], xb[15 - i]) for i in range(16)]
    dist = 8
    while dist >= 1:
        for i in range(16):
            if i & dist == 0:
                hi = jnp.maximum(z[i], z[i + dist])
                lo = jnp.minimum(z[i], z[i + dist])
                z[i], z[i + dist] = hi, lo
        dist //= 2
    return z


def _top16_rows(s):
    x = [s[SUBLANES * v:SUBLANES * (v + 1), :] for v in range(16)]
    for i, j in _SORT16:
        hi = jnp.maximum(x[i], x[j])
        lo = jnp.minimum(x[i], x[j])
        x[i], x[j] = hi, lo
    shift = 1
    while shift < SUBLANES:
        x = _bitonic_top16(x, [pltpu.roll(t, shift, 0) for t in x])
        shift *= 2
    return [t[SUBLANES - 1:SUBLANES, :] for t in x]


_CAND_Q = [PEER_TOPK // (p + 1) for p in range(PEER_TOPK)]


PACK = 2 * SUBLANES
_HS_TAU, _HS_ZINV, _HS_TOP1, _HS_ROWS = PEER_TOPK, PEER_TOPK + 1, PEER_TOPK + 2, PEER_TOPK + SUBLANES


def _peer_kernel(h_ref, nf_ref, nfin_ref, wq_ref, k1_ref, k2_ref, u_ref, vt_ref, y_ref,
                 hn_ref, sa_ref, sb_ref, e1_ref, e2_ref, v1_ref, v2_ref, hs_ref,
                 ca_ref, ea_ref, ht_ref, g_ref, acc_ref, *, tn, ec):
    e = pl.program_id(1)
    na = ec // PEER_NKEYS
    nt = tn // LANES
    rb = 4 * SUBLANES

    @pl.when(e == 0)
    def _routing():
        hn_ref[...] = _rms(h_ref[...], nf_ref[...]).astype(BF16)
        acc_ref[...] = jnp.zeros_like(acc_ref)

        for hd in range(PEER_HEADS):
            wq = wq_ref[hd * PEER_QDIM:(hd + 1) * PEER_QDIM, :]
            qt = _dot_nt(wq, hn_ref[...])
            s1 = _dot(k1_ref[hd], qt[:PEER_HALF].astype(BF16))
            s2 = _dot(k2_ref[hd], qt[PEER_HALF:].astype(BF16))
            sa_ref[hd] = s1
            sb_ref[hd] = s2
            for p, r in enumerate(_top16_rows(s1)):
                v1_ref[p, hd:hd + 1, :] = r
            for p, r in enumerate(_top16_rows(s2)):
                v2_ref[p, hd:hd + 1, :] = r
                hs_ref[hd, p:p + 1, :] = r

        v1 = [v1_ref[p] for p in range(PEER_TOPK)]
        v2 = [v2_ref[q] for q in range(PEER_TOPK)]
        ninf = jnp.full((PEER_HEADS, tn), -jnp.inf, F32)
        best = [v1[0] + v2[q] for q in range(PEER_TOPK)]
        for p in range(1, PEER_TOPK):
            row = [v1[p] + v2[q] if q < _CAND_Q[p] else ninf for q in range(PEER_TOPK)]
            best = _bitonic_top16(best, row)
        tau = best[PEER_TOPK - 1]
        top = best[0]
        zsum = jnp.zeros((PEER_HEADS, tn), F32)
        for p in range(PEER_TOPK):
            for q in range(_CAND_Q[p]):
                cnd = v1[p] + v2[q]
                zsum = zsum + jnp.where(cnd >= tau, jnp.exp(cnd - top), 0.0)
        zinv = 1.0 / zsum
        for hd in range(PEER_HEADS):
            hs_ref[hd, _HS_TAU:_HS_TAU + 1, :] = tau[hd:hd + 1]
            hs_ref[hd, _HS_ZINV:_HS_ZINV + 1, :] = zinv[hd:hd + 1]
            hs_ref[hd, _HS_TOP1:_HS_TOP1 + 1, :] = v1[0][hd:hd + 1]

        def staircase(idx, carry):
            hd = idx // (PEER_NKEYS // rb)
            rows = pl.ds(pl.multiple_of((idx % (PEER_NKEYS // rb)) * rb, rb), rb)
            s1 = sa_ref[hd, rows, :]
            tau_h = hs_ref[hd, _HS_TAU:_HS_TAU + 1, :]
            theta = jnp.full((rb, tn), jnp.inf, F32)
            for q in range(PEER_TOPK):
                v2q = hs_ref[hd, q:q + 1, :]
                theta = jnp.where(s1 + v2q >= tau_h, v2q, theta)
            e2_ref[hd, rows, :] = (jnp.exp(sb_ref[hd, rows, :] - hs_ref[hd, 0:1, :])
                                   * hs_ref[hd, _HS_ZINV:_HS_ZINV + 1, :])
            e1_ref[hd, rows, :] = 0.5 * jnp.exp(s1 - hs_ref[hd, _HS_TOP1:_HS_TOP1 + 1, :])
            sa_ref[hd, rows, :] = theta
            return carry

        lax.fori_loop(0, PEER_HEADS * (PEER_NKEYS // rb), staircase, 0)

    ht_ref[...] = _dot_nt(u_ref[...], hn_ref[...])
    a_rows = pl.ds(pl.multiple_of(e * na, na), na)
    for hd in range(PEER_HEADS):
        ca_ref[hd] = sa_ref[hd, a_rows, :]
        ea_ref[hd] = e1_ref[hd, a_rows, :]

    ga = 4
    gb = 4
    n_bg = PEER_NKEYS // (gb * SUBLANES)

    def gate(idx, carry):
        cols = pl.ds(pl.multiple_of((idx // n_bg) * LANES, LANES), LANES)
        b0 = pl.multiple_of((idx % n_bg) * (gb * SUBLANES), gb * SUBLANES)
        bc = lambda ref, hd, r: jnp.broadcast_to(ref[hd, r:r + 1, cols], (SUBLANES, LANES))
        for a_blk in range(0, na, ga):
            w = [[jnp.zeros((SUBLANES, LANES), F32) for _ in range(gb)] for _ in range(ga)]
            for hd in range(PEER_HEADS):
                s2 = [sb_ref[hd, pl.ds(b0 + t * SUBLANES, SUBLANES), cols] for t in range(gb)]
                e2 = [e2_ref[hd, pl.ds(b0 + t * SUBLANES, SUBLANES), cols] for t in range(gb)]
                for i in range(ga):
                    theta_b = bc(ca_ref, hd, a_blk + i)
                    e1_b = bc(ea_ref, hd, a_blk + i)
                    for t in range(gb):
                        w[i][t] = w[i][t] + jnp.where(s2[t] >= theta_b, e2[t], 0.0) * e1_b
            for i in range(ga):
                for t in range(0, gb, 2):
                    rows = pl.ds(pl.multiple_of((a_blk + i) * PEER_NKEYS + b0 + t * SUBLANES, PACK), PACK)
                    hx = ht_ref[rows, cols]
                    act = hx * (1.0 + lax.erf(hx * (2.0 ** -0.5)))
                    g_ref[rows, cols] = (jnp.concatenate([w[i][t], w[i][t + 1]], axis=0) * act).astype(BF16)
        return carry

    lax.fori_loop(0, nt * n_bg, gate, 0)
    acc_ref[...] += _dot(vt_ref[...], g_ref[...])

    @pl.when(e == pl.num_programs(1) - 1)
    def _finish():
        y_ref[...] = _rms(h_ref[...] + acc_ref[...].T, nfin_ref[...])


def _peer(h2d, nf, nfin, wq_t, k1, k2, u, vt, tn, ec):
    n = h2d.shape[0]
    assert ec % (SUBLANES * PEER_NKEYS) == 0 and tn % LANES == 0
    tok = pl.BlockSpec((tn, D_MODEL), lambda i, e: (i, 0))
    const2 = lambda i, e: (0, 0)
    const3 = lambda i, e: (0, 0, 0)
    vec = pl.BlockSpec((1, D_MODEL), const2)
    kspec = pl.BlockSpec((PEER_HEADS, PEER_NKEYS, PEER_HALF), const3)
    head_f32 = pltpu.VMEM((PEER_HEADS, PEER_NKEYS, tn), F32)
    rank_tok = pltpu.VMEM((PEER_TOPK, PEER_HEADS, tn), F32)
    stage = pltpu.VMEM((PEER_HEADS, ec // PEER_NKEYS, tn), F32)
    return pl.pallas_call(
        functools.partial(_peer_kernel, tn=tn, ec=ec),
        grid=(n // tn, PEER_N_EXPERTS // ec),
        in_specs=[tok, vec, vec,
                  pl.BlockSpec((PEER_HEADS * PEER_QDIM, D_MODEL), const2, pipeline_mode=pl.Buffered(1)),
                  kspec, kspec,
                  pl.BlockSpec((ec, D_MODEL), lambda i, e: (e, 0)),
                  pl.BlockSpec((D_MODEL, ec), lambda i, e: (0, e))],
        out_specs=tok,
        out_shape=jax.ShapeDtypeStruct((n, D_MODEL), F32),
        scratch_shapes=[pltpu.VMEM((tn, D_MODEL), BF16),
                        head_f32, head_f32, head_f32, head_f32, rank_tok, rank_tok,
                        pltpu.VMEM((PEER_HEADS, _HS_ROWS, tn), F32),
                        stage, stage, pltpu.VMEM((ec, tn), F32), pltpu.VMEM((ec, tn), BF16),
                        pltpu.VMEM((D_MODEL, tn), F32)],
        compiler_params=pltpu.CompilerParams(
            dimension_semantics=("parallel", "arbitrary"), vmem_limit_bytes=VMEM_LIMIT),
        name="peer",
    )(h2d, nf, nfin, wq_t, k1, k2, u, vt)


def _pack_params(norm_mix, w_in, gla_w_decay, gla_b_decay, gla_norm, swa_sinks, w_branch_a, w_branch_b,
                 w_out, norm_ffn, peer_w_q, peer_keys1, peer_keys2, peer_u, peer_v):
    off = np.cumsum((0, GLA_QK, GLA_QK, GLA_V, GLA_V, GLA_RANK, SWA_Q, SWA_KV, SWA_KV, D_MODEL, D_MODEL))
    col = lambda i: w_in[:, off[i]:off[i + 1]]
    lr = jnp.pad(col(4), ((0, 0), (0, LANES - GLA_RANK)))
    w_all = jnp.concatenate([col(0), col(1), col(2), col(3), col(5), col(6), col(7), col(8), col(9), lr],
                            axis=1).astype(BF16)
    wdec = jnp.pad(gla_w_decay, ((0, LANES - GLA_RANK), (0, 0))).astype(BF16)
    return dict(
        nm=norm_mix[None], w_all=w_all, wdec=wdec, bdec=gla_b_decay[None], gn=gla_norm[None],
        sinks=swa_sinks, wa=w_branch_a.astype(BF16), wb=w_branch_b.astype(BF16), wo=w_out.astype(BF16),
        nf=norm_ffn[None], wq_t=peer_w_q.T.astype(BF16), k1=peer_keys1.astype(BF16),
        k2=peer_keys2.astype(BF16), u=peer_u.astype(BF16), vt=peer_v.T.astype(BF16))


def _layer(x, p, nfin, s0, win_k, win_v, *, tn, act_dtype, gla_tb, gla_bb, swa_nb, swa_bb, peer_tn, peer_ec):
    bsz, t, _ = x.shape
    x2d = x.reshape(bsz * t, D_MODEL)
    gq, gk, gv, gg, la, sq, sk, sv, ga, gb = _inproj(x2d, p["nm"], p["w_all"], p["wdec"], p["bdec"], tn, act_dtype)
    r3 = lambda a: a.reshape(bsz, t, a.shape[-1])
    o_a, s_new = _gla(r3(gq), r3(gk), r3(gv), r3(la), s0, gla_tb, gla_bb)
    if win_k is None:
        o_b = _swa(r3(sq), r3(sk), r3(sv), p["sinks"], nb=swa_nb, bb=swa_bb)
        ln = min(WINDOW, t)
        new_k, new_v = r3(sk)[:, t - ln:], r3(sv)[:, t - ln:]
    else:
        wk = win_k.reshape(bsz, WINDOW, SWA_KV)
        wv = win_v.reshape(bsz, WINDOW, SWA_KV)
        o_b = _swa(r3(sq), r3(sk), r3(sv), p["sinks"], wk, wv, bb=swa_bb)
        new_k = jnp.concatenate([wk, r3(sk)], axis=1)[:, -WINDOW:]
        new_v = jnp.concatenate([wv, r3(sv)], axis=1)[:, -WINDOW:]
    h = _merge(x2d, o_a.reshape(bsz * t, GLA_V), gg, o_b.reshape(bsz * t, SWA_Q), ga, gb,
               p["gn"], p["wa"], p["wb"], p["wo"], tn)
    y = _peer(h, p["nf"], nfin, p["wq_t"], p["k1"], p["k2"], p["u"], p["vt"], peer_tn, peer_ec)
    kv_shape = (bsz, -1, SWA_KV_HEADS, SWA_HEAD_DIM)
    return y.reshape(bsz, t, D_MODEL), new_k.reshape(kv_shape), new_v.reshape(kv_shape), s_new


def kernel(x_prompt, x_sample, cache_win_k, cache_win_v, state_gla, norm_mix, w_in, gla_w_decay, gla_b_decay,
           gla_norm, swa_sinks, w_branch_a, w_branch_b, w_out, norm_ffn, peer_w_q, peer_keys1, peer_keys2,
           peer_u, peer_v, norm_final):
    depth = w_in.shape[0]
    assert depth == 1, "the final norm is fused into the last layer's channel mixer; one layer supported"
    p = _pack_params(norm_mix[0], w_in[0], gla_w_decay[0], gla_b_decay[0], gla_norm[0], swa_sinks[0],
                     w_branch_a[0], w_branch_b[0], w_out[0], norm_ffn[0], peer_w_q[0], peer_keys1[0],
                     peer_keys2[0], peer_u[0], peer_v[0])
    nfin = norm_final[None]
    yp, pk, pv, ps = _layer(x_prompt, p, nfin, None, None, None,
                            tn=512, act_dtype=BF16, gla_tb=512, gla_bb=1, swa_nb=4, swa_bb=1, peer_tn=512, peer_ec=2048)
    ys, sk, sv, ss = _layer(x_sample, p, nfin, state_gla[0], cache_win_k[0], cache_win_v[0],
                            tn=512, act_dtype=F32, gla_tb=x_sample.shape[1], gla_bb=4, swa_nb=1, swa_bb=4, peer_tn=512, peer_ec=2048)
    return (yp, ys, pk[None], pv[None], ps[None], sk[None], sv[None], ss[None])
```

```python
import functools

import jax
import jax.numpy as jnp
import numpy as np
from jax import lax
from jax.experimental import pallas as pl
from jax.experimental.pallas import tpu as pltpu

F32 = jnp.float32
BF16 = jnp.bfloat16

D_MODEL = 1024
GLA_HEADS = 4
GLA_DK = 128
GLA_DV = 256
GLA_RANK = 16
GLA_TAU = 16.0
SWA_HEADS = 16
SWA_KV_HEADS = 2
SWA_GROUP = SWA_HEADS // SWA_KV_HEADS
SWA_HEAD_DIM = 64
WINDOW = 128
PEER_HEADS = 8
PEER_NKEYS = 128
PEER_N_EXPERTS = PEER_NKEYS * PEER_NKEYS
PEER_QDIM = 256
PEER_HALF = PEER_QDIM // 2
PEER_TOPK = 16
EPS = 1e-6

GLA_QK = GLA_HEADS * GLA_DK
GLA_V = GLA_HEADS * GLA_DV
SWA_Q = SWA_HEADS * SWA_HEAD_DIM
SWA_KV = SWA_KV_HEADS * SWA_HEAD_DIM

LANES = 128
SUBLANES = 8
GLA_CHUNK = 128
GLA_HEAD_GROUP = 4
GLA_SAFE_DECAY = 60.0
VMEM_LIMIT = 56 * 1024 * 1024
NEG_BIG = -1e30

_NT = (((1,), (1,)), ((), ()))


def _rms(x, g):
    return x * lax.rsqrt(jnp.mean(x * x, axis=-1, keepdims=True) + EPS) * g


def _dot(a, b):
    return jnp.dot(a, b, preferred_element_type=F32)


def _dot_nt(a, b):
    return lax.dot_general(a, b, _NT, preferred_element_type=F32)


def _sigmoid(x):
    return 1.0 / (1.0 + jnp.exp(-x))


_P_GQ, _P_GK, _P_GV, _P_GG, _P_SQ, _P_SK, _P_SV, _P_GA, _P_GB, _P_LR, _P_END = (
    0, 512, 1024, 2048, 3072, 4096, 4224, 4352, 5376, 6400, 6528)


def _inproj_kernel(x_ref, nm_ref, w_ref, wdec_ref, bdec_ref,
                   gq_ref, gk_ref, gv_ref, gg_ref, la_ref, sq_ref, sk_ref, sv_ref, ga_ref, gb_ref):
    xb = _rms(x_ref[...], nm_ref[...]).astype(BF16)

    def proj(lo, hi):
        return _dot(xb, w_ref[:, lo:hi])

    gq_ref[...] = proj(_P_GQ, _P_GK)
    gk_ref[...] = proj(_P_GK, _P_GV)
    gv_ref[...] = proj(_P_GV, _P_GG).astype(gv_ref.dtype)
    gg_ref[...] = proj(_P_GG, _P_SQ).astype(gg_ref.dtype)
    sq_ref[...] = proj(_P_SQ, _P_SK).astype(sq_ref.dtype)
    sk_ref[...] = proj(_P_SK, _P_SV)
    sv_ref[...] = proj(_P_SV, _P_GA)
    ga_ref[...] = proj(_P_GA, _P_GB).astype(ga_ref.dtype)
    gb_ref[...] = proj(_P_GB, _P_LR).astype(gb_ref.dtype)
    glr = proj(_P_LR, _P_END)
    z = _dot(glr.astype(BF16), wdec_ref[...]) + bdec_ref[...]
    la_ref[...] = (jnp.minimum(z, 0.0) - jnp.log(1.0 + jnp.exp(-jnp.abs(z)))) * (1.0 / GLA_TAU)


def _inproj(x2d, nm, w_all, wdec, bdec, tn, act_dtype):
    n = x2d.shape[0]
    widths = (GLA_QK, GLA_QK, GLA_V, GLA_V, GLA_QK, SWA_Q, SWA_KV, SWA_KV, D_MODEL, D_MODEL)
    dtypes = (F32, F32, act_dtype, act_dtype, F32, act_dtype, F32, F32, act_dtype, act_dtype)
    const = lambda i: (0, 0)
    return pl.pallas_call(
        _inproj_kernel,
        grid=(n // tn,),
        in_specs=[
            pl.BlockSpec((tn, D_MODEL), lambda i: (i, 0)),
            pl.BlockSpec((1, D_MODEL), const),
            pl.BlockSpec((D_MODEL, _P_END), const, pipeline_mode=pl.Buffered(1)),
            pl.BlockSpec((LANES, GLA_QK), const),
            pl.BlockSpec((1, GLA_QK), const),
        ],
        out_specs=[pl.BlockSpec((tn, w), lambda i: (i, 0)) for w in widths],
        out_shape=[jax.ShapeDtypeStruct((n, w), d) for w, d in zip(widths, dtypes)],
        compiler_params=pltpu.CompilerParams(
            dimension_semantics=("parallel",), vmem_limit_bytes=VMEM_LIMIT),
        name="in_proj",
    )(x2d, nm, w_all, wdec, bdec)


def _gla_kernel(*refs, bb, n_chunks, t_valid, has_s0):
    if has_s0:
        q_ref, k_ref, v_ref, la_ref, s0_ref, o_ref, so_ref, s_ref = refs
    else:
        q_ref, k_ref, v_ref, la_ref, o_ref, so_ref, s_ref = refs
    j = pl.program_id(1)
    c = GLA_CHUNK

    @pl.when(j == 0)
    def _():
        if has_s0:
            s_ref[...] = s0_ref[...]
        else:
            s_ref[...] = jnp.zeros_like(s_ref)

    row = lax.broadcasted_iota(jnp.int32, (c, c), 0)
    col = lax.broadcasted_iota(jnp.int32, (c, c), 1)
    causal = row >= col
    tril = causal.astype(BF16)
    scale = GLA_DK ** -0.5

    def load(ref, bi, r0, lo, w):
        if t_valid >= c:
            return ref[bi, r0:r0 + c, lo:lo + w]
        t = ref[bi, :, lo:lo + w]
        return jnp.concatenate([t, jnp.zeros((c - t_valid, w), t.dtype)], axis=0)

    def pairwise_scores(qs, k, b):
        def score_row(i, a_t):
            pick = row == i
            q_i = jnp.sum(jnp.where(pick, qs, 0.0), axis=0, keepdims=True)
            b_i = jnp.sum(jnp.where(pick, b, 0.0), axis=0, keepdims=True)
            rel = jnp.where(row <= i, b_i - b, -jnp.inf)
            s_col = jnp.sum(k * jnp.exp(rel) * q_i, axis=1, keepdims=True)
            return jnp.where(col == i, s_col, a_t)
        return lax.fori_loop(0, c, score_row, jnp.zeros((c, c), F32)).T

    def run(factored):
        pairs = [(bi, h) for bi in range(bb) for h in range(GLA_HEADS)]
        groups = [pairs[g0:g0 + GLA_HEAD_GROUP] for g0 in range(0, len(pairs), GLA_HEAD_GROUP)]
        for ci in range(n_chunks):
            r0 = ci * c
            for hs in groups:
                q = [load(q_ref, bi, r0, h * GLA_DK, GLA_DK) for bi, h in hs]
                k = [load(k_ref, bi, r0, h * GLA_DK, GLA_DK) for bi, h in hs]
                v = [load(v_ref, bi, r0, h * GLA_DV, GLA_DV).astype(BF16) for bi, h in hs]
                la = [load(la_ref, bi, r0, h * GLA_DK, GLA_DK) for bi, h in hs]
                la_hi = [x.astype(BF16) for x in la]
                la_lo = [(x - y.astype(F32)).astype(BF16) for x, y in zip(la, la_hi)]
                b = [_dot(tril, x) + _dot(tril, y) for x, y in zip(la_hi, la_lo)]
                qs = [x * scale for x in q]
                qd = [(x * jnp.exp(y)).astype(BF16) for x, y in zip(qs, b)]
                if factored:
                    kn = [(x * jnp.exp(-y)).astype(BF16) for x, y in zip(k, b)]
                    a = [jnp.where(causal, _dot_nt(x, y), 0.0) for x, y in zip(qd, kn)]
                else:
                    a = [pairwise_scores(x, y, z) for x, y, z in zip(qs, k, b)]
                s_old = [s_ref[bi, h] for bi, h in hs]
                o = [_dot(x, s.astype(BF16)) + _dot(y.astype(BF16), z) for x, s, y, z in zip(qd, s_old, a, v)]
                for (bi, h), x in zip(hs, o):
                    if t_valid >= c:
                        o_ref[bi, r0:r0 + c, h * GLA_DV:(h + 1) * GLA_DV] = x.astype(o_ref.dtype)
                    else:
                        o_ref[bi, :, h * GLA_DV:(h + 1) * GLA_DV] = x[:t_valid].astype(o_ref.dtype)
                bl = [y[c - 1:c, :] for y in b]
                kd_t = [(x * jnp.exp(l - y)).T.astype(BF16) for x, l, y in zip(k, bl, b)]
                dec_t = [jnp.broadcast_to(jnp.exp(l), (c, GLA_DK)).T for l in bl]
                for (bi, h), d, s, x, z in zip(hs, dec_t, s_old, kd_t, v):
                    s_ref[bi, h] = jnp.concatenate([d] * (GLA_DV // c), axis=1) * s + _dot(x, z)

    worst = jnp.float32(0.0)
    for bi in range(bb):
        for ci in range(n_chunks):
            rows = slice(ci * c, (ci + 1) * c) if t_valid >= c else slice(None)
            worst = jnp.maximum(worst, jnp.max(-jnp.sum(la_ref[bi, rows, :], axis=0, keepdims=True)))
    safe = worst < GLA_SAFE_DECAY
    pl.when(safe)(lambda: run(True))
    pl.when(jnp.logical_not(safe))(lambda: run(False))

    @pl.when(j == pl.num_programs(1) - 1)
    def _():
        so_ref[...] = s_ref[...]


def _gla(gq, gk, gv, la, s0, tb, bb):
    bsz, t, _ = gq.shape
    t_valid = min(tb, GLA_CHUNK)
    n_chunks = max(tb // GLA_CHUNK, 1)
    has_s0 = s0 is not None
    tok = lambda w: pl.BlockSpec((bb, tb, w), lambda b, j: (b, j, 0))
    st = pl.BlockSpec((bb, GLA_HEADS, GLA_DK, GLA_DV), lambda b, j: (b, 0, 0, 0))
    in_specs = [tok(GLA_QK), tok(GLA_QK), tok(GLA_V), tok(GLA_QK)] + ([st] if has_s0 else [])
    args = (gq, gk, gv, la) + ((s0,) if has_s0 else ())
    return pl.pallas_call(
        functools.partial(_gla_kernel, bb=bb, n_chunks=n_chunks, t_valid=t_valid, has_s0=has_s0),
        grid=(bsz // bb, t // tb),
        in_specs=in_specs,
        out_specs=[tok(GLA_V), st],
        out_shape=[jax.ShapeDtypeStruct((bsz, t, GLA_V), gv.dtype),
                   jax.ShapeDtypeStruct((bsz, GLA_HEADS, GLA_DK, GLA_DV), F32)],
        scratch_shapes=[pltpu.VMEM((bb, GLA_HEADS, GLA_DK, GLA_DV), F32)],
        compiler_params=pltpu.CompilerParams(
            dimension_semantics=("parallel", "arbitrary"), vmem_limit_bytes=VMEM_LIMIT),
        name="gla",
    )(*args)


def _swa_kernel(sink_ref, q_ref, kc_ref, vc_ref, kp_ref, vp_ref, o_ref, s_ref, p_ref, *, bb, tq, nb, prev_is_block):
    i = pl.program_id(1)
    w = WINDOW
    nk = 2 * w
    pairs = SWA_GROUP // 2
    t_idx = lax.broadcasted_iota(jnp.int32, (tq, nk), 0)
    j_idx = lax.broadcasted_iota(jnp.int32, (tq, nk), 1)
    d = j_idx - t_idx
    band = (d >= 1) & (d <= w)
    lane = lax.broadcasted_iota(jnp.int32, (nk, LANES), 1)
    left = lane < SWA_HEAD_DIM

    def window(prev_ref, cur_ref, bi, blk):
        cur = cur_ref[bi, blk * tq:(blk + 1) * tq, :]
        if tq < w:
            cur = jnp.concatenate([cur, jnp.zeros((w - tq, SWA_KV), F32)], axis=0)
        prev = prev_ref[bi] if blk == 0 else cur_ref[bi, (blk - 1) * tq:blk * tq, :]
        full = jnp.concatenate([prev, cur], axis=0)
        return full, pltpu.roll(full, SWA_HEAD_DIM, 1)

    def block_diag(full, rot, kvh):
        a, b = (full, rot) if kvh == 0 else (rot, full)
        return jnp.concatenate([jnp.where(left, a, 0.0), jnp.where(left, 0.0, b)], axis=0).astype(BF16)

    def mask(blk):
        if prev_is_block and blk == 0:
            return band & ((j_idx >= w) | (i > 0))
        return band

    units = [((bi, blk), kvh) for bi in range(bb) for blk in range(nb) for kvh in range(SWA_KV_HEADS)]
    ug = s_ref.shape[0]
    win, vis = {}, {}
    for g0 in range(0, len(units), ug):
        grp = units[g0:g0 + ug]
        for blk, _ in grp:
            if blk not in win:
                win[blk] = (window(kp_ref, kc_ref, *blk), window(vp_ref, vc_ref, *blk))
                vis[blk] = mask(blk[1])
        kbd = [block_diag(*win[blk][0], kvh) for blk, kvh in grp]
        vbd = [block_diag(*win[blk][1], kvh) for blk, kvh in grp]
        base = [kvh * SWA_GROUP * SWA_HEAD_DIM for _, kvh in grp]
        rows = [(blk[0], slice(blk[1] * tq, (blk[1] + 1) * tq)) for blk, _ in grp]
        for gi, ((bi, r), b0) in enumerate(zip(rows, base)):
            qs = jnp.concatenate([q_ref[bi, r, b0 + p * LANES: b0 + (p + 1) * LANES] for p in range(pairs)], axis=0)
            s_ref[gi] = _dot_nt((qs * (SWA_HEAD_DIM ** -0.5)).astype(BF16), kbd[gi])
        for p in range(pairs):
            for e in range(2):
                pr, seg = slice(p * tq, (p + 1) * tq), slice(e * nk, (e + 1) * nk)
                se = [jnp.where(vis[blk], s_ref[gi, pr, seg], NEG_BIG) for gi, (blk, _) in enumerate(grp)]
                sink = [sink_ref[kvh * SWA_GROUP + 2 * p + e] for _, kvh in grp]
                m = [jnp.maximum(jnp.max(x, axis=-1, keepdims=True), y) for x, y in zip(se, sink)]
                ex = [jnp.exp(x - y) for x, y in zip(se, m)]
                den = [jnp.sum(x, axis=-1, keepdims=True) + jnp.exp(y - z) for x, y, z in zip(ex, sink, m)]
                for gi, (x, y) in enumerate(zip(ex, den)):
                    p_ref[gi, pr, seg] = (x / y).astype(BF16)
        for gi, ((bi, r), b0) in enumerate(zip(rows, base)):
            o = _dot(p_ref[gi], vbd[gi])
            for p in range(pairs):
                o_ref[bi, r, b0 + p * LANES: b0 + (p + 1) * LANES] = o[p * tq:(p + 1) * tq].astype(o_ref.dtype)


def _swa(sq, sk, sv, sinks, prev_k=None, prev_v=None, nb=1, bb=1):
    bsz, t, _ = sq.shape
    prev_is_block = prev_k is None
    if prev_is_block:
        tq = WINDOW
        prev_k, prev_v = sk, sv
        prev_spec = pl.BlockSpec((bb, WINDOW, SWA_KV), lambda b, i: (b, jnp.maximum(i * nb - 1, 0), 0))
    else:
        tq = t
        assert nb == 1
        prev_spec = pl.BlockSpec((bb, WINDOW, SWA_KV), lambda b, i: (b, 0, 0))
    ug = 2 * SWA_KV_HEADS
    rows = (SWA_GROUP // 2) * tq
    cur_spec = pl.BlockSpec((bb, nb * tq, SWA_KV), lambda b, i: (b, i, 0))
    q_spec = pl.BlockSpec((bb, nb * tq, SWA_Q), lambda b, i: (b, i, 0))
    return pl.pallas_call(
        functools.partial(_swa_kernel, bb=bb, tq=tq, nb=nb, prev_is_block=prev_is_block),
        grid=(bsz // bb, t // (nb * tq)),
        in_specs=[pl.BlockSpec(memory_space=pltpu.SMEM), q_spec, cur_spec, cur_spec, prev_spec, prev_spec],
        out_specs=q_spec,
        out_shape=jax.ShapeDtypeStruct((bsz, t, SWA_Q), sq.dtype),
        scratch_shapes=[pltpu.VMEM((ug, rows, 4 * WINDOW), F32), pltpu.VMEM((ug, rows, 4 * WINDOW), BF16)],
        compiler_params=pltpu.CompilerParams(
            dimension_semantics=("parallel", "arbitrary"), vmem_limit_bytes=VMEM_LIMIT),
        name="swa",
    )(sinks, sq, sk, sv, prev_k, prev_v)


def _merge_kernel(x_ref, oa_ref, gg_ref, ob_ref, ga_ref, gb_ref, gn_ref, wa_ref, wb_ref, wo_ref, h_ref):
    parts = []
    for hd in range(GLA_HEADS):
        sl = slice(hd * GLA_DV, (hd + 1) * GLA_DV)
        g = gg_ref[:, sl].astype(F32)
        parts.append(_rms(oa_ref[:, sl].astype(F32), gn_ref[...]) * (g * _sigmoid(g)))
    oa = jnp.concatenate(parts, axis=1).astype(BF16)
    br_a = _dot(oa, wa_ref[...])
    br_b = _dot(ob_ref[...].astype(BF16), wb_ref[...])
    merged = _sigmoid(ga_ref[...].astype(F32)) * br_a + _sigmoid(gb_ref[...].astype(F32)) * br_b
    h_ref[...] = x_ref[...] + _dot(merged.astype(BF16), wo_ref[...])


def _merge(x2d, oa, gg, ob, ga, gb, gn, wa, wb, wo, tn):
    n = x2d.shape[0]
    tok = pl.BlockSpec((tn, D_MODEL), lambda i: (i, 0))
    const = lambda i: (0, 0)
    wspec = pl.BlockSpec((D_MODEL, D_MODEL), const)
    return pl.pallas_call(
        _merge_kernel,
        grid=(n // tn,),
        in_specs=[tok] * 6 + [pl.BlockSpec((1, GLA_DV), const), wspec, wspec, wspec],
        out_specs=tok,
        out_shape=jax.ShapeDtypeStruct((n, D_MODEL), F32),
        compiler_params=pltpu.CompilerParams(
            dimension_semantics=("parallel",), vmem_limit_bytes=VMEM_LIMIT),
        name="merge",
    )(x2d, oa, gg, ob, ga, gb, gn, wa, wb, wo)


def _sort16_pairs():
    pairs = []
    n = 16
    p = 1
    while p < n:
        k = p
        while k >= 1:
            for j in range(k % p, n - k, 2 * k):
                for i in range(min(k, n - j - k)):
                    if (i + j) // (2 * p) == (i + j + k) // (2 * p):
                        pairs.append((i + j, i + j + k))
            k //= 2
        p *= 2
    return pairs


_SORT16 = _sort16_pairs()


def _bitonic_top16(xa, xb):
    z = [jnp.maximum(xa[i], xb[15 - i]) for i in range(16)]
    dist = 8
    while dist >= 1:
        for i in range(16):
            if i & dist == 0:
                hi = jnp.maximum(z[i], z[i + dist])
                lo = jnp.minimum(z[i], z[i + dist])
                z[i], z[i + dist] = hi, lo
        dist //= 2
    return z


def _top16_rows(s):
    x = [s[SUBLANES * v:SUBLANES * (v + 1), :] for v in range(16)]
    for i, j in _SORT16:
        hi = jnp.maximum(x[i], x[j])
        lo = jnp.minimum(x[i], x[j])
        x[i], x[j] = hi, lo
    shift = 1
    while shift < SUBLANES:
        x = _bitonic_top16(x, [pltpu.roll(t, shift, 0) for t in x])
        shift *= 2
    return [t[SUBLANES - 1:SUBLANES, :] for t in x]


_CAND_Q = [PEER_TOPK // (p + 1) for p in range(PEER_TOPK)]


PACK = 2 * SUBLANES
_HS_TAU, _HS_ZINV, _HS_TOP1, _HS_ROWS = PEER_TOPK, PEER_TOPK + 1, PEER_TOPK + 2, PEER_TOPK + SUBLANES


def _peer_kernel(h_ref, nf_ref, nfin_ref, wq_ref, k1_ref, k2_ref, u_ref, vt_ref, y_ref,
                 hn_ref, sa_ref, sb_ref, e1_ref, e2_ref, v1_ref, v2_ref, hs_ref,
                 ca_ref, ea_ref, ht_ref, g_ref, acc_ref, *, tn, ec):
    e = pl.program_id(1)
    na = ec // PEER_NKEYS
    nt = tn // LANES
    rb = 4 * SUBLANES

    @pl.when(e == 0)
    def _routing():
        hn_ref[...] = _rms(h_ref[...], nf_ref[...]).astype(BF16)
        acc_ref[...] = jnp.zeros_like(acc_ref)

        ht_ref[0:PEER_HEADS * PEER_QDIM, :] = _dot_nt(wq_ref[...], hn_ref[...])
        for hd in range(PEER_HEADS):
            qt = ht_ref[hd * PEER_QDIM:(hd + 1) * PEER_QDIM, :]
            s1 = _dot(k1_ref[hd], qt[:PEER_HALF].astype(BF16))
            s2 = _dot(k2_ref[hd], qt[PEER_HALF:].astype(BF16))
            sa_ref[hd] = s1
            sb_ref[hd] = s2
            for p, r in enumerate(_top16_rows(s1)):
                v1_ref[p, hd:hd + 1, :] = r
            for p, r in enumerate(_top16_rows(s2)):
                v2_ref[p, hd:hd + 1, :] = r
                hs_ref[hd, p:p + 1, :] = r

        v1 = [v1_ref[p] for p in range(PEER_TOPK)]
        v2 = [v2_ref[q] for q in range(PEER_TOPK)]
        ninf = jnp.full((PEER_HEADS, tn), -jnp.inf, F32)
        best = [v1[0] + v2[q] for q in range(PEER_TOPK)]
        for p in range(1, PEER_TOPK):
            row = [v1[p] + v2[q] if q < _CAND_Q[p] else ninf for q in range(PEER_TOPK)]
            best = _bitonic_top16(best, row)
        tau = best[PEER_TOPK - 1]
        top = best[0]
        zsum = jnp.zeros((PEER_HEADS, tn), F32)
        for p in range(PEER_TOPK):
            for q in range(_CAND_Q[p]):
                cnd = v1[p] + v2[q]
                zsum = zsum + jnp.where(cnd >= tau, jnp.exp(cnd - top), 0.0)
        zinv = 1.0 / zsum
        for hd in range(PEER_HEADS):
            hs_ref[hd, _HS_TAU:_HS_TAU + 1, :] = tau[hd:hd + 1]
            hs_ref[hd, _HS_ZINV:_HS_ZINV + 1, :] = zinv[hd:hd + 1]
            hs_ref[hd, _HS_TOP1:_HS_TOP1 + 1, :] = v1[0][hd:hd + 1]

        def staircase(idx, carry):
            hd = idx // (PEER_NKEYS // rb)
            rows = pl.ds(pl.multiple_of((idx % (PEER_NKEYS // rb)) * rb, rb), rb)
            s1 = sa_ref[hd, rows, :]
            tau_h = hs_ref[hd, _HS_TAU:_HS_TAU + 1, :]
            theta = jnp.full((rb, tn), jnp.inf, F32)
            for q in range(PEER_TOPK):
                v2q = hs_ref[hd, q:q + 1, :]
                theta = jnp.where(s1 + v2q >= tau_h, v2q, theta)
            e2_ref[hd, rows, :] = (jnp.exp(sb_ref[hd, rows, :] - hs_ref[hd, 0:1, :])
                                   * hs_ref[hd, _HS_ZINV:_HS_ZINV + 1, :])
            e1_ref[hd, rows, :] = 0.5 * jnp.exp(s1 - hs_ref[hd, _HS_TOP1:_HS_TOP1 + 1, :])
            sa_ref[hd, rows, :] = theta
            return carry

        lax.fori_loop(0, PEER_HEADS * (PEER_NKEYS // rb), staircase, 0)

    ht_ref[...] = _dot_nt(u_ref[...], hn_ref[...])
    a_rows = pl.ds(pl.multiple_of(e * na, na), na)
    for hd in range(PEER_HEADS):
        ca_ref[hd] = sa_ref[hd, a_rows, :]
        ea_ref[hd] = e1_ref[hd, a_rows, :]

    ga = 4
    gb = 4
    n_bg = PEER_NKEYS // (gb * SUBLANES)

    def gate(idx, carry):
        cols = pl.ds(pl.multiple_of((idx // n_bg) * LANES, LANES), LANES)
        b0 = pl.multiple_of((idx % n_bg) * (gb * SUBLANES), gb * SUBLANES)
        bc = lambda ref, hd, r: jnp.broadcast_to(ref[hd, r:r + 1, cols], (SUBLANES, LANES))
        for a_blk in range(0, na, ga):
            w = [[jnp.zeros((SUBLANES, LANES), F32) for _ in range(gb)] for _ in range(ga)]
            for hd in range(PEER_HEADS):
                s2 = [sb_ref[hd, pl.ds(b0 + t * SUBLANES, SUBLANES), cols] for t in range(gb)]
                e2 = [e2_ref[hd, pl.ds(b0 + t * SUBLANES, SUBLANES), cols] for t in range(gb)]
                for i in range(ga):
                    theta_b = bc(ca_ref, hd, a_blk + i)
                    e1_b = bc(ea_ref, hd, a_blk + i)
                    for t in range(gb):
                        w[i][t] = w[i][t] + jnp.where(s2[t] >= theta_b, e2[t], 0.0) * e1_b
            for i in range(ga):
                for t in range(0, gb, 2):
                    rows = pl.ds(pl.multiple_of((a_blk + i) * PEER_NKEYS + b0 + t * SUBLANES, PACK), PACK)
                    hx = ht_ref[rows, cols]
                    act = hx * (1.0 + lax.erf(hx * (2.0 ** -0.5)))
                    g_ref[rows, cols] = (jnp.concatenate([w[i][t], w[i][t + 1]], axis=0) * act).astype(BF16)
        return carry

    lax.fori_loop(0, nt * n_bg, gate, 0)
    acc_ref[...] += _dot(vt_ref[...], g_ref[...])

    @pl.when(e == pl.num_programs(1) - 1)
    def _finish():
        y_ref[...] = _rms(h_ref[...] + acc_ref[...].T, nfin_ref[...])


def _peer(h2d, nf, nfin, wq_t, k1, k2, u, vt, tn, ec):
    n = h2d.shape[0]
    assert ec % (SUBLANES * PEER_NKEYS) == 0 and tn % LANES == 0
    assert ec >= PEER_HEADS * PEER_QDIM
    tok = pl.BlockSpec((tn, D_MODEL), lambda i, e: (i, 0))
    const2 = lambda i, e: (0, 0)
    const3 = lambda i, e: (0, 0, 0)
    vec = pl.BlockSpec((1, D_MODEL), const2)
    kspec = pl.BlockSpec((PEER_HEADS, PEER_NKEYS, PEER_HALF), const3)
    head_f32 = pltpu.VMEM((PEER_HEADS, PEER_NKEYS, tn), F32)
    rank_tok = pltpu.VMEM((PEER_TOPK, PEER_HEADS, tn), F32)
    stage = pltpu.VMEM((PEER_HEADS, ec // PEER_NKEYS, tn), F32)
    return pl.pallas_call(
        functools.partial(_peer_kernel, tn=tn, ec=ec),
        grid=(n // tn, PEER_N_EXPERTS // ec),
        in_specs=[tok, vec, vec,
                  pl.BlockSpec((PEER_HEADS * PEER_QDIM, D_MODEL), const2, pipeline_mode=pl.Buffered(1)),
                  kspec, kspec,
                  pl.BlockSpec((ec, D_MODEL), lambda i, e: (e, 0)),
                  pl.BlockSpec((D_MODEL, ec), lambda i, e: (0, e))],
        out_specs=tok,
        out_shape=jax.ShapeDtypeStruct((n, D_MODEL), F32),
        scratch_shapes=[pltpu.VMEM((tn, D_MODEL), BF16),
                        head_f32, head_f32, head_f32, head_f32, rank_tok, rank_tok,
                        pltpu.VMEM((PEER_HEADS, _HS_ROWS, tn), F32),
                        stage, stage, pltpu.VMEM((ec, tn), F32), pltpu.VMEM((ec, tn), BF16),
                        pltpu.VMEM((D_MODEL, tn), F32)],
        compiler_params=pltpu.CompilerParams(
            dimension_semantics=("parallel", "arbitrary"), vmem_limit_bytes=VMEM_LIMIT),
        name="peer",
    )(h2d, nf, nfin, wq_t, k1, k2, u, vt)


def _pack_params(norm_mix, w_in, gla_w_decay, gla_b_decay, gla_norm, swa_sinks, w_branch_a, w_branch_b,
                 w_out, norm_ffn, peer_w_q, peer_keys1, peer_keys2, peer_u, peer_v):
    off = np.cumsum((0, GLA_QK, GLA_QK, GLA_V, GLA_V, GLA_RANK, SWA_Q, SWA_KV, SWA_KV, D_MODEL, D_MODEL))
    col = lambda i: w_in[:, off[i]:off[i + 1]]
    lr = jnp.pad(col(4), ((0, 0), (0, LANES - GLA_RANK)))
    w_all = jnp.concatenate([col(0), col(1), col(2), col(3), col(5), col(6), col(7), col(8), col(9), lr],
                            axis=1).astype(BF16)
    wdec = jnp.pad(gla_w_decay, ((0, LANES - GLA_RANK), (0, 0))).astype(BF16)
    return dict(
        nm=norm_mix[None], w_all=w_all, wdec=wdec, bdec=gla_b_decay[None], gn=gla_norm[None],
        sinks=swa_sinks, wa=w_branch_a.astype(BF16), wb=w_branch_b.astype(BF16), wo=w_out.astype(BF16),
        nf=norm_ffn[None], wq_t=peer_w_q.T.astype(BF16), k1=peer_keys1.astype(BF16),
        k2=peer_keys2.astype(BF16), u=peer_u.astype(BF16), vt=peer_v.T.astype(BF16))


def _layer(x, p, nfin, s0, win_k, win_v, *, tn, act_dtype, gla_tb, gla_bb, swa_nb, swa_bb, peer_tn, peer_ec):
    bsz, t, _ = x.shape
    x2d = x.reshape(bsz * t, D_MODEL)
    gq, gk, gv, gg, la, sq, sk, sv, ga, gb = _inproj(x2d, p["nm"], p["w_all"], p["wdec"], p["bdec"], tn, act_dtype)
    r3 = lambda a: a.reshape(bsz, t, a.shape[-1])
    o_a, s_new = _gla(r3(gq), r3(gk), r3(gv), r3(la), s0, gla_tb, gla_bb)
    if win_k is None:
        o_b = _swa(r3(sq), r3(sk), r3(sv), p["sinks"], nb=swa_nb, bb=swa_bb)
        ln = min(WINDOW, t)
        new_k, new_v = r3(sk)[:, t - ln:], r3(sv)[:, t - ln:]
    else:
        wk = win_k.reshape(bsz, WINDOW, SWA_KV)
        wv = win_v.reshape(bsz, WINDOW, SWA_KV)
        o_b = _swa(r3(sq), r3(sk), r3(sv), p["sinks"], wk, wv, bb=swa_bb)
        new_k = jnp.concatenate([wk, r3(sk)], axis=1)[:, -WINDOW:]
        new_v = jnp.concatenate([wv, r3(sv)], axis=1)[:, -WINDOW:]
    h = _merge(x2d, o_a.reshape(bsz * t, GLA_V), gg, o_b.reshape(bsz * t, SWA_Q), ga, gb,
               p["gn"], p["wa"], p["wb"], p["wo"], tn)
    y = _peer(h, p["nf"], nfin, p["wq_t"], p["k1"], p["k2"], p["u"], p["vt"], peer_tn, peer_ec)
    kv_shape = (bsz, -1, SWA_KV_HEADS, SWA_HEAD_DIM)
    return y.reshape(bsz, t, D_MODEL), new_k.reshape(kv_shape), new_v.reshape(kv_shape), s_new


def kernel(x_prompt, x_sample, cache_win_k, cache_win_v, state_gla, norm_mix, w_in, gla_w_decay, gla_b_decay,
           gla_norm, swa_sinks, w_branch_a, w_branch_b, w_out, norm_ffn, peer_w_q, peer_keys1, peer_keys2,
           peer_u, peer_v, norm_final):
    depth = w_in.shape[0]
    assert depth == 1, "the final norm is fused into the last layer's channel mixer; one layer supported"
    p = _pack_params(norm_mix[0], w_in[0], gla_w_decay[0], gla_b_decay[0], gla_norm[0], swa_sinks[0],
                     w_branch_a[0], w_branch_b[0], w_out[0], norm_ffn[0], peer_w_q[0], peer_keys1[0],
                     peer_keys2[0], peer_u[0], peer_v[0])
    nfin = norm_final[None]
    yp, pk, pv, ps = _layer(x_prompt, p, nfin, None, None, None,
                            tn=512, act_dtype=BF16, gla_tb=512, gla_bb=1, swa_nb=4, swa_bb=1, peer_tn=512, peer_ec=2048)
    ys, sk, sv, ss = _layer(x_sample, p, nfin, state_gla[0], cache_win_k[0], cache_win_v[0],
                            tn=512, act_dtype=F32, gla_tb=x_sample.shape[1], gla_bb=4, swa_nb=1, swa_bb=4, peer_tn=512, peer_ec=2048)
    return (yp, ys, pk[None], pv[None], ps[None], sk[None], sv[None], ss[None])
```

```python
import functools

import jax
import jax.numpy as jnp
import numpy as np
from jax import lax
from jax.experimental import pallas as pl
from jax.experimental.pallas import tpu as pltpu

F32 = jnp.float32
BF16 = jnp.bfloat16

D_MODEL = 1024
GLA_HEADS = 4
GLA_DK = 128
GLA_DV = 256
GLA_RANK = 16
GLA_TAU = 16.0
SWA_HEADS = 16
SWA_KV_HEADS = 2
SWA_GROUP = SWA_HEADS // SWA_KV_HEADS
SWA_HEAD_DIM = 64
WINDOW = 128
PEER_HEADS = 8
PEER_NKEYS = 128
PEER_N_EXPERTS = PEER_NKEYS * PEER_NKEYS
PEER_QDIM = 256
PEER_HALF = PEER_QDIM // 2
PEER_TOPK = 16
EPS = 1e-6

GLA_QK = GLA_HEADS * GLA_DK
GLA_V = GLA_HEADS * GLA_DV
SWA_Q = SWA_HEADS * SWA_HEAD_DIM
SWA_KV = SWA_KV_HEADS * SWA_HEAD_DIM

LANES = 128
SUBLANES = 8
GLA_CHUNK = 128
GLA_HEAD_GROUP = 4
GLA_SAFE_DECAY = 60.0
VMEM_LIMIT = 56 * 1024 * 1024
NEG_BIG = -1e30

_NT = (((1,), (1,)), ((), ()))


def _rms(x, g):
    return x * lax.rsqrt(jnp.mean(x * x, axis=-1, keepdims=True) + EPS) * g


def _dot(a, b):
    return jnp.dot(a, b, preferred_element_type=F32)


def _dot_nt(a, b):
    return lax.dot_general(a, b, _NT, preferred_element_type=F32)


def _sigmoid(x):
    return 1.0 / (1.0 + jnp.exp(-x))


_P_GQ, _P_GK, _P_GV, _P_GG, _P_SQ, _P_SK, _P_SV, _P_GA, _P_GB, _P_LR, _P_END = (
    0, 512, 1024, 2048, 3072, 4096, 4224, 4352, 5376, 6400, 6528)


def _inproj_kernel(x_ref, nm_ref, w_ref, wdec_ref, bdec_ref,
                   gq_ref, gk_ref, gv_ref, gg_ref, la_ref, sq_ref, sk_ref, sv_ref, ga_ref, gb_ref):
    xb = _rms(x_ref[...], nm_ref[...]).astype(BF16)

    def proj(lo, hi):
        return _dot(xb, w_ref[:, lo:hi])

    gq_ref[...] = proj(_P_GQ, _P_GK)
    gk_ref[...] = proj(_P_GK, _P_GV)
    gv_ref[...] = proj(_P_GV, _P_GG).astype(gv_ref.dtype)
    gg_ref[...] = proj(_P_GG, _P_SQ).astype(gg_ref.dtype)
    sq_ref[...] = proj(_P_SQ, _P_SK).astype(sq_ref.dtype)
    sk_ref[...] = proj(_P_SK, _P_SV)
    sv_ref[...] = proj(_P_SV, _P_GA)
    ga_ref[...] = proj(_P_GA, _P_GB).astype(ga_ref.dtype)
    gb_ref[...] = proj(_P_GB, _P_LR).astype(gb_ref.dtype)
    glr = proj(_P_LR, _P_END)
    z = _dot(glr.astype(BF16), wdec_ref[...]) + bdec_ref[...]
    la_ref[...] = (jnp.minimum(z, 0.0) - jnp.log(1.0 + jnp.exp(-jnp.abs(z)))) * (1.0 / GLA_TAU)


def _inproj(x2d, nm, w_all, wdec, bdec, tn, act_dtype):
    n = x2d.shape[0]
    widths = (GLA_QK, GLA_QK, GLA_V, GLA_V, GLA_QK, SWA_Q, SWA_KV, SWA_KV, D_MODEL, D_MODEL)
    dtypes = (F32, F32, act_dtype, act_dtype, F32, act_dtype, F32, F32, act_dtype, act_dtype)
    const = lambda i: (0, 0)
    return pl.pallas_call(
        _inproj_kernel,
        grid=(n // tn,),
        in_specs=[
            pl.BlockSpec((tn, D_MODEL), lambda i: (i, 0)),
            pl.BlockSpec((1, D_MODEL), const),
            pl.BlockSpec((D_MODEL, _P_END), const, pipeline_mode=pl.Buffered(1)),
            pl.BlockSpec((LANES, GLA_QK), const),
            pl.BlockSpec((1, GLA_QK), const),
        ],
        out_specs=[pl.BlockSpec((tn, w), lambda i: (i, 0)) for w in widths],
        out_shape=[jax.ShapeDtypeStruct((n, w), d) for w, d in zip(widths, dtypes)],
        compiler_params=pltpu.CompilerParams(
            dimension_semantics=("parallel",), vmem_limit_bytes=VMEM_LIMIT),
        name="in_proj",
    )(x2d, nm, w_all, wdec, bdec)


def _gla_kernel(*refs, bb, n_chunks, t_valid, has_s0):
    if has_s0:
        q_ref, k_ref, v_ref, la_ref, s0_ref, o_ref, so_ref, s_ref = refs
    else:
        q_ref, k_ref, v_ref, la_ref, o_ref, so_ref, s_ref = refs
    j = pl.program_id(1)
    c = GLA_CHUNK

    @pl.when(j == 0)
    def _():
        if has_s0:
            s_ref[...] = s0_ref[...]
        else:
            s_ref[...] = jnp.zeros_like(s_ref)

    row = lax.broadcasted_iota(jnp.int32, (c, c), 0)
    col = lax.broadcasted_iota(jnp.int32, (c, c), 1)
    causal = row >= col
    tril = causal.astype(BF16)
    scale = GLA_DK ** -0.5

    def load(ref, bi, r0, lo, w):
        if t_valid >= c:
            return ref[bi, r0:r0 + c, lo:lo + w]
        t = ref[bi, :, lo:lo + w]
        return jnp.concatenate([t, jnp.zeros((c - t_valid, w), t.dtype)], axis=0)

    def pairwise_scores(qs, k, b):
        def score_row(i, a_t):
            pick = row == i
            q_i = jnp.sum(jnp.where(pick, qs, 0.0), axis=0, keepdims=True)
            b_i = jnp.sum(jnp.where(pick, b, 0.0), axis=0, keepdims=True)
            rel = jnp.where(row <= i, b_i - b, -jnp.inf)
            s_col = jnp.sum(k * jnp.exp(rel) * q_i, axis=1, keepdims=True)
            return jnp.where(col == i, s_col, a_t)
        return lax.fori_loop(0, c, score_row, jnp.zeros((c, c), F32)).T

    def run(factored):
        pairs = [(bi, h) for bi in range(bb) for h in range(GLA_HEADS)]
        groups = [pairs[g0:g0 + GLA_HEAD_GROUP] for g0 in range(0, len(pairs), GLA_HEAD_GROUP)]
        for ci in range(n_chunks):
            r0 = ci * c
            for hs in groups:
                q = [load(q_ref, bi, r0, h * GLA_DK, GLA_DK) for bi, h in hs]
                k = [load(k_ref, bi, r0, h * GLA_DK, GLA_DK) for bi, h in hs]
                v = [load(v_ref, bi, r0, h * GLA_DV, GLA_DV).astype(BF16) for bi, h in hs]
                la = [load(la_ref, bi, r0, h * GLA_DK, GLA_DK) for bi, h in hs]
                la_hi = [x.astype(BF16) for x in la]
                la_lo = [(x - y.astype(F32)).astype(BF16) for x, y in zip(la, la_hi)]
                b2 = [_dot(tril, jnp.concatenate([x, y], axis=1)) for x, y in zip(la_hi, la_lo)]
                b = [x[:, :GLA_DK] + x[:, GLA_DK:] for x in b2]
                qs = [x * scale for x in q]
                qd = [(x * jnp.exp(y)).astype(BF16) for x, y in zip(qs, b)]
                if factored:
                    kn = [(x * jnp.exp(-y)).astype(BF16) for x, y in zip(k, b)]
                    a = [jnp.where(causal, _dot_nt(x, y), 0.0) for x, y in zip(qd, kn)]
                else:
                    a = [pairwise_scores(x, y, z) for x, y, z in zip(qs, k, b)]
                s_old = [s_ref[bi, h] for bi, h in hs]
                o = [_dot(jnp.concatenate([x, y.astype(BF16)], axis=1), jnp.concatenate([s.astype(BF16), z], axis=0))
                     for x, s, y, z in zip(qd, s_old, a, v)]
                for (bi, h), x in zip(hs, o):
                    if t_valid >= c:
                        o_ref[bi, r0:r0 + c, h * GLA_DV:(h + 1) * GLA_DV] = x.astype(o_ref.dtype)
                    else:
                        o_ref[bi, :, h * GLA_DV:(h + 1) * GLA_DV] = x[:t_valid].astype(o_ref.dtype)
                bl = [y[c - 1:c, :] for y in b]
                kd_t = [(x * jnp.exp(l - y)).T.astype(BF16) for x, l, y in zip(k, bl, b)]
                dec_t = [jnp.broadcast_to(jnp.exp(l), (c, GLA_DK)).T for l in bl]
                for (bi, h), d, s, x, z in zip(hs, dec_t, s_old, kd_t, v):
                    s_ref[bi, h] = jnp.concatenate([d] * (GLA_DV // c), axis=1) * s + _dot(x, z)

    worst = jnp.float32(0.0)
    for bi in range(bb):
        for ci in range(n_chunks):
            rows = slice(ci * c, (ci + 1) * c) if t_valid >= c else slice(None)
            worst = jnp.maximum(worst, jnp.max(-jnp.sum(la_ref[bi, rows, :], axis=0, keepdims=True)))
    safe = worst < GLA_SAFE_DECAY
    pl.when(safe)(lambda: run(True))
    pl.when(jnp.logical_not(safe))(lambda: run(False))

    @pl.when(j == pl.num_programs(1) - 1)
    def _():
        so_ref[...] = s_ref[...]


def _gla(gq, gk, gv, la, s0, tb, bb):
    bsz, t, _ = gq.shape
    t_valid = min(tb, GLA_CHUNK)
    n_chunks = max(tb // GLA_CHUNK, 1)
    has_s0 = s0 is not None
    tok = lambda w: pl.BlockSpec((bb, tb, w), lambda b, j: (b, j, 0))
    st = pl.BlockSpec((bb, GLA_HEADS, GLA_DK, GLA_DV), lambda b, j: (b, 0, 0, 0))
    in_specs = [tok(GLA_QK), tok(GLA_QK), tok(GLA_V), tok(GLA_QK)] + ([st] if has_s0 else [])
    args = (gq, gk, gv, la) + ((s0,) if has_s0 else ())
    return pl.pallas_call(
        functools.partial(_gla_kernel, bb=bb, n_chunks=n_chunks, t_valid=t_valid, has_s0=has_s0),
        grid=(bsz // bb, t // tb),
        in_specs=in_specs,
        out_specs=[tok(GLA_V), st],
        out_shape=[jax.ShapeDtypeStruct((bsz, t, GLA_V), gv.dtype),
                   jax.ShapeDtypeStruct((bsz, GLA_HEADS, GLA_DK, GLA_DV), F32)],
        scratch_shapes=[pltpu.VMEM((bb, GLA_HEADS, GLA_DK, GLA_DV), F32)],
        compiler_params=pltpu.CompilerParams(
            dimension_semantics=("parallel", "arbitrary"), vmem_limit_bytes=VMEM_LIMIT),
        name="gla",
    )(*args)


def _swa_kernel(sink_ref, q_ref, kc_ref, vc_ref, kp_ref, vp_ref, o_ref, s_ref, p_ref, *, bb, tq, nb, prev_is_block):
    i = pl.program_id(1)
    w = WINDOW
    nk = 2 * w
    pairs = SWA_GROUP // 2
    t_idx = lax.broadcasted_iota(jnp.int32, (tq, nk), 0)
    j_idx = lax.broadcasted_iota(jnp.int32, (tq, nk), 1)
    d = j_idx - t_idx
    band = (d >= 1) & (d <= w)
    lane = lax.broadcasted_iota(jnp.int32, (nk, LANES), 1)
    left = lane < SWA_HEAD_DIM

    def window(prev_ref, cur_ref, bi, blk):
        cur = cur_ref[bi, blk * tq:(blk + 1) * tq, :]
        if tq < w:
            cur = jnp.concatenate([cur, jnp.zeros((w - tq, SWA_KV), F32)], axis=0)
        prev = prev_ref[bi] if blk == 0 else cur_ref[bi, (blk - 1) * tq:blk * tq, :]
        full = jnp.concatenate([prev, cur], axis=0)
        return full, pltpu.roll(full, SWA_HEAD_DIM, 1)

    def block_diag(full, rot, kvh):
        a, b = (full, rot) if kvh == 0 else (rot, full)
        return jnp.concatenate([jnp.where(left, a, 0.0), jnp.where(left, 0.0, b)], axis=0).astype(BF16)

    def mask(blk):
        if prev_is_block and blk == 0:
            return band & ((j_idx >= w) | (i > 0))
        return band

    units = [((bi, blk), kvh) for bi in range(bb) for blk in range(nb) for kvh in range(SWA_KV_HEADS)]
    ug = s_ref.shape[0]
    win, vis = {}, {}
    for g0 in range(0, len(units), ug):
        grp = units[g0:g0 + ug]
        for blk, _ in grp:
            if blk not in win:
                win[blk] = (window(kp_ref, kc_ref, *blk), window(vp_ref, vc_ref, *blk))
                vis[blk] = mask(blk[1])
        kbd = [block_diag(*win[blk][0], kvh) for blk, kvh in grp]
        vbd = [block_diag(*win[blk][1], kvh) for blk, kvh in grp]
        base = [kvh * SWA_GROUP * SWA_HEAD_DIM for _, kvh in grp]
        rows = [(blk[0], slice(blk[1] * tq, (blk[1] + 1) * tq)) for blk, _ in grp]
        for gi, ((bi, r), b0) in enumerate(zip(rows, base)):
            qs = jnp.concatenate([q_ref[bi, r, b0 + p * LANES: b0 + (p + 1) * LANES] for p in range(pairs)], axis=0)
            s_ref[gi] = _dot_nt((qs * (SWA_HEAD_DIM ** -0.5)).astype(BF16), kbd[gi])
        for p in range(pairs):
            for e in range(2):
                pr, seg = slice(p * tq, (p + 1) * tq), slice(e * nk, (e + 1) * nk)
                se = [jnp.where(vis[blk], s_ref[gi, pr, seg], NEG_BIG) for gi, (blk, _) in enumerate(grp)]
                sink = [sink_ref[kvh * SWA_GROUP + 2 * p + e] for _, kvh in grp]
                m = [jnp.maximum(jnp.max(x, axis=-1, keepdims=True), y) for x, y in zip(se, sink)]
                ex = [jnp.exp(x - y) for x, y in zip(se, m)]
                den = [jnp.sum(x, axis=-1, keepdims=True) + jnp.exp(y - z) for x, y, z in zip(ex, sink, m)]
                for gi, (x, y) in enumerate(zip(ex, den)):
                    p_ref[gi, pr, seg] = (x / y).astype(BF16)
        for gi, ((bi, r), b0) in enumerate(zip(rows, base)):
            o = _dot(p_ref[gi], vbd[gi])
            for p in range(pairs):
                o_ref[bi, r, b0 + p * LANES: b0 + (p + 1) * LANES] = o[p * tq:(p + 1) * tq].astype(o_ref.dtype)


def _swa(sq, sk, sv, sinks, prev_k=None, prev_v=None, nb=1, bb=1):
    bsz, t, _ = sq.shape
    prev_is_block = prev_k is None
    if prev_is_block:
        tq = WINDOW
        prev_k, prev_v = sk, sv
        prev_spec = pl.BlockSpec((bb, WINDOW, SWA_KV), lambda b, i: (b, jnp.maximum(i * nb - 1, 0), 0))
    else:
        tq = t
        assert nb == 1
        prev_spec = pl.BlockSpec((bb, WINDOW, SWA_KV), lambda b, i: (b, 0, 0))
    ug = 2 * SWA_KV_HEADS
    rows = (SWA_GROUP // 2) * tq
    cur_spec = pl.BlockSpec((bb, nb * tq, SWA_KV), lambda b, i: (b, i, 0))
    q_spec = pl.BlockSpec((bb, nb * tq, SWA_Q), lambda b, i: (b, i, 0))
    return pl.pallas_call(
        functools.partial(_swa_kernel, bb=bb, tq=tq, nb=nb, prev_is_block=prev_is_block),
        grid=(bsz // bb, t // (nb * tq)),
        in_specs=[pl.BlockSpec(memory_space=pltpu.SMEM), q_spec, cur_spec, cur_spec, prev_spec, prev_spec],
        out_specs=q_spec,
        out_shape=jax.ShapeDtypeStruct((bsz, t, SWA_Q), sq.dtype),
        scratch_shapes=[pltpu.VMEM((ug, rows, 4 * WINDOW), F32), pltpu.VMEM((ug, rows, 4 * WINDOW), BF16)],
        compiler_params=pltpu.CompilerParams(
            dimension_semantics=("parallel", "arbitrary"), vmem_limit_bytes=VMEM_LIMIT),
        name="swa",
    )(sinks, sq, sk, sv, prev_k, prev_v)


def _merge_kernel(x_ref, oa_ref, gg_ref, ob_ref, ga_ref, gb_ref, gn_ref, wa_ref, wb_ref, wo_ref, h_ref):
    parts = []
    for hd in range(GLA_HEADS):
        sl = slice(hd * GLA_DV, (hd + 1) * GLA_DV)
        g = gg_ref[:, sl].astype(F32)
        parts.append(_rms(oa_ref[:, sl].astype(F32), gn_ref[...]) * (g * _sigmoid(g)))
    oa = jnp.concatenate(parts, axis=1).astype(BF16)
    br_a = _dot(oa, wa_ref[...])
    br_b = _dot(ob_ref[...].astype(BF16), wb_ref[...])
    merged = _sigmoid(ga_ref[...].astype(F32)) * br_a + _sigmoid(gb_ref[...].astype(F32)) * br_b
    h_ref[...] = x_ref[...] + _dot(merged.astype(BF16), wo_ref[...])


def _merge(x2d, oa, gg, ob, ga, gb, gn, wa, wb, wo, tn):
    n = x2d.shape[0]
    tok = pl.BlockSpec((tn, D_MODEL), lambda i: (i, 0))
    const = lambda i: (0, 0)
    wspec = pl.BlockSpec((D_MODEL, D_MODEL), const)
    return pl.pallas_call(
        _merge_kernel,
        grid=(n // tn,),
        in_specs=[tok] * 6 + [pl.BlockSpec((1, GLA_DV), const), wspec, wspec, wspec],
        out_specs=tok,
        out_shape=jax.ShapeDtypeStruct((n, D_MODEL), F32),
        compiler_params=pltpu.CompilerParams(
            dimension_semantics=("parallel",), vmem_limit_bytes=VMEM_LIMIT),
        name="merge",
    )(x2d, oa, gg, ob, ga, gb, gn, wa, wb, wo)


def _sort16_pairs():
    pairs = []
    n = 16
    p = 1
    while p < n:
        k = p
        while k >= 1:
            for j in range(k % p, n - k, 2 * k):
                for i in range(min(k, n - j - k)):
                    if (i + j) // (2 * p) == (i + j + k) // (2 * p):
                        pairs.append((i + j, i + j + k))
            k //= 2
        p *= 2
    return pairs


_SORT16 = _sort16_pairs()


def _bitonic_top16(xa, xb):
    z = [jnp.maximum(xa[i], xb[15 - i]) for i in range(16)]
    dist = 8
    while dist >= 1:
        for i in range(16):
            if i & dist == 0:
                hi = jnp.maximum(z[i], z[i + dist])
                lo = jnp.minimum(z[i], z[i + dist])
                z[i], z[i + dist] = hi, lo
        dist //= 2
    return z


def _top16_rows(s):
    x = [s[SUBLANES * v:SUBLANES * (v + 1), :] for v in range(16)]
    for i, j in _SORT16:
        hi = jnp.maximum(x[i], x[j])
        lo = jnp.minimum(x[i], x[j])
        x[i], x[j] = hi, lo
    shift = 1
    while shift < SUBLANES:
        x = _bitonic_top16(x, [pltpu.roll(t, shift, 0) for t in x])
        shift *= 2
    return [t[SUBLANES - 1:SUBLANES, :] for t in x]


_CAND_Q = [PEER_TOPK // (p + 1) for p in range(PEER_TOPK)]


PACK = 2 * SUBLANES
_HS_TAU, _HS_ZINV, _HS_TOP1, _HS_ROWS = PEER_TOPK, PEER_TOPK + 1, PEER_TOPK + 2, PEER_TOPK + SUBLANES


def _peer_kernel(h_ref, nf_ref, nfin_ref, wq_ref, k1_ref, k2_ref, u_ref, vt_ref, y_ref,
                 hn_ref, sa_ref, sb_ref, e1_ref, e2_ref, v1_ref, v2_ref, hs_ref,
                 ca_ref, ea_ref, ht_ref, g_ref, acc_ref, *, tn, ec):
    e = pl.program_id(1)
    na = ec // PEER_NKEYS
    nt = tn // LANES
    rb = 4 * SUBLANES

    @pl.when(e == 0)
    def _routing():
        hn_ref[...] = _rms(h_ref[...], nf_ref[...]).astype(BF16)
        acc_ref[...] = jnp.zeros_like(acc_ref)

        ht_ref[0:PEER_HEADS * PEER_QDIM, :] = _dot_nt(wq_ref[...], hn_ref[...])
        for hd in range(PEER_HEADS):
            qt = ht_ref[hd * PEER_QDIM:(hd + 1) * PEER_QDIM, :]
            s1 = _dot(k1_ref[hd], qt[:PEER_HALF].astype(BF16))
            s2 = _dot(k2_ref[hd], qt[PEER_HALF:].astype(BF16))
            sa_ref[hd] = s1
            sb_ref[hd] = s2
            for p, r in enumerate(_top16_rows(s1)):
                v1_ref[p, hd:hd + 1, :] = r
            for p, r in enumerate(_top16_rows(s2)):
                v2_ref[p, hd:hd + 1, :] = r
                hs_ref[hd, p:p + 1, :] = r

        v1 = [v1_ref[p] for p in range(PEER_TOPK)]
        v2 = [v2_ref[q] for q in range(PEER_TOPK)]
        ninf = jnp.full((PEER_HEADS, tn), -jnp.inf, F32)
        best = [v1[0] + v2[q] for q in range(PEER_TOPK)]
        for p in range(1, PEER_TOPK):
            row = [v1[p] + v2[q] if q < _CAND_Q[p] else ninf for q in range(PEER_TOPK)]
            best = _bitonic_top16(best, row)
        tau = best[PEER_TOPK - 1]
        top = best[0]
        zsum = jnp.zeros((PEER_HEADS, tn), F32)
        for p in range(PEER_TOPK):
            for q in range(_CAND_Q[p]):
                cnd = v1[p] + v2[q]
                zsum = zsum + jnp.where(cnd >= tau, jnp.exp(cnd - top), 0.0)
        zinv = 1.0 / zsum
        for hd in range(PEER_HEADS):
            hs_ref[hd, _HS_TAU:_HS_TAU + 1, :] = tau[hd:hd + 1]
            hs_ref[hd, _HS_ZINV:_HS_ZINV + 1, :] = zinv[hd:hd + 1]
            hs_ref[hd, _HS_TOP1:_HS_TOP1 + 1, :] = v1[0][hd:hd + 1]

        def staircase(idx, carry):
            hd = idx // (PEER_NKEYS // rb)
            rows = pl.ds(pl.multiple_of((idx % (PEER_NKEYS // rb)) * rb, rb), rb)
            s1 = sa_ref[hd, rows, :]
            tau_h = hs_ref[hd, _HS_TAU:_HS_TAU + 1, :]
            theta = jnp.full((rb, tn), jnp.inf, F32)
            for q in range(PEER_TOPK):
                v2q = hs_ref[hd, q:q + 1, :]
                theta = jnp.where(s1 + v2q >= tau_h, v2q, theta)
            e2_ref[hd, rows, :] = (jnp.exp(sb_ref[hd, rows, :] - hs_ref[hd, 0:1, :])
                                   * hs_ref[hd, _HS_ZINV:_HS_ZINV + 1, :])
            e1_ref[hd, rows, :] = 0.5 * jnp.exp(s1 - hs_ref[hd, _HS_TOP1:_HS_TOP1 + 1, :])
            sa_ref[hd, rows, :] = theta
            return carry

        lax.fori_loop(0, PEER_HEADS * (PEER_NKEYS // rb), staircase, 0)

    ht_ref[...] = _dot_nt(u_ref[...], hn_ref[...])
    a_rows = pl.ds(pl.multiple_of(e * na, na), na)
    for hd in range(PEER_HEADS):
        ca_ref[hd] = sa_ref[hd, a_rows, :]
        ea_ref[hd] = e1_ref[hd, a_rows, :]

    ga = 4
    gb = 4
    n_bg = PEER_NKEYS // (gb * SUBLANES)

    def gate(idx, carry):
        cols = pl.ds(pl.multiple_of((idx // n_bg) * LANES, LANES), LANES)
        b0 = pl.multiple_of((idx % n_bg) * (gb * SUBLANES), gb * SUBLANES)
        bc = lambda ref, hd, r: jnp.broadcast_to(ref[hd, r:r + 1, cols], (SUBLANES, LANES))
        for a_blk in range(0, na, ga):
            w = [[jnp.zeros((SUBLANES, LANES), F32) for _ in range(gb)] for _ in range(ga)]
            for hd in range(PEER_HEADS):
                s2 = [sb_ref[hd, pl.ds(b0 + t * SUBLANES, SUBLANES), cols] for t in range(gb)]
                e2 = [e2_ref[hd, pl.ds(b0 + t * SUBLANES, SUBLANES), cols] for t in range(gb)]
                for i in range(ga):
                    theta_b = bc(ca_ref, hd, a_blk + i)
                    e1_b = bc(ea_ref, hd, a_blk + i)
                    for t in range(gb):
                        w[i][t] = w[i][t] + jnp.where(s2[t] >= theta_b, e2[t], 0.0) * e1_b
            for i in range(ga):
                for t in range(0, gb, 2):
                    rows = pl.ds(pl.multiple_of((a_blk + i) * PEER_NKEYS + b0 + t * SUBLANES, PACK), PACK)
                    hx = ht_ref[rows, cols]
                    act = hx * (1.0 + lax.erf(hx * (2.0 ** -0.5)))
                    g_ref[rows, cols] = (jnp.concatenate([w[i][t], w[i][t + 1]], axis=0) * act).astype(BF16)
        return carry

    lax.fori_loop(0, nt * n_bg, gate, 0)
    acc_ref[...] += _dot(vt_ref[...], g_ref[...])

    @pl.when(e == pl.num_programs(1) - 1)
    def _finish():
        y_ref[...] = _rms(h_ref[...] + acc_ref[...].T, nfin_ref[...])


def _peer(h2d, nf, nfin, wq_t, k1, k2, u, vt, tn, ec):
    n = h2d.shape[0]
    assert ec % (SUBLANES * PEER_NKEYS) == 0 and tn % LANES == 0
    assert ec >= PEER_HEADS * PEER_QDIM
    tok = pl.BlockSpec((tn, D_MODEL), lambda i, e: (i, 0))
    const2 = lambda i, e: (0, 0)
    const3 = lambda i, e: (0, 0, 0)
    vec = pl.BlockSpec((1, D_MODEL), const2)
    kspec = pl.BlockSpec((PEER_HEADS, PEER_NKEYS, PEER_HALF), const3)
    head_f32 = pltpu.VMEM((PEER_HEADS, PEER_NKEYS, tn), F32)
    rank_tok = pltpu.VMEM((PEER_TOPK, PEER_HEADS, tn), F32)
    stage = pltpu.VMEM((PEER_HEADS, ec // PEER_NKEYS, tn), F32)
    return pl.pallas_call(
        functools.partial(_peer_kernel, tn=tn, ec=ec),
        grid=(n // tn, PEER_N_EXPERTS // ec),
        in_specs=[tok, vec, vec,
                  pl.BlockSpec((PEER_HEADS * PEER_QDIM, D_MODEL), const2, pipeline_mode=pl.Buffered(1)),
                  kspec, kspec,
                  pl.BlockSpec((ec, D_MODEL), lambda i, e: (e, 0)),
                  pl.BlockSpec((D_MODEL, ec), lambda i, e: (0, e))],
        out_specs=tok,
        out_shape=jax.ShapeDtypeStruct((n, D_MODEL), F32),
        scratch_shapes=[pltpu.VMEM((tn, D_MODEL), BF16),
                        head_f32, head_f32, head_f32, head_f32, rank_tok, rank_tok,
                        pltpu.VMEM((PEER_HEADS, _HS_ROWS, tn), F32),
                        stage, stage, pltpu.VMEM((ec, tn), F32), pltpu.VMEM((ec, tn), BF16),
                        pltpu.VMEM((D_MODEL, tn), F32)],
        compiler_params=pltpu.CompilerParams(
            dimension_semantics=("parallel", "arbitrary"), vmem_limit_bytes=VMEM_LIMIT),
        name="peer",
    )(h2d, nf, nfin, wq_t, k1, k2, u, vt)


def _pack_params(norm_mix, w_in, gla_w_decay, gla_b_decay, gla_norm, swa_sinks, w_branch_a, w_branch_b,
                 w_out, norm_ffn, peer_w_q, peer_keys1, peer_keys2, peer_u, peer_v):
    off = np.cumsum((0, GLA_QK, GLA_QK, GLA_V, GLA_V, GLA_RANK, SWA_Q, SWA_KV, SWA_KV, D_MODEL, D_MODEL))
    col = lambda i: w_in[:, off[i]:off[i + 1]]
    lr = jnp.pad(col(4), ((0, 0), (0, LANES - GLA_RANK)))
    w_all = jnp.concatenate([col(0), col(1), col(2), col(3), col(5), col(6), col(7), col(8), col(9), lr],
                            axis=1).astype(BF16)
    wdec = jnp.pad(gla_w_decay, ((0, LANES - GLA_RANK), (0, 0))).astype(BF16)
    return dict(
        nm=norm_mix[None], w_all=w_all, wdec=wdec, bdec=gla_b_decay[None], gn=gla_norm[None],
        sinks=swa_sinks, wa=w_branch_a.astype(BF16), wb=w_branch_b.astype(BF16), wo=w_out.astype(BF16),
        nf=norm_ffn[None], wq_t=peer_w_q.T.astype(BF16), k1=peer_keys1.astype(BF16),
        k2=peer_keys2.astype(BF16), u=peer_u.astype(BF16), vt=peer_v.T.astype(BF16))


def _layer(x, p, nfin, s0, win_k, win_v, *, tn, act_dtype, gla_tb, gla_bb, swa_nb, swa_bb, peer_tn, peer_ec):
    bsz, t, _ = x.shape
    x2d = x.reshape(bsz * t, D_MODEL)
    gq, gk, gv, gg, la, sq, sk, sv, ga, gb = _inproj(x2d, p["nm"], p["w_all"], p["wdec"], p["bdec"], tn, act_dtype)
    r3 = lambda a: a.reshape(bsz, t, a.shape[-1])
    o_a, s_new = _gla(r3(gq), r3(gk), r3(gv), r3(la), s0, gla_tb, gla_bb)
    if win_k is None:
        o_b = _swa(r3(sq), r3(sk), r3(sv), p["sinks"], nb=swa_nb, bb=swa_bb)
        ln = min(WINDOW, t)
        new_k, new_v = r3(sk)[:, t - ln:], r3(sv)[:, t - ln:]
    else:
        wk = win_k.reshape(bsz, WINDOW, SWA_KV)
        wv = win_v.reshape(bsz, WINDOW, SWA_KV)
        o_b = _swa(r3(sq), r3(sk), r3(sv), p["sinks"], wk, wv, bb=swa_bb)
        new_k = jnp.concatenate([wk, r3(sk)], axis=1)[:, -WINDOW:]
        new_v = jnp.concatenate([wv, r3(sv)], axis=1)[:, -WINDOW:]
    h = _merge(x2d, o_a.reshape(bsz * t, GLA_V), gg, o_b.reshape(bsz * t, SWA_Q), ga, gb,
               p["gn"], p["wa"], p["wb"], p["wo"], tn)
    y = _peer(h, p["nf"], nfin, p["wq_t"], p["k1"], p["k2"], p["u"], p["vt"], peer_tn, peer_ec)
    kv_shape = (bsz, -1, SWA_KV_HEADS, SWA_HEAD_DIM)
    return y.reshape(bsz, t, D_MODEL), new_k.reshape(kv_shape), new_v.reshape(kv_shape), s_new


def kernel(x_prompt, x_sample, cache_win_k, cache_win_v, state_gla, norm_mix, w_in, gla_w_decay, gla_b_decay,
           gla_norm, swa_sinks, w_branch_a, w_branch_b, w_out, norm_ffn, peer_w_q, peer_keys1, peer_keys2,
           peer_u, peer_v, norm_final):
    depth = w_in.shape[0]
    assert depth == 1, "the final norm is fused into the last layer's channel mixer; one layer supported"
    p = _pack_params(norm_mix[0], w_in[0], gla_w_decay[0], gla_b_decay[0], gla_norm[0], swa_sinks[0],
                     w_branch_a[0], w_branch_b[0], w_out[0], norm_ffn[0], peer_w_q[0], peer_keys1[0],
                     peer_keys2[0], peer_u[0], peer_v[0])
    nfin = norm_final[None]
    yp, pk, pv, ps = _layer(x_prompt, p, nfin, None, None, None,
                            tn=512, act_dtype=BF16, gla_tb=512, gla_bb=1, swa_nb=4, swa_bb=1, peer_tn=512, peer_ec=2048)
    ys, sk, sv, ss = _layer(x_sample, p, nfin, state_gla[0], cache_win_k[0], cache_win_v[0],
                            tn=512, act_dtype=F32, gla_tb=x_sample.shape[1], gla_bb=4, swa_nb=1, swa_bb=4, peer_tn=512, peer_ec=2048)
    return (yp, ys, pk[None], pv[None], ps[None], sk[None], sv[None], ss[None])
```

```python
import functools

import jax
import jax.numpy as jnp
import numpy as np
from jax import lax
from jax.experimental import pallas as pl
from jax.experimental.pallas import tpu as pltpu

F32 = jnp.float32
BF16 = jnp.bfloat16

D_MODEL = 1024
GLA_HEADS = 4
GLA_DK = 128
GLA_DV = 256
GLA_RANK = 16
GLA_TAU = 16.0
SWA_HEADS = 16
SWA_KV_HEADS = 2
SWA_GROUP = SWA_HEADS // SWA_KV_HEADS
SWA_HEAD_DIM = 64
WINDOW = 128
PEER_HEADS = 8
PEER_NKEYS = 128
PEER_N_EXPERTS = PEER_NKEYS * PEER_NKEYS
PEER_QDIM = 256
PEER_HALF = PEER_QDIM // 2
PEER_TOPK = 16
EPS = 1e-6

GLA_QK = GLA_HEADS * GLA_DK
GLA_V = GLA_HEADS * GLA_DV
SWA_Q = SWA_HEADS * SWA_HEAD_DIM
SWA_KV = SWA_KV_HEADS * SWA_HEAD_DIM

LANES = 128
SUBLANES = 8
GLA_CHUNK = 128
GLA_HEAD_GROUP = 4
GLA_SAFE_DECAY = 60.0
VMEM_LIMIT = 56 * 1024 * 1024
NEG_BIG = -1e30

_NT = (((1,), (1,)), ((), ()))


def _rms(x, g):
    return x * lax.rsqrt(jnp.mean(x * x, axis=-1, keepdims=True) + EPS) * g


def _dot(a, b):
    return jnp.dot(a, b, preferred_element_type=F32)


def _dot_nt(a, b):
    return lax.dot_general(a, b, _NT, preferred_element_type=F32)


def _sigmoid(x):
    return 1.0 / (1.0 + jnp.exp(-x))


_P_GQ, _P_GK, _P_GV, _P_GG, _P_SQ, _P_SK, _P_SV, _P_GA, _P_GB, _P_LR, _P_END = (
    0, 512, 1024, 2048, 3072, 4096, 4224, 4352, 5376, 6400, 6528)


def _inproj_kernel(x_ref, nm_ref, w_ref, wdec_ref, bdec_ref,
                   gq_ref, gk_ref, gv_ref, gg_ref, la_ref, sq_ref, sk_ref, sv_ref, ga_ref, gb_ref):
    xb = _rms(x_ref[...], nm_ref[...]).astype(BF16)

    def proj(lo, hi):
        return _dot(xb, w_ref[:, lo:hi])

    gq_ref[...] = proj(_P_GQ, _P_GK)
    gk_ref[...] = proj(_P_GK, _P_GV)
    gv_ref[...] = proj(_P_GV, _P_GG).astype(gv_ref.dtype)
    gg_ref[...] = proj(_P_GG, _P_SQ).astype(gg_ref.dtype)
    sq_ref[...] = proj(_P_SQ, _P_SK).astype(sq_ref.dtype)
    sk_ref[...] = proj(_P_SK, _P_SV)
    sv_ref[...] = proj(_P_SV, _P_GA)
    ga_ref[...] = proj(_P_GA, _P_GB).astype(ga_ref.dtype)
    gb_ref[...] = proj(_P_GB, _P_LR).astype(gb_ref.dtype)
    glr = proj(_P_LR, _P_END)
    z = _dot(glr.astype(BF16), wdec_ref[...]) + bdec_ref[...]
    la_ref[...] = (jnp.minimum(z, 0.0) - jnp.log(1.0 + jnp.exp(-jnp.abs(z)))) * (1.0 / GLA_TAU)


def _inproj(x2d, nm, w_all, wdec, bdec, tn, act_dtype):
    n = x2d.shape[0]
    widths = (GLA_QK, GLA_QK, GLA_V, GLA_V, GLA_QK, SWA_Q, SWA_KV, SWA_KV, D_MODEL, D_MODEL)
    dtypes = (F32, F32, act_dtype, act_dtype, F32, act_dtype, F32, F32, act_dtype, act_dtype)
    const = lambda i: (0, 0)
    return pl.pallas_call(
        _inproj_kernel,
        grid=(n // tn,),
        in_specs=[
            pl.BlockSpec((tn, D_MODEL), lambda i: (i, 0)),
            pl.BlockSpec((1, D_MODEL), const),
            pl.BlockSpec((D_MODEL, _P_END), const, pipeline_mode=pl.Buffered(1)),
            pl.BlockSpec((LANES, GLA_QK), const),
            pl.BlockSpec((1, GLA_QK), const),
        ],
        out_specs=[pl.BlockSpec((tn, w), lambda i: (i, 0)) for w in widths],
        out_shape=[jax.ShapeDtypeStruct((n, w), d) for w, d in zip(widths, dtypes)],
        compiler_params=pltpu.CompilerParams(
            dimension_semantics=("parallel",), vmem_limit_bytes=VMEM_LIMIT),
        name="in_proj",
    )(x2d, nm, w_all, wdec, bdec)


def _gla_kernel(*refs, bb, n_chunks, t_valid, has_s0):
    if has_s0:
        q_ref, k_ref, v_ref, la_ref, s0_ref, o_ref, so_ref, s_ref = refs
    else:
        q_ref, k_ref, v_ref, la_ref, o_ref, so_ref, s_ref = refs
    j = pl.program_id(1)
    c = GLA_CHUNK

    @pl.when(j == 0)
    def _():
        if has_s0:
            s_ref[...] = s0_ref[...]
        else:
            s_ref[...] = jnp.zeros_like(s_ref)

    row = lax.broadcasted_iota(jnp.int32, (c, c), 0)
    col = lax.broadcasted_iota(jnp.int32, (c, c), 1)
    causal = row >= col
    tril = causal.astype(BF16)
    scale = GLA_DK ** -0.5

    def load(ref, bi, r0, lo, w):
        if t_valid >= c:
            return ref[bi, r0:r0 + c, lo:lo + w]
        t = ref[bi, :, lo:lo + w]
        return jnp.concatenate([t, jnp.zeros((c - t_valid, w), t.dtype)], axis=0)

    def pairwise_scores(qs, k, b):
        def score_row(i, a_t):
            pick = row == i
            q_i = jnp.sum(jnp.where(pick, qs, 0.0), axis=0, keepdims=True)
            b_i = jnp.sum(jnp.where(pick, b, 0.0), axis=0, keepdims=True)
            rel = jnp.where(row <= i, b_i - b, -jnp.inf)
            s_col = jnp.sum(k * jnp.exp(rel) * q_i, axis=1, keepdims=True)
            return jnp.where(col == i, s_col, a_t)
        return lax.fori_loop(0, c, score_row, jnp.zeros((c, c), F32)).T

    def run(factored):
        pairs = [(bi, h) for bi in range(bb) for h in range(GLA_HEADS)]
        groups = [pairs[g0:g0 + GLA_HEAD_GROUP] for g0 in range(0, len(pairs), GLA_HEAD_GROUP)]
        for ci in range(n_chunks):
            r0 = ci * c
            for hs in groups:
                q = [load(q_ref, bi, r0, h * GLA_DK, GLA_DK) for bi, h in hs]
                k = [load(k_ref, bi, r0, h * GLA_DK, GLA_DK) for bi, h in hs]
                v = [load(v_ref, bi, r0, h * GLA_DV, GLA_DV).astype(BF16) for bi, h in hs]
                la = [load(la_ref, bi, r0, h * GLA_DK, GLA_DK) for bi, h in hs]
                la_hi = [x.astype(BF16) for x in la]
                la_lo = [(x - y.astype(F32)).astype(BF16) for x, y in zip(la, la_hi)]
                b2 = [_dot(tril, jnp.concatenate([x, y], axis=1)) for x, y in zip(la_hi, la_lo)]
                b = [x[:, :GLA_DK] + x[:, GLA_DK:] for x in b2]
                qs = [x * scale for x in q]
                qd = [(x * jnp.exp(y)).astype(BF16) for x, y in zip(qs, b)]
                if factored:
                    kn = [(x * jnp.exp(-y)).astype(BF16) for x, y in zip(k, b)]
                    a = [jnp.where(causal, _dot_nt(x, y), 0.0) for x, y in zip(qd, kn)]
                else:
                    a = [pairwise_scores(x, y, z) for x, y, z in zip(qs, k, b)]
                s_old = [s_ref[bi, h] for bi, h in hs]
                o = [_dot(jnp.concatenate([x, y.astype(BF16)], axis=1), jnp.concatenate([s.astype(BF16), z], axis=0))
                     for x, s, y, z in zip(qd, s_old, a, v)]
                for (bi, h), x in zip(hs, o):
                    if t_valid >= c:
                        o_ref[bi, r0:r0 + c, h * GLA_DV:(h + 1) * GLA_DV] = x.astype(o_ref.dtype)
                    else:
                        o_ref[bi, :, h * GLA_DV:(h + 1) * GLA_DV] = x[:t_valid].astype(o_ref.dtype)
                bl = [y[c - 1:c, :] for y in b]
                kd_t = [(x * jnp.exp(l - y)).T.astype(BF16) for x, l, y in zip(k, bl, b)]
                dec_t = [jnp.broadcast_to(jnp.exp(l), (c, GLA_DK)).T for l in bl]
                for (bi, h), d, s, x, z in zip(hs, dec_t, s_old, kd_t, v):
                    s_ref[bi, h] = jnp.concatenate([d] * (GLA_DV // c), axis=1) * s + _dot(x, z)

    worst = jnp.float32(0.0)
    for bi in range(bb):
        for ci in range(n_chunks):
            rows = slice(ci * c, (ci + 1) * c) if t_valid >= c else slice(None)
            worst = jnp.maximum(worst, jnp.max(-jnp.sum(la_ref[bi, rows, :], axis=0, keepdims=True)))
    safe = worst < GLA_SAFE_DECAY
    pl.when(safe)(lambda: run(True))
    pl.when(jnp.logical_not(safe))(lambda: run(False))

    @pl.when(j == pl.num_programs(1) - 1)
    def _():
        so_ref[...] = s_ref[...]


def _gla(gq, gk, gv, la, s0, tb, bb):
    bsz, t, _ = gq.shape
    t_valid = min(tb, GLA_CHUNK)
    n_chunks = max(tb // GLA_CHUNK, 1)
    has_s0 = s0 is not None
    tok = lambda w: pl.BlockSpec((bb, tb, w), lambda b, j: (b, j, 0))
    st = pl.BlockSpec((bb, GLA_HEADS, GLA_DK, GLA_DV), lambda b, j: (b, 0, 0, 0))
    in_specs = [tok(GLA_QK), tok(GLA_QK), tok(GLA_V), tok(GLA_QK)] + ([st] if has_s0 else [])
    args = (gq, gk, gv, la) + ((s0,) if has_s0 else ())
    return pl.pallas_call(
        functools.partial(_gla_kernel, bb=bb, n_chunks=n_chunks, t_valid=t_valid, has_s0=has_s0),
        grid=(bsz // bb, t // tb),
        in_specs=in_specs,
        out_specs=[tok(GLA_V), st],
        out_shape=[jax.ShapeDtypeStruct((bsz, t, GLA_V), gv.dtype),
                   jax.ShapeDtypeStruct((bsz, GLA_HEADS, GLA_DK, GLA_DV), F32)],
        scratch_shapes=[pltpu.VMEM((bb, GLA_HEADS, GLA_DK, GLA_DV), F32)],
        compiler_params=pltpu.CompilerParams(
            dimension_semantics=("parallel", "arbitrary"), vmem_limit_bytes=VMEM_LIMIT),
        name="gla",
    )(*args)


def _swa_kernel(sink_ref, q_ref, kc_ref, vc_ref, kp_ref, vp_ref, o_ref, s_ref, p_ref, *, bb, tq, nb, prev_is_block):
    i = pl.program_id(1)
    w = WINDOW
    nk = 2 * w
    pairs = SWA_GROUP // 2
    t_idx = lax.broadcasted_iota(jnp.int32, (tq, nk), 0)
    j_idx = lax.broadcasted_iota(jnp.int32, (tq, nk), 1)
    d = j_idx - t_idx
    band = (d >= 1) & (d <= w)
    lane = lax.broadcasted_iota(jnp.int32, (nk, LANES), 1)
    left = lane < SWA_HEAD_DIM

    def window(prev_ref, cur_ref, bi, blk):
        cur = cur_ref[bi, blk * tq:(blk + 1) * tq, :]
        if tq < w:
            cur = jnp.concatenate([cur, jnp.zeros((w - tq, SWA_KV), F32)], axis=0)
        prev = prev_ref[bi] if blk == 0 else cur_ref[bi, (blk - 1) * tq:blk * tq, :]
        full = jnp.concatenate([prev, cur], axis=0)
        return full, pltpu.roll(full, SWA_HEAD_DIM, 1)

    def block_diag(full, rot, kvh):
        a, b = (full, rot) if kvh == 0 else (rot, full)
        return jnp.concatenate([jnp.where(left, a, 0.0), jnp.where(left, 0.0, b)], axis=0).astype(BF16)

    def mask(blk):
        if prev_is_block and blk == 0:
            return band & ((j_idx >= w) | (i > 0))
        return band

    units = [((bi, blk), kvh) for bi in range(bb) for blk in range(nb) for kvh in range(SWA_KV_HEADS)]
    ug = s_ref.shape[0]
    win, vis = {}, {}
    for g0 in range(0, len(units), ug):
        grp = units[g0:g0 + ug]
        for blk, _ in grp:
            if blk not in win:
                win[blk] = (window(kp_ref, kc_ref, *blk), window(vp_ref, vc_ref, *blk))
                vis[blk] = mask(blk[1])
        kbd = [block_diag(*win[blk][0], kvh) for blk, kvh in grp]
        vbd = [block_diag(*win[blk][1], kvh) for blk, kvh in grp]
        base = [kvh * SWA_GROUP * SWA_HEAD_DIM for _, kvh in grp]
        rows = [(blk[0], slice(blk[1] * tq, (blk[1] + 1) * tq)) for blk, _ in grp]
        for gi, ((bi, r), b0) in enumerate(zip(rows, base)):
            qs = jnp.concatenate([q_ref[bi, r, b0 + p * LANES: b0 + (p + 1) * LANES] for p in range(pairs)], axis=0)
            s_ref[gi] = _dot_nt((qs * (SWA_HEAD_DIM ** -0.5)).astype(BF16), kbd[gi])
        for p in range(pairs):
            for e in range(2):
                pr, seg = slice(p * tq, (p + 1) * tq), slice(e * nk, (e + 1) * nk)
                se = [jnp.where(vis[blk], s_ref[gi, pr, seg], NEG_BIG) for gi, (blk, _) in enumerate(grp)]
                sink = [sink_ref[kvh * SWA_GROUP + 2 * p + e] for _, kvh in grp]
                m = [jnp.maximum(jnp.max(x, axis=-1, keepdims=True), y) for x, y in zip(se, sink)]
                ex = [jnp.exp(x - y) for x, y in zip(se, m)]
                den = [jnp.sum(x, axis=-1, keepdims=True) + jnp.exp(y - z) for x, y, z in zip(ex, sink, m)]
                for gi, (x, y) in enumerate(zip(ex, den)):
                    p_ref[gi, pr, seg] = (x / y).astype(BF16)
        for gi, ((bi, r), b0) in enumerate(zip(rows, base)):
            o = _dot(p_ref[gi], vbd[gi])
            for p in range(pairs):
                o_ref[bi, r, b0 + p * LANES: b0 + (p + 1) * LANES] = o[p * tq:(p + 1) * tq].astype(o_ref.dtype)


def _swa(sq, sk, sv, sinks, prev_k=None, prev_v=None, nb=1, bb=1):
    bsz, t, _ = sq.shape
    prev_is_block = prev_k is None
    if prev_is_block:
        tq = WINDOW
        prev_k, prev_v = sk, sv
        prev_spec = pl.BlockSpec((bb, WINDOW, SWA_KV), lambda b, i: (b, jnp.maximum(i * nb - 1, 0), 0))
    else:
        tq = t
        assert nb == 1
        prev_spec = pl.BlockSpec((bb, WINDOW, SWA_KV), lambda b, i: (b, 0, 0))
    ug = 2 * SWA_KV_HEADS
    rows = (SWA_GROUP // 2) * tq
    cur_spec = pl.BlockSpec((bb, nb * tq, SWA_KV), lambda b, i: (b, i, 0))
    q_spec = pl.BlockSpec((bb, nb * tq, SWA_Q), lambda b, i: (b, i, 0))
    return pl.pallas_call(
        functools.partial(_swa_kernel, bb=bb, tq=tq, nb=nb, prev_is_block=prev_is_block),
        grid=(bsz // bb, t // (nb * tq)),
        in_specs=[pl.BlockSpec(memory_space=pltpu.SMEM), q_spec, cur_spec, cur_spec, prev_spec, prev_spec],
        out_specs=q_spec,
        out_shape=jax.ShapeDtypeStruct((bsz, t, SWA_Q), sq.dtype),
        scratch_shapes=[pltpu.VMEM((ug, rows, 4 * WINDOW), F32), pltpu.VMEM((ug, rows, 4 * WINDOW), BF16)],
        compiler_params=pltpu.CompilerParams(
            dimension_semantics=("parallel", "arbitrary"), vmem_limit_bytes=VMEM_LIMIT),
        name="swa",
    )(sinks, sq, sk, sv, prev_k, prev_v)


def _merge_kernel(x_ref, oa_ref, gg_ref, ob_ref, ga_ref, gb_ref, gn_ref, wa_ref, wb_ref, wo_ref, h_ref):
    parts = []
    for hd in range(GLA_HEADS):
        sl = slice(hd * GLA_DV, (hd + 1) * GLA_DV)
        g = gg_ref[:, sl].astype(F32)
        parts.append(_rms(oa_ref[:, sl].astype(F32), gn_ref[...]) * (g * _sigmoid(g)))
    oa = jnp.concatenate(parts, axis=1).astype(BF16)
    br_a = _dot(oa, wa_ref[...])
    br_b = _dot(ob_ref[...].astype(BF16), wb_ref[...])
    merged = _sigmoid(ga_ref[...].astype(F32)) * br_a + _sigmoid(gb_ref[...].astype(F32)) * br_b
    h_ref[...] = x_ref[...] + _dot(merged.astype(BF16), wo_ref[...])


def _merge(x2d, oa, gg, ob, ga, gb, gn, wa, wb, wo, tn):
    n = x2d.shape[0]
    tok = pl.BlockSpec((tn, D_MODEL), lambda i: (i, 0))
    const = lambda i: (0, 0)
    wspec = pl.BlockSpec((D_MODEL, D_MODEL), const)
    return pl.pallas_call(
        _merge_kernel,
        grid=(n // tn,),
        in_specs=[tok] * 6 + [pl.BlockSpec((1, GLA_DV), const), wspec, wspec, wspec],
        out_specs=tok,
        out_shape=jax.ShapeDtypeStruct((n, D_MODEL), F32),
        compiler_params=pltpu.CompilerParams(
            dimension_semantics=("parallel",), vmem_limit_bytes=VMEM_LIMIT),
        name="merge",
    )(x2d, oa, gg, ob, ga, gb, gn, wa, wb, wo)


def _sort16_pairs():
    pairs = []
    n = 16
    p = 1
    while p < n:
        k = p
        while k >= 1:
            for j in range(k % p, n - k, 2 * k):
                for i in range(min(k, n - j - k)):
                    if (i + j) // (2 * p) == (i + j + k) // (2 * p):
                        pairs.append((i + j, i + j + k))
            k //= 2
        p *= 2
    return pairs


_SORT16 = _sort16_pairs()


def _bitonic_top16(xa, xb):
    z = [jnp.maximum(xa[i], xb[15 - i]) for i in range(16)]
    dist = 8
    while dist >= 1:
        for i in range(16):
            if i & dist == 0:
                hi = jnp.maximum(z[i], z[i + dist])
                lo = jnp.minimum(z[i], z[i + dist])
                z[i], z[i + dist] = hi, lo
        dist //= 2
    return z


def _top16_rows(s):
    x = [s[SUBLANES * v:SUBLANES * (v + 1), :] for v in range(16)]
    for i, j in _SORT16:
        hi = jnp.maximum(x[i], x[j])
        lo = jnp.minimum(x[i], x[j])
        x[i], x[j] = hi, lo
    shift = 1
    while shift < SUBLANES:
        x = _bitonic_top16(x, [pltpu.roll(t, shift, 0) for t in x])
        shift *= 2
    return [t[SUBLANES - 1:SUBLANES, :] for t in x]


_CAND_Q = [PEER_TOPK // (p + 1) for p in range(PEER_TOPK)]


PACK = 2 * SUBLANES
_HS_TAU, _HS_ZINV, _HS_TOP1, _HS_ROWS = PEER_TOPK, PEER_TOPK + 1, PEER_TOPK + 2, PEER_TOPK + SUBLANES


def _peer_kernel(h_ref, nf_ref, nfin_ref, wq_ref, kk_ref, u_ref, vt_ref, y_ref,
                 hn_ref, sa_ref, sb_ref, e1_ref, e2_ref, v1_ref, v2_ref, hs_ref,
                 ca_ref, ea_ref, ht_ref, g_ref, acc_ref, *, tn, ec):
    e = pl.program_id(1)
    na = ec // PEER_NKEYS
    nt = tn // LANES
    rb = 4 * SUBLANES

    @pl.when(e == 0)
    def _routing():
        hn_ref[...] = _rms(h_ref[...], nf_ref[...]).astype(BF16)
        acc_ref[...] = jnp.zeros_like(acc_ref)

        ht_ref[0:PEER_HEADS * PEER_QDIM, :] = _dot_nt(wq_ref[...], hn_ref[...])
        for hd in range(PEER_HEADS):
            qt = ht_ref[hd * PEER_QDIM:(hd + 1) * PEER_QDIM, :]
            s12 = _dot(kk_ref[hd], qt.astype(BF16))
            s1, s2 = s12[:PEER_NKEYS], s12[PEER_NKEYS:]
            sa_ref[hd] = s1
            sb_ref[hd] = s2
            for p, r in enumerate(_top16_rows(s1)):
                v1_ref[p, hd:hd + 1, :] = r
            for p, r in enumerate(_top16_rows(s2)):
                v2_ref[p, hd:hd + 1, :] = r
                hs_ref[hd, p:p + 1, :] = r

        v1 = [v1_ref[p] for p in range(PEER_TOPK)]
        v2 = [v2_ref[q] for q in range(PEER_TOPK)]
        ninf = jnp.full((PEER_HEADS, tn), -jnp.inf, F32)
        best = [v1[0] + v2[q] for q in range(PEER_TOPK)]
        for p in range(1, PEER_TOPK):
            row = [v1[p] + v2[q] if q < _CAND_Q[p] else ninf for q in range(PEER_TOPK)]
            best = _bitonic_top16(best, row)
        tau = best[PEER_TOPK - 1]
        top = best[0]
        zsum = jnp.zeros((PEER_HEADS, tn), F32)
        for p in range(PEER_TOPK):
            for q in range(_CAND_Q[p]):
                cnd = v1[p] + v2[q]
                zsum = zsum + jnp.where(cnd >= tau, jnp.exp(cnd - top), 0.0)
        zinv = 1.0 / zsum
        for hd in range(PEER_HEADS):
            hs_ref[hd, _HS_TAU:_HS_TAU + 1, :] = tau[hd:hd + 1]
            hs_ref[hd, _HS_ZINV:_HS_ZINV + 1, :] = zinv[hd:hd + 1]
            hs_ref[hd, _HS_TOP1:_HS_TOP1 + 1, :] = v1[0][hd:hd + 1]

        def staircase(idx, carry):
            hd = idx // (PEER_NKEYS // rb)
            rows = pl.ds(pl.multiple_of((idx % (PEER_NKEYS // rb)) * rb, rb), rb)
            s1 = sa_ref[hd, rows, :]
            tau_h = hs_ref[hd, _HS_TAU:_HS_TAU + 1, :]
            theta = jnp.full((rb, tn), jnp.inf, F32)
            for q in range(PEER_TOPK):
                v2q = hs_ref[hd, q:q + 1, :]
                theta = jnp.where(s1 + v2q >= tau_h, v2q, theta)
            e2_ref[hd, rows, :] = (jnp.exp(sb_ref[hd, rows, :] - hs_ref[hd, 0:1, :])
                                   * hs_ref[hd, _HS_ZINV:_HS_ZINV + 1, :])
            e1_ref[hd, rows, :] = 0.5 * jnp.exp(s1 - hs_ref[hd, _HS_TOP1:_HS_TOP1 + 1, :])
            sa_ref[hd, rows, :] = theta
            return carry

        lax.fori_loop(0, PEER_HEADS * (PEER_NKEYS // rb), staircase, 0)

    ht_ref[...] = _dot_nt(u_ref[...], hn_ref[...])
    a_rows = pl.ds(pl.multiple_of(e * na, na), na)
    for hd in range(PEER_HEADS):
        ca_ref[hd] = sa_ref[hd, a_rows, :]
        ea_ref[hd] = e1_ref[hd, a_rows, :]

    ga = 4
    gb = 4
    n_bg = PEER_NKEYS // (gb * SUBLANES)

    def gate(idx, carry):
        cols = pl.ds(pl.multiple_of((idx // n_bg) * LANES, LANES), LANES)
        b0 = pl.multiple_of((idx % n_bg) * (gb * SUBLANES), gb * SUBLANES)
        bc = lambda ref, hd, r: jnp.broadcast_to(ref[hd, r:r + 1, cols], (SUBLANES, LANES))
        for a_blk in range(0, na, ga):
            w = [[jnp.zeros((SUBLANES, LANES), F32) for _ in range(gb)] for _ in range(ga)]
            for hd in range(PEER_HEADS):
                s2 = [sb_ref[hd, pl.ds(b0 + t * SUBLANES, SUBLANES), cols] for t in range(gb)]
                e2 = [e2_ref[hd, pl.ds(b0 + t * SUBLANES, SUBLANES), cols] for t in range(gb)]
                for i in range(ga):
                    theta_b = bc(ca_ref, hd, a_blk + i)
                    e1_b = bc(ea_ref, hd, a_blk + i)
                    for t in range(gb):
                        w[i][t] = w[i][t] + jnp.where(s2[t] >= theta_b, e2[t], 0.0) * e1_b
            for i in range(ga):
                for t in range(0, gb, 2):
                    rows = pl.ds(pl.multiple_of((a_blk + i) * PEER_NKEYS + b0 + t * SUBLANES, PACK), PACK)
                    hx = ht_ref[rows, cols]
                    act = hx * (1.0 + lax.erf(hx * (2.0 ** -0.5)))
                    g_ref[rows, cols] = (jnp.concatenate([w[i][t], w[i][t + 1]], axis=0) * act).astype(BF16)
        return carry

    lax.fori_loop(0, nt * n_bg, gate, 0)
    acc_ref[...] += _dot(vt_ref[...], g_ref[...])

    @pl.when(e == pl.num_programs(1) - 1)
    def _finish():
        y_ref[...] = _rms(h_ref[...] + acc_ref[...].T, nfin_ref[...])


def _peer(h2d, nf, nfin, wq_t, kk, u, vt, tn, ec):
    n = h2d.shape[0]
    assert ec % (SUBLANES * PEER_NKEYS) == 0 and tn % LANES == 0
    assert ec >= PEER_HEADS * PEER_QDIM
    tok = pl.BlockSpec((tn, D_MODEL), lambda i, e: (i, 0))
    const2 = lambda i, e: (0, 0)
    const3 = lambda i, e: (0, 0, 0)
    vec = pl.BlockSpec((1, D_MODEL), const2)
    kspec = pl.BlockSpec((PEER_HEADS, 2 * PEER_NKEYS, PEER_QDIM), const3)
    head_f32 = pltpu.VMEM((PEER_HEADS, PEER_NKEYS, tn), F32)
    rank_tok = pltpu.VMEM((PEER_TOPK, PEER_HEADS, tn), F32)
    stage = pltpu.VMEM((PEER_HEADS, ec // PEER_NKEYS, tn), F32)
    return pl.pallas_call(
        functools.partial(_peer_kernel, tn=tn, ec=ec),
        grid=(n // tn, PEER_N_EXPERTS // ec),
        in_specs=[tok, vec, vec,
                  pl.BlockSpec((PEER_HEADS * PEER_QDIM, D_MODEL), const2, pipeline_mode=pl.Buffered(1)),
                  kspec,
                  pl.BlockSpec((ec, D_MODEL), lambda i, e: (e, 0)),
                  pl.BlockSpec((D_MODEL, ec), lambda i, e: (0, e))],
        out_specs=tok,
        out_shape=jax.ShapeDtypeStruct((n, D_MODEL), F32),
        scratch_shapes=[pltpu.VMEM((tn, D_MODEL), BF16),
                        head_f32, head_f32, head_f32, head_f32, rank_tok, rank_tok,
                        pltpu.VMEM((PEER_HEADS, _HS_ROWS, tn), F32),
                        stage, stage, pltpu.VMEM((ec, tn), F32), pltpu.VMEM((ec, tn), BF16),
                        pltpu.VMEM((D_MODEL, tn), F32)],
        compiler_params=pltpu.CompilerParams(
            dimension_semantics=("parallel", "arbitrary"), vmem_limit_bytes=VMEM_LIMIT),
        name="peer",
    )(h2d, nf, nfin, wq_t, kk, u, vt)


def _pack_params(norm_mix, w_in, gla_w_decay, gla_b_decay, gla_norm, swa_sinks, w_branch_a, w_branch_b,
                 w_out, norm_ffn, peer_w_q, peer_keys1, peer_keys2, peer_u, peer_v):
    off = np.cumsum((0, GLA_QK, GLA_QK, GLA_V, GLA_V, GLA_RANK, SWA_Q, SWA_KV, SWA_KV, D_MODEL, D_MODEL))
    col = lambda i: w_in[:, off[i]:off[i + 1]]
    lr = jnp.pad(col(4), ((0, 0), (0, LANES - GLA_RANK)))
    w_all = jnp.concatenate([col(0), col(1), col(2), col(3), col(5), col(6), col(7), col(8), col(9), lr],
                            axis=1).astype(BF16)
    wdec = jnp.pad(gla_w_decay, ((0, LANES - GLA_RANK), (0, 0))).astype(BF16)
    kk = jnp.concatenate([jnp.pad(peer_keys1, ((0, 0), (0, 0), (0, PEER_HALF))),
                          jnp.pad(peer_keys2, ((0, 0), (0, 0), (PEER_HALF, 0)))], axis=1)
    return dict(
        nm=norm_mix[None], w_all=w_all, wdec=wdec, bdec=gla_b_decay[None], gn=gla_norm[None],
        sinks=swa_sinks, wa=w_branch_a.astype(BF16), wb=w_branch_b.astype(BF16), wo=w_out.astype(BF16),
        nf=norm_ffn[None], wq_t=peer_w_q.T.astype(BF16), kk=kk.astype(BF16),
        u=peer_u.astype(BF16), vt=peer_v.T.astype(BF16))


def _layer(x, p, nfin, s0, win_k, win_v, *, tn, act_dtype, gla_tb, gla_bb, swa_nb, swa_bb, peer_tn, peer_ec):
    bsz, t, _ = x.shape
    x2d = x.reshape(bsz * t, D_MODEL)
    gq, gk, gv, gg, la, sq, sk, sv, ga, gb = _inproj(x2d, p["nm"], p["w_all"], p["wdec"], p["bdec"], tn, act_dtype)
    r3 = lambda a: a.reshape(bsz, t, a.shape[-1])
    o_a, s_new = _gla(r3(gq), r3(gk), r3(gv), r3(la), s0, gla_tb, gla_bb)
    if win_k is None:
        o_b = _swa(r3(sq), r3(sk), r3(sv), p["sinks"], nb=swa_nb, bb=swa_bb)
        ln = min(WINDOW, t)
        new_k, new_v = r3(sk)[:, t - ln:], r3(sv)[:, t - ln:]
    else:
        wk = win_k.reshape(bsz, WINDOW, SWA_KV)
        wv = win_v.reshape(bsz, WINDOW, SWA_KV)
        o_b = _swa(r3(sq), r3(sk), r3(sv), p["sinks"], wk, wv, bb=swa_bb)
        new_k = jnp.concatenate([wk, r3(sk)], axis=1)[:, -WINDOW:]
        new_v = jnp.concatenate([wv, r3(sv)], axis=1)[:, -WINDOW:]
    h = _merge(x2d, o_a.reshape(bsz * t, GLA_V), gg, o_b.reshape(bsz * t, SWA_Q), ga, gb,
               p["gn"], p["wa"], p["wb"], p["wo"], tn)
    y = _peer(h, p["nf"], nfin, p["wq_t"], p["kk"], p["u"], p["vt"], peer_tn, peer_ec)
    kv_shape = (bsz, -1, SWA_KV_HEADS, SWA_HEAD_DIM)
    return y.reshape(bsz, t, D_MODEL), new_k.reshape(kv_shape), new_v.reshape(kv_shape), s_new


def kernel(x_prompt, x_sample, cache_win_k, cache_win_v, state_gla, norm_mix, w_in, gla_w_decay, gla_b_decay,
           gla_norm, swa_sinks, w_branch_a, w_branch_b, w_out, norm_ffn, peer_w_q, peer_keys1, peer_keys2,
           peer_u, peer_v, norm_final):
    depth = w_in.shape[0]
    assert depth == 1, "the final norm is fused into the last layer's channel mixer; one layer supported"
    p = _pack_params(norm_mix[0], w_in[0], gla_w_decay[0], gla_b_decay[0], gla_norm[0], swa_sinks[0],
                     w_branch_a[0], w_branch_b[0], w_out[0], norm_ffn[0], peer_w_q[0], peer_keys1[0],
                     peer_keys2[0], peer_u[0], peer_v[0])
    nfin = norm_final[None]
    yp, pk, pv, ps = _layer(x_prompt, p, nfin, None, None, None,
                            tn=512, act_dtype=BF16, gla_tb=512, gla_bb=1, swa_nb=4, swa_bb=1, peer_tn=512, peer_ec=2048)
    ys, sk, sv, ss = _layer(x_sample, p, nfin, state_gla[0], cache_win_k[0], cache_win_v[0],
                            tn=512, act_dtype=F32, gla_tb=x_sample.shape[1], gla_bb=4, swa_nb=1, swa_bb=4, peer_tn=512, peer_ec=2048)
    return (yp, ys, pk[None], pv[None], ps[None], sk[None], sv[None], ss[None])
```
